```python
import math
import jax, jax.numpy as jnp
from jax import lax
import numpy as np

D_MODEL = 2048
BATCH = 4
SEQ = 4096
DEPTH = 2

GRID_W = 64
CTX_LEN = 256
N_EVEN = (DEPTH + 1) // 2
N_ODD = DEPTH // 2
MIX_WIDTH = D_MODEL
EPS = 1e-6

A_DIM = 128
A_WIDTH = D_MODEL // 2
A_HEADS = A_WIDTH // A_DIM
A_CHUNK = 64
B_DIM = 128
B_WIDTH = MIX_WIDTH - A_WIDTH
B_HEADS = B_WIDTH // B_DIM
NA_ROWS = 8
NA_COLS = 16
EVEN_IN = 5 * A_WIDTH + 3 * B_WIDTH

S5_WIDTH = D_MODEL // 4
S5_GROUP = 16
S5_GROUPS = S5_WIDTH // S5_GROUP
S5_STATE = 64
C_DIM = 128
C_Q_WIDTH = MIX_WIDTH - S5_WIDTH
C_Q_HEADS = C_Q_WIDTH // C_DIM
C_KV_HEADS = C_Q_HEADS // 3
C_GROUP = C_Q_HEADS // C_KV_HEADS
C_KV_WIDTH = C_KV_HEADS * C_DIM
C_BLOCK = 128
ROPE_THETA = 10000.0
ODD_IN = C_Q_WIDTH + 2 * C_KV_WIDTH + S5_WIDTH

N_EXPERTS = 16
EXPERT_FF = D_MODEL
EC_CAPACITY = 2

kernel_name = "hybrid_hgrn2_natten_gqa_s5_ecmoe_dit"

F32 = jnp.float32


def _rmsnorm(x, g):
    xf = x.astype(F32)
    y = xf * lax.rsqrt(jnp.mean(xf * xf, axis=-1, keepdims=True) + EPS)
    return (y * g.astype(F32)).astype(x.dtype)


def _heads(a, h):
    return a.reshape(a.shape[:-1] + (h, a.shape[-1] // h))


def _gla_chunk_scan(q, k, v, logf, s0):
    bn, h, t, _ = q.shape
    n = t // A_CHUNK
    to_chunks = lambda a: jnp.moveaxis(a.reshape(bn, h, n, A_CHUNK, a.shape[-1]), 2, 0)
    causal = jnp.tril(jnp.ones((A_CHUNK, A_CHUNK), bool))

    def step(s, inp):
        qc, kc, vc, lc = inp
        b = jnp.cumsum(lc, axis=2)
        o_inter = jnp.einsum('bhtk,bhkv->bhtv', qc * jnp.exp(b), s)
        diff = b[:, :, :, None, :] - b[:, :, None, :, :]
        decay = jnp.exp(jnp.where(causal[:, :, None], diff, -jnp.inf))
        att = jnp.einsum('bhtk,bhsk,bhtsk->bhts', qc, kc, decay)
        o = o_inter + jnp.einsum('bhts,bhsv->bhtv', att, vc)
        b_end = b[:, :, -1:, :]
        s_new = jnp.exp(b_end[:, :, 0, :, None]) * s + jnp.einsum('bhsk,bhsv->bhkv', kc * jnp.exp(b_end - b), vc)
        return s_new, o

    s_fin, o = lax.scan(step, s0, (to_chunks(q), to_chunks(k), to_chunks(v), to_chunks(logf)))
    return jnp.moveaxis(o, 0, 2).reshape(bn, h, t, -1), s_fin


def _gla_dir(q, k, v, logf, s0, reverse):
    if not reverse:
        return _gla_chunk_scan(q, k, v, logf, s0)
    fl = lambda a: jnp.flip(a, axis=2)
    o, s = _gla_chunk_scan(fl(q), fl(k), fl(v), fl(logf), s0)
    return fl(o), s


def _hgrn2(p_lat, p_ctx, lb, g_norm):
    lb = lb.astype(F32)
    bhtk = lambda a: jnp.swapaxes(_heads(a, A_HEADS), 1, 2)

    def prep(p):
        q, ff, fb, i, g = jnp.split(p.astype(F32), 5, axis=-1)
        dirs = []
        for d, f_raw in enumerate((ff, fb)):
            f = lb[d] + (1.0 - lb[d]) * jax.nn.sigmoid(f_raw)
            dirs.append((bhtk(1.0 - f), bhtk(jnp.log(f))))
        return bhtk(q), bhtk(i), dirs, g

    q_c, v_c, dirs_c, g_c = prep(p_ctx)
    q_l, v_l, dirs_l, g_l = prep(p_lat)
    s0 = jnp.zeros((p_lat.shape[0], A_HEADS, A_DIM, A_DIM), F32)
    o_cf, s_cf = _gla_dir(q_c, dirs_c[0][0], v_c, dirs_c[0][1], s0, False)
    o_cb, s_cb = _gla_dir(q_c, dirs_c[1][0], v_c, dirs_c[1][1], s0, True)
    o_lf, _ = _gla_dir(q_l, dirs_l[0][0], v_l, dirs_l[0][1], s_cf, False)
    o_lb, _ = _gla_dir(q_l, dirs_l[1][0], v_l, dirs_l[1][1], s_cb, True)

    def readout(o, g):
        o = _rmsnorm(jnp.swapaxes(o, 1, 2), g_norm) * jax.nn.silu(_heads(g, A_HEADS))
        return o.reshape(o.shape[0], o.shape[1], A_WIDTH)

    return readout(o_lf + o_lb, g_l), readout(o_cf + o_cb, g_c)


def _softmax_attn(q, k, v):
    s = jnp.einsum('bqhd,bkhd->bhqk', q, k).astype(F32) * (q.shape[-1] ** -0.5)
    p = jax.nn.softmax(s, axis=-1).astype(v.dtype)
    return jnp.einsum('bhqk,bkhd->bqhd', p, v)


def _neighbourhood_attn(q, k, v, k_ctx, v_ctx, rpb):
    bn, t, h, d = q.shape
    rows = t // GRID_W
    wr = min(NA_ROWS, rows)
    scale = d ** -0.5
    qg = q.reshape(bn, rows, GRID_W, h, d)
    kg = k.reshape(bn, rows, GRID_W, h, d)
    vg = v.reshape(bn, rows, GRID_W, h, d)
    col = jnp.arange(GRID_W)
    col_start = jnp.clip(col - NA_COLS // 2, 0, GRID_W - NA_COLS)
    col_mask = (col[None, :] >= col_start[:, None]) & (col[None, :] < col_start[:, None] + NA_COLS)
    dc_idx = jnp.clip(col[None, :] - col[:, None] + NA_COLS - 1, 0, 2 * NA_COLS - 2)
    n_win = wr * GRID_W

    def row_step(inp):
        r, qr = inp
        rs = jnp.clip(r - wr // 2, 0, rows - wr)
        kb = lax.dynamic_slice_in_dim(kg, rs, wr, axis=1)
        vb = lax.dynamic_slice_in_dim(vg, rs, wr, axis=1)
        dr_idx = rs + jnp.arange(wr) - r + NA_ROWS - 1
        bias = rpb[:, dr_idx[:, None, None], dc_idx[None, :, :]].astype(F32)
        s_win = jnp.einsum('bqhd,brkhd->bhqrk', qr, kb).astype(F32) * scale + jnp.transpose(bias, (0, 2, 1, 3))[None]
        s_win = jnp.where(col_mask[None, None, :, None, :], s_win, -jnp.inf).reshape(bn, h, GRID_W, n_win)
        s_ctx = jnp.einsum('bqhd,blhd->bhql', qr, k_ctx).astype(F32) * scale
        p = jax.nn.softmax(jnp.concatenate([s_win, s_ctx], axis=-1), axis=-1).astype(v.dtype)
        p_win = p[..., :n_win].reshape(bn, h, GRID_W, wr, GRID_W)
        return (jnp.einsum('bhqrk,brkhd->bqhd', p_win, vb)
                + jnp.einsum('bhql,blhd->bqhd', p[..., n_win:], v_ctx))

    o = lax.map(row_step, (jnp.arange(rows), jnp.moveaxis(qg, 1, 0)))
    return jnp.moveaxis(o, 0, 1).reshape(bn, t, h * d)


def _rope_2d(t):
    pos = jnp.arange(t)
    row = (pos // GRID_W).astype(F32)
    col = (pos % GRID_W).astype(F32)
    half = C_DIM // 2
    inv = ROPE_THETA ** (-jnp.arange(0, half, 2, dtype=F32) / half)
    ang = jnp.concatenate([row[:, None] * inv, col[:, None] * inv], axis=-1)
    return jnp.cos(ang), jnp.sin(ang)


def _apply_rope(x, cos, sin):
    xf = x.astype(F32).reshape(x.shape[:-1] + (-1, 2))
    x0, x1 = xf[..., 0], xf[..., 1]
    cs, sn = cos[None, :, None, :], sin[None, :, None, :]
    return jnp.stack([x0 * cs - x1 * sn, x0 * sn + x1 * cs], axis=-1).reshape(x.shape).astype(x.dtype)


def _gqa_attend(qb, k, v):
    s = jnp.einsum('bqhgd,bkhd->bhgqk', qb, k).astype(F32) * (C_DIM ** -0.5)
    p = jax.nn.softmax(s, axis=-1).astype(v.dtype)
    return jnp.einsum('bhgqk,bkhd->bqhgd', p, v)


def _ssm_combine(e1, e2):
    a1, b1 = e1
    a2, b2 = e2
    return a1 * a2, a2 * b1 + b2


def _ssm_scan(u, a_bar, b_bar, x0, reverse):
    bu = jnp.einsum('btgp,gnp->btgn', u.astype(jnp.complex64), b_bar)
    edge = -1 if reverse else 0
    bu = bu.at[:, edge].add(a_bar * x0)
    a = jnp.broadcast_to(a_bar, bu.shape)
    _, xs = lax.associative_scan(_ssm_combine, (a, bu), axis=1, reverse=reverse)
    return xs


def _s5(u, uc, a_re, a_im, log_dt, b_re, b_im, c_re, c_im, d_skip, w_glu, b_glu, need_ctx):
    bn, t, _ = u.shape
    ug = u.astype(F32).reshape(bn, t, S5_GROUPS, S5_GROUP)
    ucg = uc.astype(F32).reshape(bn, uc.shape[1], S5_GROUPS, S5_GROUP)
    dsk = d_skip.astype(F32).reshape(S5_GROUPS, S5_GROUP)
    y_l = dsk * ug
    y_c = dsk * ucg
    x0 = jnp.zeros((bn, S5_GROUPS, S5_STATE), jnp.complex64)
    for d in range(2):
        rev = d == 1
        a_c = lax.complex(a_re[d].astype(F32), a_im[d].astype(F32))
        dt = jnp.exp(log_dt[d].astype(F32))[:, None]
        a_bar = jnp.exp(a_c * dt)
        b_bar = ((a_bar - 1.0) / a_c)[..., None] * lax.complex(b_re[d].astype(F32), b_im[d].astype(F32))
        c_mat = lax.complex(c_re[d].astype(F32), c_im[d].astype(F32))
        xs_c = _ssm_scan(ucg, a_bar, b_bar, x0, rev)
        x_c_final = xs_c[:, 0] if rev else xs_c[:, -1]
        xs_l = _ssm_scan(ug, a_bar, b_bar, x_c_final, rev)
        y_l = y_l + jnp.real(jnp.einsum('btgn,gpn->btgp', xs_l, c_mat))
        if need_ctx:
            y_c = y_c + jnp.real(jnp.einsum('btgn,gpn->btgp', xs_c, c_mat))

    def glu(y):
        y = jax.nn.gelu(y.reshape(y.shape[0], y.shape[1], S5_WIDTH))
        return y * jax.nn.sigmoid(y @ w_glu.astype(F32) + b_glu.astype(F32))

    return glu(y_l).astype(u.dtype), (glu(y_c).astype(u.dtype) if need_ctx else None)


def _even_mixer(hx, hc, w_in, w_out, lb, g_norm, rpb, need_ctx):
    p = hx @ w_in
    pc = hc @ w_in
    na = 5 * A_WIDTH
    a_l, a_c = _hgrn2(p[..., :na], pc[..., :na], lb, g_norm)
    q, k, v = [_heads(z, B_HEADS) for z in jnp.split(p[..., na:], 3, axis=-1)]
    qc, kc, vc = [_heads(z, B_HEADS) for z in jnp.split(pc[..., na:], 3, axis=-1)]
    b_l = _neighbourhood_attn(q, k, v, kc, vc, rpb)
    y = jnp.concatenate([a_l.astype(hx.dtype), b_l], axis=-1) @ w_out
    if not need_ctx:
        return y, None
    b_c = _softmax_attn(qc, kc, vc).reshape(hc.shape[0], hc.shape[1], B_WIDTH)
    yc = jnp.concatenate([a_c.astype(hc.dtype), b_c], axis=-1) @ w_out
    return y, yc


def _odd_mixer(hx, hc, w_in, w_out, q_norm, k_norm, cos, sin,
               a_re, a_im, log_dt, b_re, b_im, c_re, c_im, d_skip, w_glu, b_glu, need_ctx):
    bn, t, _ = hx.shape
    o1, o2, o3 = C_Q_WIDTH, C_Q_WIDTH + C_KV_WIDTH, C_Q_WIDTH + 2 * C_KV_WIDTH
    p = hx @ w_in
    pc = hc @ w_in[:, C_Q_WIDTH:]
    q = _apply_rope(_rmsnorm(_heads(p[..., :o1], C_Q_HEADS), q_norm), cos, sin)
    k = _apply_rope(_rmsnorm(_heads(p[..., o1:o2], C_KV_HEADS), k_norm), cos, sin)
    v = _heads(p[..., o2:o3], C_KV_HEADS)
    u = p[..., o3:]
    kc = _rmsnorm(_heads(pc[..., :C_KV_WIDTH], C_KV_HEADS), k_norm)
    vc = _heads(pc[..., C_KV_WIDTH:2 * C_KV_WIDTH], C_KV_HEADS)
    uc = pc[..., 2 * C_KV_WIDTH:]
    k_all = jnp.concatenate([k, kc], axis=1)
    v_all = jnp.concatenate([v, vc], axis=1)
    nb = t // C_BLOCK
    qb = jnp.moveaxis(q.reshape(bn, nb, C_BLOCK, C_KV_HEADS, C_GROUP, C_DIM), 1, 0)
    att = lax.map(lambda blk: _gqa_attend(blk, k_all, v_all), qb)
    att = jnp.moveaxis(att, 0, 1).reshape(bn, t, C_Q_WIDTH)
    s5_l, s5_c = _s5(u, uc, a_re, a_im, log_dt, b_re, b_im, c_re, c_im, d_skip, w_glu, b_glu, need_ctx)
    y = jnp.concatenate([att, s5_l], axis=-1) @ w_out
    if not need_ctx:
        return y, None
    qc = _rmsnorm(_heads(hc @ w_in[:, :C_Q_WIDTH], C_Q_HEADS), q_norm)
    qc = qc.reshape(bn, hc.shape[1], C_KV_HEADS, C_GROUP, C_DIM)
    att_c = _gqa_attend(qc, kc, vc).reshape(bn, hc.shape[1], C_Q_WIDTH)
    yc = jnp.concatenate([att_c, s5_c], axis=-1) @ w_out
    return y, yc


def _ec_moe(h, w_router, w_gate, w_up, w_down):
    bn, n, d = h.shape
    cap = max(1, EC_CAPACITY * n // N_EXPERTS)
    probs = jax.nn.softmax(jnp.einsum('bnd,de->bne', h, w_router).astype(F32), axis=-1)
    gate, idx = lax.top_k(jnp.swapaxes(probs, 1, 2), cap)
    xin = jax.vmap(lambda hb, ib: hb[ib])(h, idx)
    hid = jax.nn.silu(jnp.einsum('becd,edf->becf', xin, w_gate)) * jnp.einsum('becd,edf->becf', xin, w_up)
    y = jnp.einsum('becf,efd->becd', hid, w_down) * gate[..., None].astype(h.dtype)
    return jax.vmap(lambda ib, yb: jnp.zeros((n, d), yb.dtype).at[ib.reshape(-1)].add(yb.reshape(-1, d)))(idx, y)


def setup_inputs(seed: int = 0) -> dict:
    key = jax.random.key(seed)
    ks = iter(jax.random.split(key, 48))
    nrm = lambda shape, scale: scale * jax.random.normal(next(ks), shape, F32)
    gain = lambda shape: 1.0 + nrm(shape, 0.02)
    D = D_MODEL
    n_idx = jnp.arange(S5_STATE, dtype=F32)
    ssm_shape = (N_ODD, 2, S5_GROUPS, S5_STATE)
    return {
        "x": nrm((BATCH, SEQ, D), 1.0),
        "c": nrm((BATCH, D), 1.0),
        "ctx": nrm((BATCH, CTX_LEN, D), 1.0),
        "c_ctx": nrm((D,), 1.0),
        "w_mod": nrm((DEPTH, D, 6 * D), 0.5 * D ** -0.5),
        "b_mod": nrm((DEPTH, 6 * D), 0.01),
        "g_mix_pre": gain((DEPTH, D)),
        "g_mix_post": gain((DEPTH, D)),
        "g_ffn_pre": gain((DEPTH, D)),
        "g_ffn_post": gain((DEPTH, D)),
        "w_router": nrm((DEPTH, D, N_EXPERTS), D ** -0.5),
        "w_exp_gate": nrm((DEPTH, N_EXPERTS, D, EXPERT_FF), D ** -0.5),
        "w_exp_up": nrm((DEPTH, N_EXPERTS, D, EXPERT_FF), D ** -0.5),
        "w_exp_down": nrm((DEPTH, N_EXPERTS, EXPERT_FF, D), EXPERT_FF ** -0.5),
        "ev_w_in": nrm((N_EVEN, D, EVEN_IN), D ** -0.5),
        "ev_w_out": nrm((N_EVEN, MIX_WIDTH, D), MIX_WIDTH ** -0.5),
        "hgrn_lb": nrm((N_EVEN + 1, 2, A_WIDTH), 0.1).at[0].add(-2.0),
        "hgrn_g_norm": gain((N_EVEN, A_DIM)),
        "na_rpb": nrm((N_EVEN, B_HEADS, 2 * NA_ROWS - 1, 2 * NA_COLS - 1), 0.1),
        "od_w_in": nrm((N_ODD, D, ODD_IN), D ** -0.5),
        "od_w_out": nrm((N_ODD, MIX_WIDTH, D), MIX_WIDTH ** -0.5),
        "q_norm": gain((N_ODD, C_DIM)),
        "k_norm": gain((N_ODD, C_DIM)),
        "s5_a_re": -0.5 + nrm(ssm_shape, 0.01),
        "s5_a_im": jnp.broadcast_to(math.pi * n_idx, ssm_shape) + nrm(ssm_shape, 0.01),
        "s5_log_dt": jax.random.uniform(next(ks), (N_ODD, 2, S5_GROUPS), F32, math.log(1e-3), math.log(1e-1)),
        "s5_b_re": nrm((N_ODD, 2, S5_GROUPS, S5_STATE, S5_GROUP), (2 * S5_GROUP) ** -0.5),
        "s5_b_im": nrm((N_ODD, 2, S5_GROUPS, S5_STATE, S5_GROUP), (2 * S5_GROUP) ** -0.5),
        "s5_c_re": nrm((N_ODD, 2, S5_GROUPS, S5_GROUP, S5_STATE), (2 * S5_STATE) ** -0.5),
        "s5_c_im": nrm((N_ODD, 2, S5_GROUPS, S5_GROUP, S5_STATE), (2 * S5_STATE) ** -0.5),
        "s5_d": nrm((N_ODD, S5_WIDTH), 1.0),
        "s5_w_glu": nrm((N_ODD, S5_WIDTH, S5_WIDTH), S5_WIDTH ** -0.5),
        "s5_b_glu": nrm((N_ODD, S5_WIDTH), 0.01),
    }


def reference(x, c, ctx, c_ctx, w_mod, b_mod, g_mix_pre, g_mix_post, g_ffn_pre, g_ffn_post,
              w_router, w_exp_gate, w_exp_up, w_exp_down,
              ev_w_in, ev_w_out, hgrn_lb, hgrn_g_norm, na_rpb,
              od_w_in, od_w_out, q_norm, k_norm, s5_a_re, s5_a_im, s5_log_dt,
              s5_b_re, s5_b_im, s5_c_re, s5_c_im, s5_d, s5_w_glu, s5_b_glu):
    t = x.shape[1]
    cos, sin = _rope_2d(t)
    lb_all = jnp.cumsum(jax.nn.softmax(hgrn_lb.astype(F32), axis=0), axis=0)
    for layer in range(DEPTH):
        need_ctx = layer < DEPTH - 1
        ml = jnp.split((jax.nn.silu(c) @ w_mod[layer] + b_mod[layer])[:, None, :], 6, axis=-1)
        mc = jnp.split((jax.nn.silu(c_ctx) @ w_mod[layer] + b_mod[layer])[None, None, :], 6, axis=-1)
        hx = _rmsnorm(x, g_mix_pre[layer]) * (1.0 + ml[1]) + ml[0]
        hc = _rmsnorm(ctx, g_mix_pre[layer]) * (1.0 + mc[1]) + mc[0]
        if layer % 2 == 0:
            e = layer // 2
            y, yc = _even_mixer(hx, hc, ev_w_in[e], ev_w_out[e], lb_all[e], hgrn_g_norm[e], na_rpb[e], need_ctx)
        else:
            o = layer // 2
            y, yc = _odd_mixer(hx, hc, od_w_in[o], od_w_out[o], q_norm[o], k_norm[o], cos, sin,
                               s5_a_re[o], s5_a_im[o], s5_log_dt[o], s5_b_re[o], s5_b_im[o],
                               s5_c_re[o], s5_c_im[o], s5_d[o], s5_w_glu[o], s5_b_glu[o], need_ctx)
        x = x + ml[2] * _rmsnorm(y, g_mix_post[layer])
        hx = _rmsnorm(x, g_ffn_pre[layer]) * (1.0 + ml[4]) + ml[3]
        x = x + ml[5] * _rmsnorm(_ec_moe(hx, w_router[layer], w_exp_gate[layer], w_exp_up[layer], w_exp_down[layer]), g_ffn_post[layer])
        if need_ctx:
            ctx = ctx + mc[2] * _rmsnorm(yc, g_mix_post[layer])
            hc = _rmsnorm(ctx, g_ffn_pre[layer]) * (1.0 + mc[4]) + mc[3]
            ctx = ctx + mc[5] * _rmsnorm(_ec_moe(hc, w_router[layer], w_exp_gate[layer], w_exp_up[layer], w_exp_down[layer]), g_ffn_post[layer])
    return x
```

```python
import functools
import math

import jax
import jax.numpy as jnp
from jax import lax
from jax.experimental import pallas as pl
from jax.experimental.pallas import tpu as pltpu

F32 = jnp.float32
BF16 = jnp.bfloat16
HIGHEST = lax.Precision.HIGHEST
EPS = 1e-6

LANE = 128
TOK_TILE = 256
VMEM_LIMIT = 52 << 20

GRID_W = 64
NA_ROWS = 8
NA_COLS = 16
HEAD = 128
A_CHUNK = 64
N_EXPERTS = 16
EC_CAPACITY = 2
S5_GROUP = 16
S5_STATE = 64
S5_CHUNK = 16
ROPE_THETA = 10000.0
NEG_BIG = -1e30


def _cparams(*sem):
    return pltpu.CompilerParams(dimension_semantics=sem, vmem_limit_bytes=VMEM_LIMIT)


def _dot(a, b):
    return jnp.dot(a, b, preferred_element_type=F32)


def _dot_nt(a, b):
    return lax.dot_general(a, b, (((1,), (1,)), ((), ())), preferred_element_type=F32)


def _dot_tn(a, b):
    return lax.dot_general(a, b, (((0,), (0,)), ((), ())), preferred_element_type=F32)


def _rms(x):
    return x * lax.rsqrt(jnp.mean(x * x, axis=-1, keepdims=True) + EPS)


def _silu(x):
    return x * jax.nn.sigmoid(x)


def _mod_kernel(c_ref, w_ref, b_ref, o_ref):
    o_ref[...] = jnp.dot(_silu(c_ref[...]), w_ref[...], preferred_element_type=F32, precision=HIGHEST) + b_ref[...]


def _modulation(cc, w_mod, b_mod):
    depth, d, n = w_mod.shape
    tn = 1024
    return pl.pallas_call(
        _mod_kernel,
        out_shape=jax.ShapeDtypeStruct((depth, cc.shape[0], n), F32),
        grid=(depth, n // tn),
        in_specs=[pl.BlockSpec(cc.shape, lambda l, j: (0, 0)),
                  pl.BlockSpec((None, d, tn), lambda l, j: (l, 0, j)),
                  pl.BlockSpec((None, 1, tn), lambda l, j: (l, 0, j))],
        out_specs=pl.BlockSpec((None, cc.shape[0], tn), lambda l, j: (l, 0, j)),
        compiler_params=_cparams("arbitrary", "arbitrary"),
        name="modulation",
    )(cc, w_mod, b_mod.reshape(depth, 1, n))


def _mod_spec(d):
    return pl.BlockSpec((None, None, 6, d), lambda b, j: (b, jnp.minimum(j, 1), 0, 0))


def _prenorm_kernel(x_ref, g_ref, m_ref, o_ref):
    m = m_ref[...]
    y = _rms(x_ref[...]) * g_ref[...]
    o_ref[...] = (y * (1.0 + m[1:2]) + m[0:1]).astype(BF16)


def _prenorm(xa, g, modtab):
    b, ta, d = xa.shape
    return pl.pallas_call(
        _prenorm_kernel,
        out_shape=jax.ShapeDtypeStruct((b, ta, d), BF16),
        grid=(b, ta // TOK_TILE),
        in_specs=[pl.BlockSpec((None, TOK_TILE, d), lambda i, j: (i, j, 0)),
                  pl.BlockSpec((1, d), lambda i, j: (0, 0)),
                  _mod_spec(d)],
        out_specs=pl.BlockSpec((None, TOK_TILE, d), lambda i, j: (i, j, 0)),
        compiler_params=_cparams("parallel", "parallel"),
        name="prenorm",
    )(xa, g.reshape(1, d), modtab)


def _mm_kernel(a_ref, w_ref, o_ref, wb_ref):
    @pl.when(pl.program_id(1) == 0)
    def _():
        wb_ref[...] = w_ref[...].astype(BF16)

    o_ref[...] = _dot(a_ref[...], wb_ref[...]).astype(o_ref.dtype)


def _matmul(a, w, tm=512, tn=1024, out_dtype=F32):
    m, k = a.shape
    n = w.shape[1]
    return pl.pallas_call(
        _mm_kernel,
        out_shape=jax.ShapeDtypeStruct((m, n), out_dtype),
        grid=(n // tn, m // tm),
        in_specs=[pl.BlockSpec((tm, k), lambda j, i: (i, 0)),
                  pl.BlockSpec((k, tn), lambda j, i: (0, j))],
        out_specs=pl.BlockSpec((tm, tn), lambda j, i: (i, j)),
        scratch_shapes=[pltpu.VMEM((k, tn), BF16)],
        compiler_params=_cparams("arbitrary", "arbitrary"),
        name="proj_in",
    )(a, w)


def _hgrn_chunk(q, fr, v, lb, st, rev, chunk):
    f = lb + (1.0 - lb) * jax.nn.sigmoid(fr)
    kk = 1.0 - f
    lf = jnp.log(f)
    row = lax.broadcasted_iota(jnp.int32, (chunk, LANE), 0)
    ti = lax.broadcasted_iota(jnp.int32, (chunk, chunk), 0)
    si = lax.broadcasted_iota(jnp.int32, (chunk, chunk), 1)
    att = jnp.where(ti == si, _dot_nt(q.astype(BF16), kk.astype(BF16)), 0.0)
    p_in = lf
    r_ex = jnp.zeros_like(lf)
    tot = lf
    for lvl in range(chunk.bit_length() - 1):
        step = 1 << lvl
        bit = ((row >> lvl) & 1) == 1
        up = pltpu.roll(tot, step, 0)
        dn = pltpu.roll(tot, chunk - step, 0)
        is_q = jnp.logical_not(bit) if rev else bit
        qf = jnp.where(is_q, jnp.exp(p_in) * q, 0.0).astype(BF16)
        kf = jnp.where(is_q, 0.0, jnp.exp(r_ex) * kk).astype(BF16)
        same = (ti >> (lvl + 1)) == (si >> (lvl + 1))
        att = att + jnp.where(same, _dot_nt(qf, kf), 0.0)
        if rev:
            p_in = p_in + jnp.where(bit, 0.0, dn)
            r_ex = r_ex + jnp.where(bit, up, 0.0)
        else:
            p_in = p_in + jnp.where(bit, up, 0.0)
            r_ex = r_ex + jnp.where(bit, 0.0, dn)
        tot = tot + jnp.where(bit, up, dn)
    vb = v.astype(BF16)
    o = _dot_nt((q * jnp.exp(p_in)).astype(BF16), st.astype(BF16)) + _dot(att.astype(BF16), vb)
    kd = (kk * jnp.exp(r_ex)).astype(BF16)
    st_new = st * jnp.exp(tot[0:1]) + _dot_tn(vb, kd)
    return o, st_new


def _hgrn_kernel(q_ref, f_ref, v_ref, lb_ref, o_ref, st_ref, *, rev, chunk, hb):
    @pl.when(pl.program_id(2) == 0)
    def _():
        st_ref[...] = jnp.zeros_like(st_ref)

    for h in range(hb):
        sl = slice(h * HEAD, (h + 1) * HEAD)
        o, st = _hgrn_chunk(q_ref[:, sl], f_ref[:, sl], v_ref[:, sl], lb_ref[:, sl], st_ref[h], rev, chunk)
        o_ref[:, sl] = o
        st_ref[h] = st


def _hgrn_dir(p, lb_row, ctx_len, width, rev, hb=2):
    b, ta, _ = p.shape
    chunk = A_CHUNK
    n = ta // chunk
    nc = ctx_len // chunk
    cw = HEAD * hb
    nh = width // cw

    def phys(j):
        return jnp.where(j < nc, nc - 1 - j, n - 1 + nc - j) if rev else j

    def col(base):
        return lambda i, h, j: (i, phys(j), base // cw + h)

    fcol = width * (2 if rev else 1)
    return pl.pallas_call(
        functools.partial(_hgrn_kernel, rev=rev, chunk=chunk, hb=hb),
        out_shape=jax.ShapeDtypeStruct((b, ta, width), F32),
        grid=(b, nh, n),
        in_specs=[pl.BlockSpec((None, chunk, cw), col(0)),
                  pl.BlockSpec((None, chunk, cw), col(fcol)),
                  pl.BlockSpec((None, chunk, cw), col(3 * width)),
                  pl.BlockSpec((1, cw), lambda i, h, j: (0, h))],
        out_specs=pl.BlockSpec((None, chunk, cw), lambda i, h, j: (i, phys(j), h)),
        scratch_shapes=[pltpu.VMEM((hb, HEAD, HEAD), F32)],
        compiler_params=_cparams("parallel", "parallel", "arbitrary"),
        name="hgrn_rev" if rev else "hgrn_fwd",
    )(p, p, p, lb_row)


def _hgrn_readout_kernel(of_ref, ob_ref, g_ref, gn_ref, o_ref):
    for h in range(o_ref.shape[-1] // HEAD):
        sl = slice(h * HEAD, (h + 1) * HEAD)
        o = of_ref[:, sl] + ob_ref[:, sl]
        o_ref[:, sl] = (_rms(o) * gn_ref[...] * _silu(g_ref[:, sl])).astype(BF16)


def _hgrn_readout(o_f, o_b, p, g_norm, width):
    b, ta, _ = o_f.shape
    spec = pl.BlockSpec((None, TOK_TILE, width), lambda i, j: (i, j, 0))
    return pl.pallas_call(
        _hgrn_readout_kernel,
        out_shape=jax.ShapeDtypeStruct((b, ta, width), BF16),
        grid=(b, ta // TOK_TILE),
        in_specs=[spec, spec,
                  pl.BlockSpec((None, TOK_TILE, width), lambda i, j: (i, j, 4)),
                  pl.BlockSpec((1, HEAD), lambda i, j: (0, 0))],
        out_specs=spec,
        compiler_params=_cparams("parallel", "parallel"),
        name="hgrn_readout",
    )(o_f, o_b, p, g_norm.reshape(1, HEAD))


def _natten_bias(rpb):
    col = jnp.arange(GRID_W)
    col_start = jnp.clip(col - NA_COLS // 2, 0, GRID_W - NA_COLS)
    mask = (col[None, :] >= col_start[:, None]) & (col[None, :] < col_start[:, None] + NA_COLS)
    dc = jnp.clip(col[None, :] - col[:, None] + NA_COLS - 1, 0, 2 * NA_COLS - 2)
    dr = jnp.arange(NA_ROWS)[:, None] + jnp.arange(NA_ROWS)[None, :]
    t = rpb.astype(F32)[:, dr[:, :, None, None], dc[None, None, :, :]]
    t = jnp.where(mask[None, None, None], t, NEG_BIG)
    t = jnp.transpose(t, (0, 1, 3, 2, 4))
    return t.reshape(rpb.shape[0], NA_ROWS, GRID_W, NA_ROWS * GRID_W)


def _natten_kernel(q_ref, k_ref, v_ref, bias_ref, o_ref, *, ctx_len, rows):
    r = pl.program_id(2)
    scale = HEAD ** -0.5
    qb = q_ref[...].astype(BF16)
    kc = k_ref[0:ctx_len, :].astype(BF16)
    vc = v_ref[0:ctx_len, :].astype(BF16)
    s_ctx = _dot_nt(qb, kc) * scale
    m_ctx = jnp.max(s_ctx, axis=1, keepdims=True)

    @pl.when(r < rows)
    def _():
        rs = jnp.clip(r - NA_ROWS // 2, 0, rows - NA_ROWS)
        d0 = rs - r + NA_ROWS - 1
        start = pl.multiple_of(ctx_len + rs * GRID_W, GRID_W)
        kw = k_ref[pl.ds(start, NA_ROWS * GRID_W), :].astype(BF16)
        vw = v_ref[pl.ds(start, NA_ROWS * GRID_W), :].astype(BF16)
        s_win = _dot_nt(qb, kw) * scale + bias_ref[d0]
        m = jnp.maximum(m_ctx, jnp.max(s_win, axis=1, keepdims=True))
        pw = jnp.exp(s_win - m)
        pc = jnp.exp(s_ctx - m)
        den = jnp.sum(pw, axis=1, keepdims=True) + jnp.sum(pc, axis=1, keepdims=True)
        o = _dot(pw.astype(BF16), vw) + _dot(pc.astype(BF16), vc)
        o_ref[...] = (o / den).astype(BF16)

    @pl.when(r >= rows)
    def _():
        pc = jnp.exp(s_ctx - m_ctx)
        o = _dot(pc.astype(BF16), vc) / jnp.sum(pc, axis=1, keepdims=True)
        o_ref[...] = o.astype(BF16)


def _natten(p, bias, ctx_len, col0, width):
    b, ta, _ = p.shape
    nh = width // HEAD
    rows = (ta - ctx_len) // GRID_W
    ncr = ctx_len // GRID_W

    def qmap(cb):
        return lambda i, h, r: (i, jnp.where(r < rows, ncr + r, r - rows), cb + h)

    return pl.pallas_call(
        functools.partial(_natten_kernel, ctx_len=ctx_len, rows=rows),
        out_shape=jax.ShapeDtypeStruct((b, ta, width), BF16),
        grid=(b, nh, rows + ncr),
        in_specs=[pl.BlockSpec((None, GRID_W, HEAD), qmap(col0 // HEAD)),
                  pl.BlockSpec((None, ta, HEAD), lambda i, h, r: (i, 0, (col0 + width) // HEAD + h)),
                  pl.BlockSpec((None, ta, HEAD), lambda i, h, r: (i, 0, (col0 + 2 * width) // HEAD + h)),
                  pl.BlockSpec((None, NA_ROWS, GRID_W, NA_ROWS * GRID_W), lambda i, h, r: (h, 0, 0, 0))],
        out_specs=pl.BlockSpec((None, GRID_W, HEAD), qmap(0)),
        compiler_params=_cparams("parallel", "parallel", "arbitrary"),
        name="natten",
    )(p, p, p, bias)


def _rope_tables(ctx_len, t):
    pos = jnp.arange(t)
    row = (pos // GRID_W).astype(F32)
    col = (pos % GRID_W).astype(F32)
    half = HEAD // 2
    inv = ROPE_THETA ** (-jnp.arange(0, half, 2, dtype=F32) / half)
    ang = jnp.concatenate([row[:, None] * inv, col[:, None] * inv], axis=-1)
    cos, sin = jnp.cos(ang), jnp.sin(ang)
    cosf = jnp.repeat(cos, 2, axis=-1)
    sinf = jnp.stack([-sin, sin], axis=-1).reshape(t, HEAD)
    cosf = jnp.concatenate([jnp.ones((ctx_len, HEAD), F32), cosf], axis=0)
    sinf = jnp.concatenate([jnp.zeros((ctx_len, HEAD), F32), sinf], axis=0)
    return cosf, sinf


def _qkprep_kernel(q_ref, k_ref, v_ref, cos_ref, sin_ref, qn_ref, kn_ref, qo_ref, ko_ref, vo_ref):
    cosf = cos_ref[...]
    sinf = sin_ref[...]
    even = (lax.broadcasted_iota(jnp.int32, cosf.shape, 1) & 1) == 0

    def rope(x):
        partner = jnp.where(even, pltpu.roll(x, LANE - 1, 1), pltpu.roll(x, 1, 1))
        return x * cosf + partner * sinf

    for h in range(q_ref.shape[-1] // HEAD):
        sl = slice(h * HEAD, (h + 1) * HEAD)
        qo_ref[:, sl] = (rope(_rms(q_ref[:, sl]) * qn_ref[...]) * (HEAD ** -0.5)).astype(BF16)
    for h in range(k_ref.shape[-1] // HEAD):
        sl = slice(h * HEAD, (h + 1) * HEAD)
        ko_ref[:, sl] = rope(_rms(k_ref[:, sl]) * kn_ref[...]).astype(BF16)
    vo_ref[...] = v_ref[...].astype(BF16)


def _qkprep(p, cosf, sinf, q_norm, k_norm, qw, kw):
    b, ta, _ = p.shape
    tt = TOK_TILE

    def spec(w, cb):
        return pl.BlockSpec((None, tt, w), lambda i, j: (i, j, cb))

    return pl.pallas_call(
        _qkprep_kernel,
        out_shape=(jax.ShapeDtypeStruct((b, ta, qw), BF16),
                   jax.ShapeDtypeStruct((b, ta, kw), BF16),
                   jax.ShapeDtypeStruct((b, ta, kw), BF16)),
        grid=(b, ta // tt),
        in_specs=[spec(qw, 0), spec(kw, qw // kw), spec(kw, qw // kw + 1),
                  pl.BlockSpec((tt, HEAD), lambda i, j: (j, 0)),
                  pl.BlockSpec((tt, HEAD), lambda i, j: (j, 0)),
                  pl.BlockSpec((1, HEAD), lambda i, j: (0, 0)),
                  pl.BlockSpec((1, HEAD), lambda i, j: (0, 0))],
        out_specs=(spec(qw, 0), spec(kw, 0), spec(kw, 0)),
        compiler_params=_cparams("parallel", "parallel"),
        name="qk_prep",
    )(p, p, p, cosf, sinf, q_norm.reshape(1, HEAD), k_norm.reshape(1, HEAD))


def _attn_kernel(q_ref, k_ref, v_ref, o_ref):
    s = _dot_nt(q_ref[...], k_ref[...])
    p = jnp.exp(s - jnp.max(s, axis=1, keepdims=True))
    o = _dot(p.astype(BF16), v_ref[...]) / jnp.sum(p, axis=1, keepdims=True)
    o_ref[...] = o.astype(BF16)


def _gqa_attention(qn, kn, vb, group):
    b, ta, qw = qn.shape
    nkv = kn.shape[-1] // HEAD
    tq = TOK_TILE
    return pl.pallas_call(
        _attn_kernel,
        out_shape=jax.ShapeDtypeStruct((b, ta, qw), BF16),
        grid=(b, nkv, group, ta // tq),
        in_specs=[pl.BlockSpec((None, tq, HEAD), lambda i, h, g, j: (i, j, h * group + g)),
                  pl.BlockSpec((None, ta, HEAD), lambda i, h, g, j: (i, 0, h)),
                  pl.BlockSpec((None, ta, HEAD), lambda i, h, g, j: (i, 0, h))],
        out_specs=pl.BlockSpec((None, tq, HEAD), lambda i, h, g, j: (i, j, h * group + g)),
        compiler_params=_cparams("parallel", "parallel", "parallel", "parallel"),
        name="gqa_attention",
    )(qn, kn, vb)


def _s5_operators(a_re, a_im, log_dt, b_re, b_im, c_re, c_im):
    lc = S5_CHUNK
    a_c = lax.complex(a_re.astype(F32), a_im.astype(F32))
    dt = jnp.exp(log_dt.astype(F32))[..., None]
    adt = a_c * dt
    a_bar = jnp.exp(adt)
    b_bar = ((a_bar - 1.0) / a_c)[..., None] * lax.complex(b_re.astype(F32), b_im.astype(F32))
    c_mat = lax.complex(c_re.astype(F32), c_im.astype(F32))
    tau = jnp.arange(lc + 1, dtype=F32)
    apw = jnp.exp(adt[:, :, None, :] * tau[None, None, :, None])
    kern = jnp.real(jnp.einsum('dgpn,dgtn,dgnq->dgtpq', c_mat, apw[:, :, :lc], b_bar))
    i_out = jnp.arange(lc)[None, :]
    i_in = jnp.arange(lc)[:, None]
    lag = i_out - i_in
    tm = jnp.where((lag >= 0)[None, None, :, :, None, None], kern[:, :, jnp.clip(lag, 0, lc - 1)], 0.0)
    d, g = a_re.shape[0], a_re.shape[1]
    pp = S5_GROUP
    tmat = jnp.transpose(tm, (0, 1, 2, 5, 3, 4)).reshape(d, g, lc * pp, lc * pp)
    gm = apw[:, :, lc - 1 - jnp.arange(lc), :, None] * b_bar[:, :, None]
    gm = jnp.transpose(gm, (0, 1, 2, 4, 3)).reshape(d, g, lc * pp, S5_STATE)
    gmat = jnp.concatenate([jnp.real(gm), jnp.imag(gm)], axis=-1)
    hm = c_mat[:, :, None] * apw[:, :, 1:lc + 1, None, :]
    hm = jnp.transpose(hm, (0, 1, 4, 2, 3)).reshape(d, g, S5_STATE, lc * pp)
    hmat = jnp.concatenate([jnp.real(hm), -jnp.imag(hm)], axis=2)
    al = apw[:, :, lc]
    apow = jnp.concatenate([jnp.real(al), jnp.imag(al)], axis=-1)[:, :, None, :]
    return tmat.astype(BF16), gmat.astype(BF16), hmat.astype(BF16), apow


def _s5_kernel(u_ref, t_ref, g_ref, h_ref, a_ref, y_ref, xe_ref, xr_ref, xi_ref, *, nchunk, rpc):
    n = S5_STATE
    u = u_ref[...]
    xe_ref[...] = _dot(u, g_ref[...])
    ar = a_ref[:, 0:n]
    ai = a_ref[:, n:2 * n]

    def body(c, carry):
        xr, xi = carry
        off = pl.multiple_of(c * rpc, rpc)
        xr_ref[pl.ds(off, rpc), :] = xr
        xi_ref[pl.ds(off, rpc), :] = xi
        e = xe_ref[pl.ds(off, rpc), :]
        return ar * xr - ai * xi + e[:, 0:n], ar * xi + ai * xr + e[:, n:2 * n]

    zero = jnp.zeros((rpc, n), F32)
    lax.fori_loop(0, nchunk, body, (zero, zero))
    y = _dot(u, t_ref[...])
    y = y + _dot(xr_ref[...].astype(BF16), h_ref[0:n, :]) + _dot(xi_ref[...].astype(BF16), h_ref[n:2 * n, :])
    y_ref[...] = y


def _s5_scan(useq, ops):
    tmat, gmat, hmat, apow = ops
    _, b, ta, w = useq.shape
    lc, pp = S5_CHUNK, S5_GROUP
    g = w // pp
    nchunk = ta // lc
    rpc = 8
    cw = lc * pp
    u = useq.reshape(2, b, nchunk, lc, g, pp)
    u = jnp.transpose(u, (0, 4, 2, 1, 3, 5))
    u = jnp.pad(u, ((0, 0), (0, 0), (0, 0), (0, rpc - b), (0, 0), (0, 0)))
    u = u.reshape(2, g, nchunk * rpc, cw).astype(BF16)
    rows = nchunk * rpc
    y = pl.pallas_call(
        functools.partial(_s5_kernel, nchunk=nchunk, rpc=rpc),
        out_shape=jax.ShapeDtypeStruct((2, g, rows, cw), F32),
        grid=(2, g),
        in_specs=[pl.BlockSpec((None, None, rows, cw), lambda d, i: (d, i, 0, 0)),
                  pl.BlockSpec((None, None, cw, cw), lambda d, i: (d, i, 0, 0)),
                  pl.BlockSpec((None, None, cw, 2 * S5_STATE), lambda d, i: (d, i, 0, 0)),
                  pl.BlockSpec((None, None, 2 * S5_STATE, cw), lambda d, i: (d, i, 0, 0)),
                  pl.BlockSpec((None, None, 1, 2 * S5_STATE), lambda d, i: (d, i, 0, 0))],
        out_specs=pl.BlockSpec((None, None, rows, cw), lambda d, i: (d, i, 0, 0)),
        scratch_shapes=[pltpu.VMEM((rows, 2 * S5_STATE), F32),
                        pltpu.VMEM((rows, S5_STATE), F32),
                        pltpu.VMEM((rows, S5_STATE), F32)],
        compiler_params=_cparams("parallel", "parallel"),
        name="s5_scan",
    )(u, tmat, gmat, hmat, apow)
    y = y.reshape(2, g, nchunk, rpc, lc, pp)[:, :, :, :b]
    return jnp.transpose(y, (0, 3, 2, 4, 1, 5)).reshape(2, b, ta, w)


def _s5_glu_kernel(yf_ref, yb_ref, u_ref, d_ref, w_ref, b_ref, o_ref):
    y = d_ref[...] * u_ref[...] + yf_ref[...] + yb_ref[...]
    y = 0.5 * y * (1.0 + jnp.tanh(math.sqrt(2.0 / math.pi) * (y + 0.044715 * (y * y * y))))
    z = _dot(y.astype(BF16), w_ref[...]) + b_ref[...]
    o_ref[...] = (y * jax.nn.sigmoid(z)).astype(BF16)


def _s5_glu(y_f, y_b, p, ucol, d_skip, w_glu, b_glu):
    b, ta, w = y_f.shape
    spec = pl.BlockSpec((None, TOK_TILE, w), lambda i, j: (i, j, 0))
    vec = pl.BlockSpec((1, w), lambda i, j: (0, 0))
    return pl.pallas_call(
        _s5_glu_kernel,
        out_shape=jax.ShapeDtypeStruct((b, ta, w), BF16),
        grid=(b, ta // TOK_TILE),
        in_specs=[spec, spec,
                  pl.BlockSpec((None, TOK_TILE, w), lambda i, j: (i, j, ucol // w)),
                  vec, pl.BlockSpec((w, w), lambda i, j: (0, 0)), vec],
        out_specs=spec,
        compiler_params=_cparams("parallel", "parallel"),
        name="s5_glu",
    )(y_f, y_b, p, d_skip.reshape(1, w), w_glu.astype(BF16), b_glu.reshape(1, w))


def _s5(p, ucol, width, ctx_len, ops, d_skip, w_glu, b_glu):
    u = p[..., ucol:ucol + width]
    flip = lambda a: jnp.flip(a, axis=1)
    u_rev = jnp.concatenate([flip(u[:, :ctx_len]), flip(u[:, ctx_len:])], axis=1)
    y = _s5_scan(jnp.stack([u, u_rev]), ops)
    y_b = jnp.concatenate([flip(y[1][:, :ctx_len]), flip(y[1][:, ctx_len:])], axis=1)
    return _s5_glu(y[0], y_b, p, ucol, d_skip, w_glu, b_glu)


def _postmix_kernel(a_ref, b_ref, w_ref, x_ref, m_ref, gpost_ref, gpre_ref, wr_ref, xo_ref, h_ref, lg_ref):
    wa = a_ref.shape[-1]
    y = _dot(a_ref[...], w_ref[0:wa, :]) + _dot(b_ref[...], w_ref[wa:, :])
    m = m_ref[...]
    xn = x_ref[...] + m[2:3] * (_rms(y) * gpost_ref[...])
    xo_ref[...] = xn
    h2 = _rms(xn) * gpre_ref[...] * (1.0 + m[4:5]) + m[3:4]
    h_ref[...] = h2.astype(BF16)
    lg_ref[...] = lax.dot_general(wr_ref[...], h2, (((1,), (1,)), ((), ())),
                                  preferred_element_type=F32, precision=HIGHEST)


def _postmix(mix_a, mix_b, w_out, xa, modtab, g_post, g_pre, w_router):
    b, ta, d = xa.shape
    wa, wb = mix_a.shape[-1], mix_b.shape[-1]
    ne = w_router.shape[-1]
    tok = lambda w: pl.BlockSpec((None, TOK_TILE, w), lambda i, j: (i, j, 0))
    vec = pl.BlockSpec((1, d), lambda i, j: (0, 0))
    return pl.pallas_call(
        _postmix_kernel,
        out_shape=(jax.ShapeDtypeStruct((b, ta, d), F32),
                   jax.ShapeDtypeStruct((b, ta, d), BF16),
                   jax.ShapeDtypeStruct((b, ne, ta), F32)),
        grid=(b, ta // TOK_TILE),
        in_specs=[tok(wa), tok(wb),
                  pl.BlockSpec((wa + wb, d), lambda i, j: (0, 0)),
                  tok(d), _mod_spec(d), vec, vec,
                  pl.BlockSpec((ne, d), lambda i, j: (0, 0))],
        out_specs=(tok(d), tok(d), pl.BlockSpec((None, ne, TOK_TILE), lambda i, j: (i, 0, j))),
        compiler_params=_cparams("parallel", "parallel"),
        name="mix_out",
    )(mix_a, mix_b, w_out.astype(BF16), xa, modtab, g_post.reshape(1, d), g_pre.reshape(1, d), w_router.T)


def _select_kernel(lg_ref, pos_ref, gate_ref, *, ctx_len, cap_ctx, cap_lat):
    lg = lg_ref[...]
    ne, ta = lg.shape
    ex = jnp.exp(lg - jnp.max(lg, axis=0, keepdims=True))
    probs = ex / jnp.sum(ex, axis=0, keepdims=True)
    bits = pltpu.bitcast(probs, jnp.int32)
    is_ctx = lax.broadcasted_iota(jnp.int32, (ne, ta), 1) < ctx_len

    def counts(mask):
        mf = jnp.where(mask, 1.0, 0.0)
        return (jnp.sum(jnp.where(is_ctx, mf, 0.0), axis=1, keepdims=True),
                jnp.sum(jnp.where(is_ctx, 0.0, mf), axis=1, keepdims=True))

    def search(i, carry):
        pc, pt = carry
        bit = jnp.left_shift(jnp.int32(1), 30 - i)
        cc, ct = counts(bits >= jnp.where(is_ctx, pc | bit, pt | bit))
        return jnp.where(cc >= cap_ctx, pc | bit, pc), jnp.where(ct >= cap_lat, pt | bit, pt)

    z = jnp.zeros((ne, 1), jnp.int32)
    pc, pt = lax.fori_loop(0, 31, search, (z, z))
    thr = jnp.where(is_ctx, pc, pt)
    gt = bits > thr
    eq = bits == thr
    gc, gl = counts(gt)
    need = jnp.where(is_ctx, cap_ctx - gc, cap_lat - gl)

    nb = ta // LANE
    ut = jnp.where(lax.broadcasted_iota(jnp.int32, (LANE, LANE), 0) <= lax.broadcasted_iota(jnp.int32, (LANE, LANE), 1),
                   1.0, 0.0).astype(BF16)

    def lane_prefix(mask):
        mf = jnp.where(mask, 1.0, 0.0).astype(BF16)
        blocks = jnp.concatenate([mf[:, j * LANE:(j + 1) * LANE] for j in range(nb)], axis=0)
        inc = _dot(blocks, ut)
        outs = []
        off = jnp.zeros((ne, 1), F32)
        for j in range(nb):
            if j * LANE == ctx_len:
                off = jnp.zeros((ne, 1), F32)
            blk = inc[j * ne:(j + 1) * ne]
            outs.append(blk + off)
            off = off + blk[:, LANE - 1:LANE]
        return jnp.concatenate(outs, axis=1)

    sel = gt | (eq & (lane_prefix(eq) <= need))
    slot = lane_prefix(sel) - 1.0 + jnp.where(is_ctx, 0.0, float(cap_ctx))
    pos_ref[...] = jnp.where(sel, slot, -1.0)
    gate_ref[...] = jnp.where(sel, probs, 0.0)


def _moe_select(logits_t, ctx_len, cap_ctx, cap_lat):
    b, ne, ta = logits_t.shape
    spec = pl.BlockSpec((None, ne, ta), lambda i: (i, 0, 0))
    return pl.pallas_call(
        functools.partial(_select_kernel, ctx_len=ctx_len, cap_ctx=cap_ctx, cap_lat=cap_lat),
        out_shape=(jax.ShapeDtypeStruct((b, ne, ta), F32), jax.ShapeDtypeStruct((b, ne, ta), F32)),
        grid=(b,),
        in_specs=[spec],
        out_specs=(spec, spec),
        compiler_params=_cparams("parallel"),
        name="moe_select",
    )(logits_t)


def _gather_kernel(pos_ref, h_ref, o_ref, acc_ref):
    k = pl.program_id(2)
    nslot = o_ref.shape[0]
    tk = h_ref.shape[0]
    slot = lax.broadcasted_iota(jnp.int32, (nslot, tk), 0)
    onehot = jnp.where(slot == pos_ref[...].astype(jnp.int32), 1.0, 0.0).astype(BF16)
    part = _dot(onehot, h_ref[...])

    @pl.when(k == 0)
    def _():
        acc_ref[...] = part

    @pl.when(k > 0)
    def _():
        acc_ref[...] += part

    @pl.when(k == pl.num_programs(2) - 1)
    def _():
        o_ref[...] = acc_ref[...].astype(BF16)


def _moe_gather(pos_t, h, nslot):
    b, ne, ta = pos_t.shape
    d = h.shape[-1]
    kt = 2
    tk = ta // kt
    return pl.pallas_call(
        _gather_kernel,
        out_shape=jax.ShapeDtypeStruct((ne, b, nslot, d), BF16),
        grid=(b, ne, kt),
        in_specs=[pl.BlockSpec((None, None, 1, tk), lambda i, e, k: (i, e, 0, k)),
                  pl.BlockSpec((None, tk, d), lambda i, e, k: (i, k, 0))],
        out_specs=pl.BlockSpec((None, None, nslot, d), lambda i, e, k: (e, i, 0, 0)),
        scratch_shapes=[pltpu.VMEM((nslot, d), F32)],
        compiler_params=_cparams("parallel", "parallel", "arbitrary"),
        name="moe_gather",
    )(pos_t.reshape(b, ne, 1, ta), h)


def _ffn_kernel(x_ref, wg_ref, wu_ref, wd_ref, o_ref, acc_ref):
    f = pl.program_id(2)
    x = x_ref[...]
    g = _dot(x, wg_ref[...].astype(BF16))
    u = _dot(x, wu_ref[...].astype(BF16))
    part = _dot((_silu(g) * u).astype(BF16), wd_ref[...].astype(BF16))

    @pl.when(f == 0)
    def _():
        acc_ref[...] = part

    @pl.when(f > 0)
    def _():
        acc_ref[...] += part

    @pl.when(f == pl.num_programs(2) - 1)
    def _():
        o_ref[...] = acc_ref[...].astype(BF16)


def _moe_ffn(xin, w_gate, w_up, w_down, tf=256):
    ne, ns, m, d = xin.shape
    ff = w_gate.shape[-1]
    return pl.pallas_call(
        _ffn_kernel,
        out_shape=jax.ShapeDtypeStruct((ne, ns, m, d), BF16),
        grid=(ne, ns, ff // tf),
        in_specs=[pl.BlockSpec((None, None, m, d), lambda e, s, f: (e, s, 0, 0)),
                  pl.BlockSpec((None, d, tf), lambda e, s, f: (e, 0, f)),
                  pl.BlockSpec((None, d, tf), lambda e, s, f: (e, 0, f)),
                  pl.BlockSpec((None, tf, d), lambda e, s, f: (e, f, 0))],
        out_specs=pl.BlockSpec((None, None, m, d), lambda e, s, f: (e, s, 0, 0)),
        scratch_shapes=[pltpu.VMEM((m, d), F32)],
        compiler_params=_cparams("parallel", "parallel", "arbitrary"),
        name="moe_ffn",
    )(xin, w_gate, w_up, w_down)


def _combine_kernel(pos_ref, gate_ref, y_ref, x_ref, m_ref, g_ref, o_ref, acc_ref, *, ctx_len):
    j = pl.program_id(1)
    e = pl.program_id(2)
    tt, ne = pos_ref.shape
    nslot = y_ref.shape[0]
    mine = lax.broadcasted_iota(jnp.int32, (tt, ne), 1) == e
    pe = jnp.sum(jnp.where(mine, pos_ref[...], 0.0), axis=1, keepdims=True).astype(jnp.int32)
    ge = jnp.sum(jnp.where(mine, gate_ref[...], 0.0), axis=1, keepdims=True)
    onehot = jnp.where(lax.broadcasted_iota(jnp.int32, (tt, nslot), 1) == pe, 1.0, 0.0).astype(BF16)
    part = _dot(onehot, y_ref[...]) * ge

    @pl.when(e == 0)
    def _():
        acc_ref[...] = part

    @pl.when(e > 0)
    def _():
        acc_ref[...] += part

    @pl.when(e == pl.num_programs(2) - 1)
    def _():
        m = m_ref[...]
        is_ctx = (lax.broadcasted_iota(jnp.int32, (tt, 1), 0) + j * tt) < ctx_len
        gate = jnp.where(is_ctx, m[0, 5:6], m[1, 5:6])
        o_ref[...] = x_ref[...] + gate * (_rms(acc_ref[...]) * g_ref[...])


def _moe_combine(pos, gate, yout, xa, modtab, g_post, ctx_len):
    b, ta, d = xa.shape
    ne, _, nslot, _ = yout.shape
    tt = ta // 8
    tok = lambda w: pl.BlockSpec((None, tt, w), lambda i, j, e: (i, j, 0))
    return pl.pallas_call(
        functools.partial(_combine_kernel, ctx_len=ctx_len),
        out_shape=jax.ShapeDtypeStruct((b, ta, d), F32),
        grid=(b, ta // tt, ne),
        in_specs=[tok(ne), tok(ne),
                  pl.BlockSpec((None, None, nslot, d), lambda i, j, e: (e, i, 0, 0)),
                  tok(d),
                  pl.BlockSpec((None, 2, 6, d), lambda i, j, e: (i, 0, 0, 0)),
                  pl.BlockSpec((1, d), lambda i, j, e: (0, 0))],
        out_specs=tok(d),
        scratch_shapes=[pltpu.VMEM((tt, d), F32)],
        compiler_params=_cparams("parallel", "parallel", "arbitrary"),
        name="moe_combine",
    )(pos, gate, yout, xa, modtab, g_post.reshape(1, d))


def _ec_moe(logits_t, h, xa, modtab, g_post, w_gate, w_up, w_down, ctx_len):
    b, ne, ta = logits_t.shape
    d = h.shape[-1]
    cap_ctx = max(1, EC_CAPACITY * ctx_len // ne)
    cap_lat = max(1, EC_CAPACITY * (ta - ctx_len) // ne)
    nslot = cap_ctx + cap_lat
    pos_t, gate_t = _moe_select(logits_t, ctx_len, cap_ctx, cap_lat)
    xin = _moe_gather(pos_t, h, nslot)
    pair = 2 if b % 2 == 0 else 1
    yout = _moe_ffn(xin.reshape(ne, b // pair, pair * nslot, d), w_gate, w_up, w_down)
    yout = yout.reshape(ne, b, nslot, d)
    return _moe_combine(jnp.swapaxes(pos_t, 1, 2), jnp.swapaxes(gate_t, 1, 2), yout, xa, modtab, g_post, ctx_len)


def kernel(x, c, ctx, c_ctx, w_mod, b_mod, g_mix_pre, g_mix_post, g_ffn_pre, g_ffn_post, w_router, w_exp_gate, w_exp_up, w_exp_down, ev_w_in, ev_w_out, hgrn_lb, hgrn_g_norm, na_rpb, od_w_in, od_w_out, q_norm, k_norm, s5_a_re, s5_a_im, s5_log_dt, s5_b_re, s5_b_im, s5_c_re, s5_c_im, s5_d, s5_w_glu, s5_b_glu):
    b, t, d = x.shape
    ctx_len = ctx.shape[1]
    depth = w_mod.shape[0]
    assert depth == 2 and b <= 7
    ta = ctx_len + t
    a_width = d // 2
    s5_width = d // 4
    cq_width = d - s5_width
    ckv_width = cq_width // 3

    xa = jnp.concatenate([ctx, x], axis=1)
    cc = jnp.concatenate([c, c_ctx[None], jnp.zeros((7 - b, d), F32)], axis=0)
    mod = _modulation(cc, w_mod, b_mod)
    mod_lat = mod[:, :b].reshape(depth, b, 1, 6, d)
    mod_ctx = jnp.broadcast_to(mod[:, b].reshape(depth, 1, 1, 6, d), (depth, b, 1, 6, d))
    modtab = jnp.concatenate([mod_ctx, mod_lat], axis=2)

    lb_all = jnp.cumsum(jax.nn.softmax(hgrn_lb.astype(F32), axis=0), axis=0)
    hx = _prenorm(xa, g_mix_pre[0], modtab[0])
    p = _matmul(hx.reshape(b * ta, d), ev_w_in[0]).reshape(b, ta, -1)
    o_f = _hgrn_dir(p, lb_all[0, 0:1], ctx_len, a_width, rev=False)
    o_b = _hgrn_dir(p, lb_all[0, 1:2], ctx_len, a_width, rev=True)
    mix_a = _hgrn_readout(o_f, o_b, p, hgrn_g_norm[0], a_width)
    mix_b = _natten(p, _natten_bias(na_rpb[0]), ctx_len, 5 * a_width, d - a_width)
    xa, h2, logits_t = _postmix(mix_a, mix_b, ev_w_out[0], xa, modtab[0], g_mix_post[0], g_ffn_pre[0], w_router[0])
    xa = _ec_moe(logits_t, h2, xa, modtab[0], g_ffn_post[0], w_exp_gate[0], w_exp_up[0], w_exp_down[0], ctx_len)

    hx = _prenorm(xa, g_mix_pre[1], modtab[1])
    p = _matmul(hx.reshape(b * ta, d), od_w_in[0]).reshape(b, ta, -1)
    cosf, sinf = _rope_tables(ctx_len, t)
    qn, kn, vb = _qkprep(p, cosf, sinf, q_norm[0], k_norm[0], cq_width, ckv_width)
    mix_a = _gqa_attention(qn, kn, vb, cq_width // ckv_width)
    ops = _s5_operators(s5_a_re[0], s5_a_im[0], s5_log_dt[0], s5_b_re[0], s5_b_im[0], s5_c_re[0], s5_c_im[0])
    mix_b = _s5(p, cq_width + 2 * ckv_width, s5_width, ctx_len, ops, s5_d[0], s5_w_glu[0], s5_b_glu[0])
    xa, h2, logits_t = _postmix(mix_a, mix_b, od_w_out[0], xa, modtab[1], g_mix_post[1], g_ffn_pre[1], w_router[1])
    xa = _ec_moe(logits_t, h2, xa, modtab[1], g_ffn_post[1], w_exp_gate[1], w_exp_up[1], w_exp_down[1], ctx_len)
    return xa[:, ctx_len:]
```

```python
import functools
import math

import jax
import jax.numpy as jnp
from jax import lax
from jax.experimental import pallas as pl
from jax.experimental.pallas import tpu as pltpu

F32 = jnp.float32
BF16 = jnp.bfloat16
HIGHEST = lax.Precision.HIGHEST
EPS = 1e-6

LANE = 128
TOK_TILE = 256
VMEM_LIMIT = 52 << 20

GRID_W = 64
NA_ROWS = 8
NA_COLS = 16
NA_GROUP = 4
NA_SPAN = 12
HEAD = 128
A_CHUNK = 64
N_EXPERTS = 16
EC_CAPACITY = 2
S5_GROUP = 16
S5_STATE = 64
S5_CHUNK = 16
ROPE_THETA = 10000.0
NEG_BIG = -1e30
LOG2E = 1.4426950408889634


def _cparams(*sem):
    return pltpu.CompilerParams(dimension_semantics=sem, vmem_limit_bytes=VMEM_LIMIT)


def _dot(a, b):
    return jnp.dot(a, b, preferred_element_type=F32)


def _dot_nt(a, b):
    return lax.dot_general(a, b, (((1,), (1,)), ((), ())), preferred_element_type=F32)


def _dot_tn(a, b):
    return lax.dot_general(a, b, (((0,), (0,)), ((), ())), preferred_element_type=F32)


def _rms(x):
    return x * lax.rsqrt(jnp.mean(x * x, axis=-1, keepdims=True) + EPS)


def _silu(x):
    return x * jax.nn.sigmoid(x)


def _mod_kernel(c_ref, w_ref, b_ref, o_ref):
    o_ref[...] = jnp.dot(_silu(c_ref[...]), w_ref[...], preferred_element_type=F32, precision=HIGHEST) + b_ref[...]


def _modulation(cc, w_mod, b_mod):
    depth, d, n = w_mod.shape
    tn = 1024
    return pl.pallas_call(
        _mod_kernel,
        out_shape=jax.ShapeDtypeStruct((depth, cc.shape[0], n), F32),
        grid=(depth, n // tn),
        in_specs=[pl.BlockSpec(cc.shape, lambda l, j: (0, 0)),
                  pl.BlockSpec((None, d, tn), lambda l, j: (l, 0, j)),
                  pl.BlockSpec((None, 1, tn), lambda l, j: (l, 0, j))],
        out_specs=pl.BlockSpec((None, cc.shape[0], tn), lambda l, j: (l, 0, j)),
        compiler_params=_cparams("arbitrary", "arbitrary"),
        name="modulation",
    )(cc, w_mod, b_mod.reshape(depth, 1, n))


def _mod_spec(d):
    return pl.BlockSpec((None, None, 6, d), lambda b, j: (b, jnp.minimum(j, 1), 0, 0))


def _prenorm_kernel(x_ref, g_ref, m_ref, o_ref):
    m = m_ref[...]
    y = _rms(x_ref[...]) * g_ref[...]
    o_ref[...] = (y * (1.0 + m[1:2]) + m[0:1]).astype(BF16)


def _prenorm(xa, g, modtab):
    b, ta, d = xa.shape
    return pl.pallas_call(
        _prenorm_kernel,
        out_shape=jax.ShapeDtypeStruct((b, ta, d), BF16),
        grid=(b, ta // TOK_TILE),
        in_specs=[pl.BlockSpec((None, TOK_TILE, d), lambda i, j: (i, j, 0)),
                  pl.BlockSpec((1, d), lambda i, j: (0, 0)),
                  _mod_spec(d)],
        out_specs=pl.BlockSpec((None, TOK_TILE, d), lambda i, j: (i, j, 0)),
        compiler_params=_cparams("parallel", "parallel"),
        name="prenorm",
    )(xa, g.reshape(1, d), modtab)


def _mm_kernel(a_ref, w_ref, o_ref, wb_ref):
    @pl.when(pl.program_id(1) == 0)
    def _():
        wb_ref[...] = w_ref[...].astype(BF16)

    o_ref[...] = _dot(a_ref[...], wb_ref[...]).astype(o_ref.dtype)


def _matmul(a, w, tm=512, tn=1024, out_dtype=F32):
    m, k = a.shape
    n = w.shape[1]
    return pl.pallas_call(
        _mm_kernel,
        out_shape=jax.ShapeDtypeStruct((m, n), out_dtype),
        grid=(n // tn, m // tm),
        in_specs=[pl.BlockSpec((tm, k), lambda j, i: (i, 0)),
                  pl.BlockSpec((k, tn), lambda j, i: (0, j))],
        out_specs=pl.BlockSpec((tm, tn), lambda j, i: (i, j)),
        scratch_shapes=[pltpu.VMEM((k, tn), BF16)],
        compiler_params=_cparams("arbitrary", "arbitrary"),
        name="proj_in",
    )(a, w)


def _hgrn_chunk(q, fr, v, lb, st, rev, chunk):
    f = lb + (1.0 - lb) * jax.nn.sigmoid(fr)
    kk = 1.0 - f
    lf = jnp.log(f)
    row = lax.broadcasted_iota(jnp.int32, (chunk, LANE), 0)
    ti = lax.broadcasted_iota(jnp.int32, (chunk, chunk), 0)
    si = lax.broadcasted_iota(jnp.int32, (chunk, chunk), 1)
    att = jnp.where(ti == si, _dot_nt(q.astype(BF16), kk.astype(BF16)), 0.0)
    p_in = lf
    r_ex = jnp.zeros_like(lf)
    tot = lf
    for lvl in range(chunk.bit_length() - 1):
        step = 1 << lvl
        bit = ((row >> lvl) & 1) == 1
        up = pltpu.roll(tot, step, 0)
        dn = pltpu.roll(tot, chunk - step, 0)
        is_q = jnp.logical_not(bit) if rev else bit
        qf = jnp.where(is_q, jnp.exp(p_in) * q, 0.0).astype(BF16)
        kf = jnp.where(is_q, 0.0, jnp.exp(r_ex) * kk).astype(BF16)
        same = (ti >> (lvl + 1)) == (si >> (lvl + 1))
        att = att + jnp.where(same, _dot_nt(qf, kf), 0.0)
        if rev:
            p_in = p_in + jnp.where(bit, 0.0, dn)
            r_ex = r_ex + jnp.where(bit, up, 0.0)
        else:
            p_in = p_in + jnp.where(bit, up, 0.0)
            r_ex = r_ex + jnp.where(bit, 0.0, dn)
        tot = tot + jnp.where(bit, up, dn)
    vb = v.astype(BF16)
    o = _dot_nt((q * jnp.exp(p_in)).astype(BF16), st.astype(BF16)) + _dot(att.astype(BF16), vb)
    kd = (kk * jnp.exp(r_ex)).astype(BF16)
    st_new = st * jnp.exp(tot[0:1]) + _dot_tn(vb, kd)
    return o, st_new


def _hgrn_kernel(qf_ref, ff_ref, vf_ref, qb_ref, fb_ref, vb_ref, lb_ref, of_ref, ob_ref, st_ref, *, chunk, hb):
    @pl.when(pl.program_id(2) == 0)
    def _():
        st_ref[...] = jnp.zeros_like(st_ref)

    for h in range(hb):
        sl = slice(h * HEAD, (h + 1) * HEAD)
        o, st = _hgrn_chunk(qf_ref[:, sl], ff_ref[:, sl], vf_ref[:, sl], lb_ref[0:1, sl], st_ref[0, h], False, chunk)
        of_ref[:, sl] = o
        st_ref[0, h] = st
        o, st = _hgrn_chunk(qb_ref[:, sl], fb_ref[:, sl], vb_ref[:, sl], lb_ref[1:2, sl], st_ref[1, h], True, chunk)
        ob_ref[:, sl] = o
        st_ref[1, h] = st


def _hgrn(p, lb, ctx_len, width, hb=2):
    b, ta, _ = p.shape
    chunk = A_CHUNK
    n = ta // chunk
    nc = ctx_len // chunk
    cw = HEAD * hb
    nh = width // cw
    bwd = lambda j: jnp.where(j < nc, nc - 1 - j, n - 1 + nc - j)

    def spec(base, rev):
        return pl.BlockSpec((None, chunk, cw), lambda i, h, j: (i, bwd(j) if rev else j, base // cw + h))

    out = jax.ShapeDtypeStruct((b, ta, width), F32)
    return pl.pallas_call(
        functools.partial(_hgrn_kernel, chunk=chunk, hb=hb),
        out_shape=(out, out),
        grid=(b, nh, n),
        in_specs=[spec(0, False), spec(width, False), spec(3 * width, False),
                  spec(0, True), spec(2 * width, True), spec(3 * width, True),
                  pl.BlockSpec((2, cw), lambda i, h, j: (0, h))],
        out_specs=(spec(0, False), spec(0, True)),
        scratch_shapes=[pltpu.VMEM((2, hb, HEAD, HEAD), F32)],
        compiler_params=_cparams("parallel", "parallel", "arbitrary"),
        name="hgrn_scan",
    )(p, p, p, p, p, p, lb)


def _hgrn_readout_kernel(of_ref, ob_ref, g_ref, gn_ref, o_ref):
    for h in range(o_ref.shape[-1] // HEAD):
        sl = slice(h * HEAD, (h + 1) * HEAD)
        o = of_ref[:, sl] + ob_ref[:, sl]
        o_ref[:, sl] = (_rms(o) * gn_ref[...] * _silu(g_ref[:, sl])).astype(BF16)


def _hgrn_readout(o_f, o_b, p, g_norm, width):
    b, ta, _ = o_f.shape
    spec = pl.BlockSpec((None, TOK_TILE, width), lambda i, j: (i, j, 0))
    return pl.pallas_call(
        _hgrn_readout_kernel,
        out_shape=jax.ShapeDtypeStruct((b, ta, width), BF16),
        grid=(b, ta // TOK_TILE),
        in_specs=[spec, spec,
                  pl.BlockSpec((None, TOK_TILE, width), lambda i, j: (i, j, 4)),
                  pl.BlockSpec((1, HEAD), lambda i, j: (0, 0))],
        out_specs=spec,
        compiler_params=_cparams("parallel", "parallel"),
        name="hgrn_readout",
    )(o_f, o_b, p, g_norm.reshape(1, HEAD))


def _natten_bias(rpb):
    nrow, ncol = 2 * NA_ROWS - 1, 2 * NA_COLS - 1
    col = jnp.arange(GRID_W)
    col_start = jnp.clip(col - NA_COLS // 2, 0, GRID_W - NA_COLS)
    cmask = (col[None, :] >= col_start[:, None]) & (col[None, :] < col_start[:, None] + NA_COLS)
    dc = col[None, :] - col[:, None] + NA_COLS - 1
    cm = (cmask[:, :, None] & (dc[:, :, None] == jnp.arange(ncol))).astype(F32)
    cls = jnp.arange(3)[:, None, None]
    i = jnp.arange(NA_GROUP)[None, :, None]
    w = jnp.arange(NA_SPAN)[None, None, :]
    first_w = jnp.where(cls == 0, 0, jnp.where(cls == 1, i, NA_SPAN - NA_ROWS))
    valid = (w >= first_w) & (w < first_w + NA_ROWS)
    span0 = jnp.where(cls == 0, 0, jnp.where(cls == 1, -(NA_ROWS // 2), NA_GROUP - NA_SPAN))
    dr = span0 + w - i + NA_ROWS - 1
    rm = (valid[..., None] & (dr[..., None] == jnp.arange(nrow))).astype(F32)
    t = jnp.einsum('qkb,hab->haqk', cm, rpb.astype(F32), precision=HIGHEST)
    t = jnp.einsum('ciwa,haqk->hciqwk', rm, t, precision=HIGHEST)
    ok = valid[None, :, :, None, :, None] & cmask[None, None, None, :, None, :]
    t = jnp.where(ok, t, NEG_BIG)
    return t.reshape(rpb.shape[0], 3, NA_GROUP * GRID_W, NA_SPAN * GRID_W)


def _natten_kernel(q_ref, k_ref, v_ref, bias_ref, o_ref, *, ctx_len, rows):
    j = pl.program_id(2)
    nst = rows // NA_GROUP
    scale = HEAD ** -0.5
    qb = q_ref[...].astype(BF16)
    kc = k_ref[0:ctx_len, :].astype(BF16)
    vc = v_ref[0:ctx_len, :].astype(BF16)
    s_ctx = _dot_nt(qb, kc) * scale
    m_ctx = jnp.max(s_ctx, axis=1, keepdims=True)

    @pl.when(j < nst)
    def _():
        last = j == nst - 1
        cls = jnp.where(j == 0, 0, jnp.where(last, 2, 1))
        span_row = jnp.where(j == 0, 0, jnp.where(last, rows - NA_SPAN, NA_GROUP * j - NA_ROWS // 2))
        start = pl.multiple_of(ctx_len + span_row * GRID_W, GRID_W)
        kw = k_ref[pl.ds(start, NA_SPAN * GRID_W), :].astype(BF16)
        vw = v_ref[pl.ds(start, NA_SPAN * GRID_W), :].astype(BF16)
        s_win = _dot_nt(qb, kw) * scale + bias_ref[cls]
        m = jnp.maximum(m_ctx, jnp.max(s_win, axis=1, keepdims=True))
        pw = jnp.exp(s_win - m)
        pc = jnp.exp(s_ctx - m)
        den = jnp.sum(pw, axis=1, keepdims=True) + jnp.sum(pc, axis=1, keepdims=True)
        o = _dot(pw.astype(BF16), vw) + _dot(pc.astype(BF16), vc)
        o_ref[...] = (o / den).astype(BF16)

    @pl.when(j >= nst)
    def _():
        pc = jnp.exp(s_ctx - m_ctx)
        o = _dot(pc.astype(BF16), vc) / jnp.sum(pc, axis=1, keepdims=True)
        o_ref[...] = o.astype(BF16)


def _natten(p, bias, ctx_len, col0, width):
    b, ta, _ = p.shape
    nh = width // HEAD
    rows = (ta - ctx_len) // GRID_W
    tq = NA_GROUP * GRID_W
    nst = rows // NA_GROUP
    ncq = ctx_len // tq
    assert rows % NA_GROUP == 0 and rows >= NA_SPAN and ctx_len % tq == 0

    def qmap(cb):
        return lambda i, h, j: (i, jnp.where(j < nst, ncq + j, j - nst), cb + h)

    return pl.pallas_call(
        functools.partial(_natten_kernel, ctx_len=ctx_len, rows=rows),
        out_shape=jax.ShapeDtypeStruct((b, ta, width), BF16),
        grid=(b, nh, nst + ncq),
        in_specs=[pl.BlockSpec((None, tq, HEAD), qmap(col0 // HEAD)),
                  pl.BlockSpec((None, ta, HEAD), lambda i, h, j: (i, 0, (col0 + width) // HEAD + h)),
                  pl.BlockSpec((None, ta, HEAD), lambda i, h, j: (i, 0, (col0 + 2 * width) // HEAD + h)),
                  pl.BlockSpec((None, 3, tq, NA_SPAN * GRID_W), lambda i, h, j: (h, 0, 0, 0))],
        out_specs=pl.BlockSpec((None, tq, HEAD), qmap(0)),
        compiler_params=_cparams("parallel", "parallel", "arbitrary"),
        name="natten",
    )(p, p, p, bias)


def _rope_tables(ctx_len, t):
    pos = jnp.arange(t)
    row = (pos // GRID_W).astype(F32)
    col = (pos % GRID_W).astype(F32)
    half = HEAD // 2
    inv = ROPE_THETA ** (-jnp.arange(0, half, 2, dtype=F32) / half)
    ang = jnp.concatenate([row[:, None] * inv, col[:, None] * inv], axis=-1)
    cos, sin = jnp.cos(ang), jnp.sin(ang)
    cosf = jnp.repeat(cos, 2, axis=-1)
    sinf = jnp.stack([-sin, sin], axis=-1).reshape(t, HEAD)
    cosf = jnp.concatenate([jnp.ones((ctx_len, HEAD), F32), cosf], axis=0)
    sinf = jnp.concatenate([jnp.zeros((ctx_len, HEAD), F32), sinf], axis=0)
    return cosf, sinf


def _qkprep_kernel(q_ref, k_ref, v_ref, cos_ref, sin_ref, qn_ref, kn_ref, qo_ref, ko_ref, vo_ref):
    cosf = cos_ref[...]
    sinf = sin_ref[...]
    even = (lax.broadcasted_iota(jnp.int32, cosf.shape, 1) & 1) == 0

    def rope(x):
        partner = jnp.where(even, pltpu.roll(x, LANE - 1, 1), pltpu.roll(x, 1, 1))
        return x * cosf + partner * sinf

    for h in range(q_ref.shape[-1] // HEAD):
        sl = slice(h * HEAD, (h + 1) * HEAD)
        qo_ref[:, sl] = (rope(_rms(q_ref[:, sl]) * qn_ref[...]) * (HEAD ** -0.5 * LOG2E)).astype(BF16)
    for h in range(k_ref.shape[-1] // HEAD):
        sl = slice(h * HEAD, (h + 1) * HEAD)
        ko_ref[:, sl] = rope(_rms(k_ref[:, sl]) * kn_ref[...]).astype(BF16)
    ones = jnp.ones((v_ref.shape[0], HEAD), BF16)
    for h in range(v_ref.shape[-1] // HEAD):
        vo_ref[:, 2 * h * HEAD:(2 * h + 1) * HEAD] = v_ref[:, h * HEAD:(h + 1) * HEAD].astype(BF16)
        vo_ref[:, (2 * h + 1) * HEAD:(2 * h + 2) * HEAD] = ones


def _qkprep(p, cosf, sinf, q_norm, k_norm, qw, kw):
    b, ta, _ = p.shape
    tt = TOK_TILE

    def spec(w, cb):
        return pl.BlockSpec((None, tt, w), lambda i, j: (i, j, cb))

    return pl.pallas_call(
        _qkprep_kernel,
        out_shape=(jax.ShapeDtypeStruct((b, ta, qw), BF16),
                   jax.ShapeDtypeStruct((b, ta, kw), BF16),
                   jax.ShapeDtypeStruct((b, ta, 2 * kw), BF16)),
        grid=(b, ta // tt),
        in_specs=[spec(qw, 0), spec(kw, qw // kw), spec(kw, qw // kw + 1),
                  pl.BlockSpec((tt, HEAD), lambda i, j: (j, 0)),
                  pl.BlockSpec((tt, HEAD), lambda i, j: (j, 0)),
                  pl.BlockSpec((1, HEAD), lambda i, j: (0, 0)),
                  pl.BlockSpec((1, HEAD), lambda i, j: (0, 0))],
        out_specs=(spec(qw, 0), spec(kw, 0), spec(2 * kw, 0)),
        compiler_params=_cparams("parallel", "parallel"),
        name="qk_prep",
    )(p, p, p, cosf, sinf, q_norm.reshape(1, HEAD), k_norm.reshape(1, HEAD))


def _attn_kernel(q_ref, k_ref, v_ref, o_ref):
    s = _dot_nt(q_ref[...], k_ref[...])
    p = jnp.exp2(s - jnp.max(s, axis=1, keepdims=True))
    o = _dot(p.astype(BF16), v_ref[...])
    o_ref[...] = (o[:, 0:HEAD] / o[:, HEAD:HEAD + 1]).astype(BF16)


def _gqa_attention(qn, kn, vb, group):
    b, ta, qw = qn.shape
    nkv = kn.shape[-1] // HEAD
    tq = TOK_TILE
    return pl.pallas_call(
        _attn_kernel,
        out_shape=jax.ShapeDtypeStruct((b, ta, qw), BF16),
        grid=(b, nkv, group, ta // tq),
        in_specs=[pl.BlockSpec((None, tq, HEAD), lambda i, h, g, j: (i, j, h * group + g)),
                  pl.BlockSpec((None, ta, HEAD), lambda i, h, g, j: (i, 0, h)),
                  pl.BlockSpec((None, ta, 2 * HEAD), lambda i, h, g, j: (i, 0, h))],
        out_specs=pl.BlockSpec((None, tq, HEAD), lambda i, h, g, j: (i, j, h * group + g)),
        compiler_params=_cparams("parallel", "parallel", "parallel", "parallel"),
        name="gqa_attention",
    )(qn, kn, vb)


def _s5_operators(a_re, a_im, log_dt, b_re, b_im, c_re, c_im):
    lc, pp, ns = S5_CHUNK, S5_GROUP, S5_STATE
    a_c = lax.complex(a_re.astype(F32), a_im.astype(F32))
    adt = a_c * jnp.exp(log_dt.astype(F32))[..., None]
    a_bar = jnp.exp(adt)
    b_bar = ((a_bar - 1.0) / a_c)[..., None] * lax.complex(b_re.astype(F32), b_im.astype(F32))
    c_mat = lax.complex(c_re.astype(F32), c_im.astype(F32))
    nd, g = a_re.shape[0], a_re.shape[1]
    pos = jnp.arange(lc)
    lag = pos[None, :] - pos[:, None]
    lag = jnp.stack([lag, -lag])
    live = (lag >= 0)[:, None, :, :, None]
    apl = jnp.where(live, jnp.exp(adt[:, :, None, None, :] * jnp.maximum(lag, 0).astype(F32)[:, None, :, :, None]), 0.0)
    tm = jnp.real(jnp.einsum('dgpn,dgion,dgnq->dgiqop', c_mat, apl, b_bar, precision=HIGHEST))
    tmat = tm.reshape(nd, g, lc * pp, lc * pp)
    steps_after = jnp.stack([lc - 1 - pos, pos]).astype(F32)
    gm = jnp.exp(adt[:, :, None, :] * steps_after[:, None, :, None])[:, :, :, None, :] * jnp.swapaxes(b_bar, 2, 3)[:, :, None]
    gm = gm.reshape(nd, g, lc * pp, ns)
    steps_upto = jnp.stack([pos + 1, lc - pos]).astype(F32)
    hm = c_mat[:, :, None] * jnp.exp(adt[:, :, None, :] * steps_upto[:, None, :, None])[:, :, :, None, :]
    hm = jnp.transpose(hm, (0, 1, 4, 2, 3)).reshape(nd, g, ns, lc * pp)
    hmat = jnp.concatenate([jnp.real(hm), -jnp.imag(hm)], axis=2)
    al = jnp.exp(adt * float(lc))
    apow = jnp.concatenate([jnp.real(al), jnp.imag(al)], axis=-1)[:, :, None, :]
    gd = lambda x: jnp.swapaxes(x, 0, 1)
    return (gd(tmat).astype(BF16), gd(jnp.real(gm)).astype(BF16), gd(jnp.imag(gm)).astype(BF16),
            gd(hmat).astype(BF16), gd(apow))


def _s5_kernel(u_ref, t_ref, gr_ref, gi_ref, h_ref, a_ref, d_ref, y_ref, er_ref, ei_ref, xr_ref, xi_ref,
               *, nchunk, nctx, rpc):
    ns = S5_STATE
    u = u_ref[...]
    for dr in range(2):
        er_ref[dr] = _dot(u, gr_ref[dr])
        ei_ref[dr] = _dot(u, gi_ref[dr])

    def body(c, carry):
        out = []
        for dr in range(2):
            xr, xi = carry[2 * dr], carry[2 * dr + 1]
            pc = c if dr == 0 else jnp.where(c < nctx, nctx - 1 - c, nchunk - 1 + nctx - c)
            off = pl.multiple_of(pc * rpc, rpc)
            xr_ref[dr, pl.ds(off, rpc), :] = xr
            xi_ref[dr, pl.ds(off, rpc), :] = xi
            ar = a_ref[dr, :, 0:ns]
            ai = a_ref[dr, :, ns:2 * ns]
            out.append(ar * xr - ai * xi + er_ref[dr, pl.ds(off, rpc), :])
            out.append(ar * xi + ai * xr + ei_ref[dr, pl.ds(off, rpc), :])
        return tuple(out)

    zero = jnp.zeros((rpc, ns), F32)
    lax.fori_loop(0, nchunk, body, (zero, zero, zero, zero), unroll=4)
    y = d_ref[...] * u.astype(F32)
    for dr in range(2):
        y = y + _dot(u, t_ref[dr])
        y = y + _dot(xr_ref[dr].astype(BF16), h_ref[dr, 0:ns, :]) + _dot(xi_ref[dr].astype(BF16), h_ref[dr, ns:2 * ns, :])
    y_ref[...] = y.astype(BF16)


def _s5_scan(u, ops, d_skip, ctx_len):
    tmat, g_re, g_im, hmat, apow = ops
    b, ta, w = u.shape
    lc, pp, ns = S5_CHUNK, S5_GROUP, S5_STATE
    g = w // pp
    nchunk = ta // lc
    rpc = 8
    cw = lc * pp
    rows = nchunk * rpc
    ug = jnp.transpose(u.astype(BF16).reshape(b, nchunk, lc, g, pp), (3, 1, 0, 2, 4))
    ug = jnp.pad(ug, ((0, 0), (0, 0), (0, rpc - b), (0, 0), (0, 0))).reshape(g, rows, cw)
    dvec = jnp.tile(d_skip.astype(F32).reshape(g, 1, pp), (1, lc, 1)).reshape(g, 1, cw)
    op = lambda r, c: pl.BlockSpec((None, 2, r, c), lambda i: (i, 0, 0, 0))
    y = pl.pallas_call(
        functools.partial(_s5_kernel, nchunk=nchunk, nctx=ctx_len // lc, rpc=rpc),
        out_shape=jax.ShapeDtypeStruct((g, rows, cw), BF16),
        grid=(g,),
        in_specs=[pl.BlockSpec((None, rows, cw), lambda i: (i, 0, 0)),
                  op(cw, cw), op(cw, ns), op(cw, ns), op(2 * ns, cw), op(1, 2 * ns),
                  pl.BlockSpec((None, 1, cw), lambda i: (i, 0, 0))],
        out_specs=pl.BlockSpec((None, rows, cw), lambda i: (i, 0, 0)),
        scratch_shapes=[pltpu.VMEM((2, rows, ns), F32) for _ in range(4)],
        compiler_params=_cparams("parallel"),
        name="s5_scan",
    )(ug, tmat, g_re, g_im, hmat, apow, dvec)
    y = y.reshape(g, nchunk, rpc, lc, pp)[:, :, :b]
    return jnp.transpose(y, (2, 1, 3, 0, 4)).reshape(b, ta, w)


def _s5_glu_kernel(y_ref, w_ref, b_ref, o_ref):
    y = y_ref[...].astype(F32)
    y = 0.5 * y * (1.0 + jnp.tanh(math.sqrt(2.0 / math.pi) * (y + 0.044715 * (y * y * y))))
    z = _dot(y.astype(BF16), w_ref[...]) + b_ref[...]
    o_ref[...] = (y * jax.nn.sigmoid(z)).astype(BF16)


def _s5_glu(y, w_glu, b_glu):
    b, ta, w = y.shape
    spec = pl.BlockSpec((None, TOK_TILE, w), lambda i, j: (i, j, 0))
    return pl.pallas_call(
        _s5_glu_kernel,
        out_shape=jax.ShapeDtypeStruct((b, ta, w), BF16),
        grid=(b, ta // TOK_TILE),
        in_specs=[spec, pl.BlockSpec((w, w), lambda i, j: (0, 0)), pl.BlockSpec((1, w), lambda i, j: (0, 0))],
        out_specs=spec,
        compiler_params=_cparams("parallel", "parallel"),
        name="s5_glu",
    )(y, w_glu.astype(BF16), b_glu.reshape(1, w))


def _s5(p, ucol, width, ctx_len, ops, d_skip, w_glu, b_glu):
    y = _s5_scan(p[..., ucol:ucol + width], ops, d_skip, ctx_len)
    return _s5_glu(y, w_glu, b_glu)


def _postmix_kernel(a_ref, b_ref, w_ref, x_ref, m_ref, gpost_ref, gpre_ref, wr_ref, xo_ref, h_ref, lg_ref):
    wa = a_ref.shape[-1]
    y = _dot(a_ref[...], w_ref[0:wa, :]) + _dot(b_ref[...], w_ref[wa:, :])
    m = m_ref[...]
    xn = x_ref[...] + m[2:3] * (_rms(y) * gpost_ref[...])
    xo_ref[...] = xn
    h2 = _rms(xn) * gpre_ref[...] * (1.0 + m[4:5]) + m[3:4]
    hi = h2.astype(BF16)
    h_ref[...] = hi
    lo = (h2 - hi.astype(F32)).astype(BF16)
    ne = lg_ref.shape[-1]
    both = _dot(hi, wr_ref[...])
    lg_ref[...] = both[:, 0:ne] + both[:, ne:2 * ne] + _dot(lo, wr_ref[:, 0:ne])


def _postmix(mix_a, mix_b, w_out, xa, modtab, g_post, g_pre, w_router):
    b, ta, d = xa.shape
    wa, wb = mix_a.shape[-1], mix_b.shape[-1]
    ne = w_router.shape[-1]
    tok = lambda w: pl.BlockSpec((None, TOK_TILE, w), lambda i, j: (i, j, 0))
    vec = pl.BlockSpec((1, d), lambda i, j: (0, 0))
    wr_hi = w_router.astype(BF16)
    wr2 = jnp.concatenate([wr_hi, (w_router - wr_hi.astype(F32)).astype(BF16)], axis=1)
    return pl.pallas_call(
        _postmix_kernel,
        out_shape=(jax.ShapeDtypeStruct((b, ta, d), F32),
                   jax.ShapeDtypeStruct((b, ta, d), BF16),
                   jax.ShapeDtypeStruct((b, ta, ne), F32)),
        grid=(b, ta // TOK_TILE),
        in_specs=[tok(wa), tok(wb),
                  pl.BlockSpec((wa + wb, d), lambda i, j: (0, 0)),
                  tok(d), _mod_spec(d), vec, vec,
                  pl.BlockSpec((d, 2 * ne), lambda i, j: (0, 0))],
        out_specs=(tok(d), tok(d), tok(ne)),
        compiler_params=_cparams("parallel", "parallel"),
        name="mix_out",
    )(mix_a, mix_b, w_out.astype(BF16), xa, modtab, g_post.reshape(1, d), g_pre.reshape(1, d), wr2)


def _select_kernel(lg_ref, pos_ref, gate_ref, *, ctx_len, cap_ctx, cap_lat):
    lg = lg_ref[...]
    ne, ta = lg.shape
    ex = jnp.exp(lg - jnp.max(lg, axis=0, keepdims=True))
    probs = ex / jnp.sum(ex, axis=0, keepdims=True)
    bits = pltpu.bitcast(probs, jnp.int32)
    is_ctx = lax.broadcasted_iota(jnp.int32, (ne, ta), 1) < ctx_len

    def counts(mask):
        mf = jnp.where(mask, 1.0, 0.0)
        return (jnp.sum(jnp.where(is_ctx, mf, 0.0), axis=1, keepdims=True),
                jnp.sum(jnp.where(is_ctx, 0.0, mf), axis=1, keepdims=True))

    def search(i, carry):
        pc, pt = carry
        bit = jnp.left_shift(jnp.int32(1), 30 - i)
        cc, ct = counts(bits >= jnp.where(is_ctx, pc | bit, pt | bit))
        return jnp.where(cc >= cap_ctx, pc | bit, pc), jnp.where(ct >= cap_lat, pt | bit, pt)

    z = jnp.zeros((ne, 1), jnp.int32)
    pc, pt = lax.fori_loop(0, 31, search, (z, z))
    thr = jnp.where(is_ctx, pc, pt)
    gt = bits > thr
    eq = bits == thr
    gc, gl = counts(gt)
    need = jnp.where(is_ctx, cap_ctx - gc, cap_lat - gl)

    nb = ta // LANE
    ut = jnp.where(lax.broadcasted_iota(jnp.int32, (LANE, LANE), 0) <= lax.broadcasted_iota(jnp.int32, (LANE, LANE), 1),
                   1.0, 0.0).astype(BF16)

    def lane_prefix(mask):
        mf = jnp.where(mask, 1.0, 0.0).astype(BF16)
        blocks = jnp.concatenate([mf[:, j * LANE:(j + 1) * LANE] for j in range(nb)], axis=0)
        inc = _dot(blocks, ut)
        outs = []
        off = jnp.zeros((ne, 1), F32)
        for j in range(nb):
            if j * LANE == ctx_len:
                off = jnp.zeros((ne, 1), F32)
            blk = inc[j * ne:(j + 1) * ne]
            outs.append(blk + off)
            off = off + blk[:, LANE - 1:LANE]
        return jnp.concatenate(outs, axis=1)

    sel = gt | (eq & (lane_prefix(eq) <= need))
    slot = lane_prefix(sel) - 1.0 + jnp.where(is_ctx, 0.0, float(cap_ctx))
    pos_ref[...] = jnp.where(sel, slot, -1.0)
    gate_ref[...] = jnp.where(sel, probs, 0.0)


def _moe_select(logits_t, ctx_len, cap_ctx, cap_lat):
    b, ne, ta = logits_t.shape
    spec = pl.BlockSpec((None, ne, ta), lambda i: (i, 0, 0))
    return pl.pallas_call(
        functools.partial(_select_kernel, ctx_len=ctx_len, cap_ctx=cap_ctx, cap_lat=cap_lat),
        out_shape=(jax.ShapeDtypeStruct((b, ne, ta), F32), jax.ShapeDtypeStruct((b, ne, ta), F32)),
        grid=(b,),
        in_specs=[spec],
        out_specs=(spec, spec),
        compiler_params=_cparams("parallel"),
        name="moe_select",
    )(logits_t)


def _gather_kernel(pos_ref, h_ref, o_ref, acc_ref):
    k = pl.program_id(2)
    nslot = o_ref.shape[0]
    tk = h_ref.shape[0]
    slot = lax.broadcasted_iota(jnp.int32, (nslot, tk), 0)
    onehot = jnp.where(slot == pos_ref[...].astype(jnp.int32), 1.0, 0.0).astype(BF16)
    part = _dot(onehot, h_ref[...])

    @pl.when(k == 0)
    def _():
        acc_ref[...] = part

    @pl.when(k > 0)
    def _():
        acc_ref[...] += part

    @pl.when(k == pl.num_programs(2) - 1)
    def _():
        o_ref[...] = acc_ref[...].astype(BF16)


def _moe_gather(pos_t, h, nslot):
    b, ne, ta = pos_t.shape
    d = h.shape[-1]
    kt = 2
    tk = ta // kt
    return pl.pallas_call(
        _gather_kernel,
        out_shape=jax.ShapeDtypeStruct((ne, b, nslot, d), BF16),
        grid=(b, ne, kt),
        in_specs=[pl.BlockSpec((None, None, 1, tk), lambda i, e, k: (i, e, 0, k)),
                  pl.BlockSpec((None, tk, d), lambda i, e, k: (i, k, 0))],
        out_specs=pl.BlockSpec((None, None, nslot, d), lambda i, e, k: (e, i, 0, 0)),
        scratch_shapes=[pltpu.VMEM((nslot, d), F32)],
        compiler_params=_cparams("parallel", "parallel", "arbitrary"),
        name="moe_gather",
    )(pos_t.reshape(b, ne, 1, ta), h)


def _ffn_kernel(x_ref, wg_ref, wu_ref, wd_ref, o_ref, acc_ref):
    f = pl.program_id(2)
    x = x_ref[...]
    g = _dot(x, wg_ref[...].astype(BF16))
    u = _dot(x, wu_ref[...].astype(BF16))
    part = _dot((_silu(g) * u).astype(BF16), wd_ref[...].astype(BF16))

    @pl.when(f == 0)
    def _():
        acc_ref[...] = part

    @pl.when(f > 0)
    def _():
        acc_ref[...] += part

    @pl.when(f == pl.num_programs(2) - 1)
    def _():
        o_ref[...] = acc_ref[...].astype(BF16)


def _moe_ffn(xin, w_gate, w_up, w_down, layer, tf=256):
    ne, ns, m, d = xin.shape
    ff = w_gate.shape[-1]
    return pl.pallas_call(
        _ffn_kernel,
        out_shape=jax.ShapeDtypeStruct((ne, ns, m, d), BF16),
        grid=(ne, ns, ff // tf),
        in_specs=[pl.BlockSpec((None, None, m, d), lambda e, s, f: (e, s, 0, 0)),
                  pl.BlockSpec((None, None, d, tf), lambda e, s, f: (layer, e, 0, f)),
                  pl.BlockSpec((None, None, d, tf), lambda e, s, f: (layer, e, 0, f)),
                  pl.BlockSpec((None, None, tf, d), lambda e, s, f: (layer, e, f, 0))],
        out_specs=pl.BlockSpec((None, None, m, d), lambda e, s, f: (e, s, 0, 0)),
        scratch_shapes=[pltpu.VMEM((m, d), F32)],
        compiler_params=_cparams("parallel", "parallel", "arbitrary"),
        name="moe_ffn",
    )(xin, w_gate, w_up, w_down)


def _combine_kernel(pos_ref, gate_ref, y_ref, x_ref, m_ref, g_ref, o_ref, acc_ref, *, ctx_len):
    j = pl.program_id(1)
    e = pl.program_id(2)
    tt, ne = pos_ref.shape
    nslot = y_ref.shape[0]
    mine = lax.broadcasted_iota(jnp.int32, (tt, ne), 1) == e
    pe = jnp.sum(jnp.where(mine, pos_ref[...], 0.0), axis=1, keepdims=True).astype(jnp.int32)
    ge = jnp.sum(jnp.where(mine, gate_ref[...], 0.0), axis=1, keepdims=True)
    onehot = jnp.where(lax.broadcasted_iota(jnp.int32, (tt, nslot), 1) == pe, 1.0, 0.0).astype(BF16)
    part = _dot(onehot, y_ref[...]) * ge

    @pl.when(e == 0)
    def _():
        acc_ref[...] = part

    @pl.when(e > 0)
    def _():
        acc_ref[...] += part

    @pl.when(e == pl.num_programs(2) - 1)
    def _():
        m = m_ref[...]
        is_ctx = (lax.broadcasted_iota(jnp.int32, (tt, 1), 0) + j * tt) < ctx_len
        gate = jnp.where(is_ctx, m[0, 5:6], m[1, 5:6])
        o_ref[...] = x_ref[...] + gate * (_rms(acc_ref[...]) * g_ref[...])


def _moe_combine(pos, gate, yout, xa, modtab, g_post, ctx_len):
    b, ta, d = xa.shape
    ne, _, nslot, _ = yout.shape
    tt = ta // 8
    tok = lambda w: pl.BlockSpec((None, tt, w), lambda i, j, e: (i, j, 0))
    return pl.pallas_call(
        functools.partial(_combine_kernel, ctx_len=ctx_len),
        out_shape=jax.ShapeDtypeStruct((b, ta, d), F32),
        grid=(b, ta // tt, ne),
        in_specs=[tok(ne), tok(ne),
                  pl.BlockSpec((None, None, nslot, d), lambda i, j, e: (e, i, 0, 0)),
                  tok(d),
                  pl.BlockSpec((None, 2, 6, d), lambda i, j, e: (i, 0, 0, 0)),
                  pl.BlockSpec((1, d), lambda i, j, e: (0, 0))],
        out_specs=tok(d),
        scratch_shapes=[pltpu.VMEM((tt, d), F32)],
        compiler_params=_cparams("parallel", "parallel", "arbitrary"),
        name="moe_combine",
    )(pos, gate, yout, xa, modtab, g_post.reshape(1, d))


def _ec_moe(logits_t, h, xa, modtab, g_post, w_gate, w_up, w_down, layer, ctx_len):
    b, ne, ta = logits_t.shape
    d = h.shape[-1]
    cap_ctx = max(1, EC_CAPACITY * ctx_len // ne)
    cap_lat = max(1, EC_CAPACITY * (ta - ctx_len) // ne)
    nslot = cap_ctx + cap_lat
    pos_t, gate_t = _moe_select(logits_t, ctx_len, cap_ctx, cap_lat)
    xin = _moe_gather(pos_t, h, nslot)
    pair = 2 if b % 2 == 0 else 1
    yout = _moe_ffn(xin.reshape(ne, b // pair, pair * nslot, d), w_gate, w_up, w_down, layer)
    yout = yout.reshape(ne, b, nslot, d)
    return _moe_combine(jnp.swapaxes(pos_t, 1, 2), jnp.swapaxes(gate_t, 1, 2), yout, xa, modtab, g_post, ctx_len)


def kernel(x, c, ctx, c_ctx, w_mod, b_mod, g_mix_pre, g_mix_post, g_ffn_pre, g_ffn_post, w_router, w_exp_gate, w_exp_up, w_exp_down, ev_w_in, ev_w_out, hgrn_lb, hgrn_g_norm, na_rpb, od_w_in, od_w_out, q_norm, k_norm, s5_a_re, s5_a_im, s5_log_dt, s5_b_re, s5_b_im, s5_c_re, s5_c_im, s5_d, s5_w_glu, s5_b_glu):
    b, t, d = x.shape
    ctx_len = ctx.shape[1]
    depth = w_mod.shape[0]
    assert depth == 2 and b <= 7
    ta = ctx_len + t
    a_width = d // 2
    s5_width = d // 4
    cq_width = d - s5_width
    ckv_width = cq_width // 3

    xa = jnp.concatenate([ctx, x], axis=1)
    cc = jnp.concatenate([c, c_ctx[None], jnp.zeros((7 - b, d), F32)], axis=0)
    mod = _modulation(cc, w_mod, b_mod)
    mod_lat = mod[:, :b].reshape(depth, b, 1, 6, d)
    mod_ctx = jnp.broadcast_to(mod[:, b].reshape(depth, 1, 1, 6, d), (depth, b, 1, 6, d))
    modtab = jnp.concatenate([mod_ctx, mod_lat], axis=2)

    lb_all = jnp.cumsum(jax.nn.softmax(hgrn_lb.astype(F32), axis=0), axis=0)
    hx = _prenorm(xa, g_mix_pre[0], modtab[0])
    p = _matmul(hx.reshape(b * ta, d), ev_w_in[0]).reshape(b, ta, -1)
    o_f, o_b = _hgrn(p, lb_all[0], ctx_len, a_width)
    mix_a = _hgrn_readout(o_f, o_b, p, hgrn_g_norm[0], a_width)
    mix_b = _natten(p, _natten_bias(na_rpb[0]), ctx_len, 5 * a_width, d - a_width)
    xa, h2, logits = _postmix(mix_a, mix_b, ev_w_out[0], xa, modtab[0], g_mix_post[0], g_ffn_pre[0], w_router[0])
    xa = _ec_moe(jnp.swapaxes(logits, 1, 2), h2, xa, modtab[0], g_ffn_post[0], w_exp_gate, w_exp_up, w_exp_down, 0, ctx_len)

    hx = _prenorm(xa, g_mix_pre[1], modtab[1])
    p = _matmul(hx.reshape(b * ta, d), od_w_in[0]).reshape(b, ta, -1)
    cosf, sinf = _rope_tables(ctx_len, t)
    qn, kn, vb = _qkprep(p, cosf, sinf, q_norm[0], k_norm[0], cq_width, ckv_width)
    mix_a = _gqa_attention(qn, kn, vb, cq_width // ckv_width)
    ops = _s5_operators(s5_a_re[0], s5_a_im[0], s5_log_dt[0], s5_b_re[0], s5_b_im[0], s5_c_re[0], s5_c_im[0])
    mix_b = _s5(p, cq_width + 2 * ckv_width, s5_width, ctx_len, ops, s5_d[0], s5_w_glu[0], s5_b_glu[0])
    xa, h2, logits = _postmix(mix_a, mix_b, od_w_out[0], xa, modtab[1], g_mix_post[1], g_ffn_pre[1], w_router[1])
    xa = _ec_moe(jnp.swapaxes(logits, 1, 2), h2, xa, modtab[1], g_ffn_post[1], w_exp_gate, w_exp_up, w_exp_down, 1, ctx_len)
    return xa[:, ctx_len:]
```

```python
import functools
import math

import jax
import jax.numpy as jnp
from jax import lax
from jax.experimental import pallas as pl
from jax.experimental.pallas import tpu as pltpu

F32 = jnp.float32
BF16 = jnp.bfloat16
HIGHEST = lax.Precision.HIGHEST
EPS = 1e-6

LANE = 128
TOK_TILE = 256
VMEM_LIMIT = 52 << 20

GRID_W = 64
NA_ROWS = 8
NA_COLS = 16
NA_GROUP = 4
NA_SPAN = 12
HEAD = 128
A_CHUNK = 64
N_EXPERTS = 16
EC_CAPACITY = 2
S5_GROUP = 16
S5_STATE = 64
S5_CHUNK = 16
COMBINE_WIN = 128
ROPE_THETA = 10000.0
NEG_BIG = -1e30
LOG2E = 1.4426950408889634


def _cparams(*sem):
    return pltpu.CompilerParams(dimension_semantics=sem, vmem_limit_bytes=VMEM_LIMIT)


def _dot(a, b):
    return jnp.dot(a, b, preferred_element_type=F32)


def _dot_nt(a, b):
    return lax.dot_general(a, b, (((1,), (1,)), ((), ())), preferred_element_type=F32)


def _dot_tn(a, b):
    return lax.dot_general(a, b, (((0,), (0,)), ((), ())), preferred_element_type=F32)


def _rms(x):
    return x * lax.rsqrt(jnp.mean(x * x, axis=-1, keepdims=True) + EPS)


def _silu(x):
    return x * jax.nn.sigmoid(x)


def _mod_kernel(c_ref, w_ref, b_ref, o_ref):
    o_ref[...] = jnp.dot(_silu(c_ref[...]), w_ref[...], preferred_element_type=F32, precision=HIGHEST) + b_ref[...]


def _modulation(cc, w_mod, b_mod):
    depth, d, n = w_mod.shape
    tn = 1024
    return pl.pallas_call(
        _mod_kernel,
        out_shape=jax.ShapeDtypeStruct((depth, cc.shape[0], n), F32),
        grid=(depth, n // tn),
        in_specs=[pl.BlockSpec(cc.shape, lambda l, j: (0, 0)),
                  pl.BlockSpec((None, d, tn), lambda l, j: (l, 0, j)),
                  pl.BlockSpec((None, 1, tn), lambda l, j: (l, 0, j))],
        out_specs=pl.BlockSpec((None, cc.shape[0], tn), lambda l, j: (l, 0, j)),
        compiler_params=_cparams("arbitrary", "arbitrary"),
        name="modulation",
    )(cc, w_mod, b_mod.reshape(depth, 1, n))


def _mod_spec(d):
    return pl.BlockSpec((None, None, 6, d), lambda b, j: (b, jnp.minimum(j, 1), 0, 0))


def _prenorm_kernel(x_ref, g_ref, m_ref, o_ref):
    m = m_ref[...]
    y = _rms(x_ref[...]) * g_ref[...]
    o_ref[...] = (y * (1.0 + m[1:2]) + m[0:1]).astype(BF16)


def _prenorm(xa, g, modtab):
    b, ta, d = xa.shape
    return pl.pallas_call(
        _prenorm_kernel,
        out_shape=jax.ShapeDtypeStruct((b, ta, d), BF16),
        grid=(b, ta // TOK_TILE),
        in_specs=[pl.BlockSpec((None, TOK_TILE, d), lambda i, j: (i, j, 0)),
                  pl.BlockSpec((1, d), lambda i, j: (0, 0)),
                  _mod_spec(d)],
        out_specs=pl.BlockSpec((None, TOK_TILE, d), lambda i, j: (i, j, 0)),
        compiler_params=_cparams("parallel", "parallel"),
        name="prenorm",
    )(xa, g.reshape(1, d), modtab)


def _mm_kernel(a_ref, w_ref, o_ref, wb_ref):
    @pl.when(pl.program_id(1) == 0)
    def _():
        wb_ref[...] = w_ref[...].astype(BF16)

    o_ref[...] = _dot(a_ref[...], wb_ref[...]).astype(o_ref.dtype)


def _matmul(a, w, tm=512, tn=1024, out_dtype=F32):
    m, k = a.shape
    n = w.shape[1]
    return pl.pallas_call(
        _mm_kernel,
        out_shape=jax.ShapeDtypeStruct((m, n), out_dtype),
        grid=(n // tn, m // tm),
        in_specs=[pl.BlockSpec((tm, k), lambda j, i: (i, 0)),
                  pl.BlockSpec((k, tn), lambda j, i: (0, j))],
        out_specs=pl.BlockSpec((tm, tn), lambda j, i: (i, j)),
        scratch_shapes=[pltpu.VMEM((k, tn), BF16)],
        compiler_params=_cparams("arbitrary", "arbitrary"),
        name="proj_in",
    )(a, w)


def _hgrn_masks(chunk):
    row = lax.broadcasted_iota(jnp.int32, (chunk, LANE), 0)
    ti = lax.broadcasted_iota(jnp.int32, (chunk, chunk), 0)
    si = lax.broadcasted_iota(jnp.int32, (chunk, chunk), 1)
    levels = [(((row >> lvl) & 1) == 1, (ti >> (lvl + 1)) == (si >> (lvl + 1))) for lvl in range(chunk.bit_length() - 1)]
    return ti == si, levels


def _hgrn_chunk(q, fr, v, lb, st, rev, masks):
    f = lb + (1.0 - lb) * jax.nn.sigmoid(fr)
    kk = 1.0 - f
    lf = jnp.log(f)
    chunk = q.shape[0]
    diag, levels = masks
    att = jnp.where(diag, _dot_nt(q.astype(BF16), kk.astype(BF16)), 0.0)
    p_in = lf
    r_ex = jnp.zeros_like(lf)
    tot = lf
    for lvl, (bit, same) in enumerate(levels):
        step = 1 << lvl
        up = pltpu.roll(tot, step, 0)
        dn = pltpu.roll(tot, chunk - step, 0)
        is_q = jnp.logical_not(bit) if rev else bit
        qf = jnp.where(is_q, jnp.exp(p_in) * q, 0.0).astype(BF16)
        kf = jnp.where(is_q, 0.0, jnp.exp(r_ex) * kk).astype(BF16)
        att = att + jnp.where(same, _dot_nt(qf, kf), 0.0)
        if rev:
            p_in = p_in + jnp.where(bit, 0.0, dn)
            r_ex = r_ex + jnp.where(bit, up, 0.0)
        else:
            p_in = p_in + jnp.where(bit, up, 0.0)
            r_ex = r_ex + jnp.where(bit, 0.0, dn)
        tot = tot + jnp.where(bit, up, dn)
    vb = v.astype(BF16)
    o = _dot_nt((q * jnp.exp(p_in)).astype(BF16), st.astype(BF16)) + _dot(att.astype(BF16), vb)
    kd = (kk * jnp.exp(r_ex)).astype(BF16)
    st_new = st * jnp.exp(tot[0:1]) + _dot_tn(vb, kd)
    return o, st_new


def _hgrn_kernel(qf_ref, ff_ref, vf_ref, qb_ref, fb_ref, vb_ref, lb_ref, of_ref, ob_ref, st_ref, *, chunk, hb):
    @pl.when(pl.program_id(2) == 0)
    def _():
        st_ref[...] = jnp.zeros_like(st_ref)

    masks = _hgrn_masks(chunk)
    for h in range(hb):
        sl = slice(h * HEAD, (h + 1) * HEAD)
        o, st = _hgrn_chunk(qf_ref[:, sl], ff_ref[:, sl], vf_ref[:, sl], lb_ref[0:1, sl], st_ref[0, h], False, masks)
        of_ref[:, sl] = o
        st_ref[0, h] = st
        o, st = _hgrn_chunk(qb_ref[:, sl], fb_ref[:, sl], vb_ref[:, sl], lb_ref[1:2, sl], st_ref[1, h], True, masks)
        ob_ref[:, sl] = o
        st_ref[1, h] = st


def _hgrn(p, lb, ctx_len, width, hb=2):
    b, ta, _ = p.shape
    chunk = A_CHUNK
    n = ta // chunk
    nc = ctx_len // chunk
    cw = HEAD * hb
    nh = width // cw
    bwd = lambda j: jnp.where(j < nc, nc - 1 - j, n - 1 + nc - j)

    def spec(base, rev):
        return pl.BlockSpec((None, chunk, cw), lambda i, h, j: (i, bwd(j) if rev else j, base // cw + h))

    out = jax.ShapeDtypeStruct((b, ta, width), F32)
    return pl.pallas_call(
        functools.partial(_hgrn_kernel, chunk=chunk, hb=hb),
        out_shape=(out, out),
        grid=(b, nh, n),
        in_specs=[spec(0, False), spec(width, False), spec(3 * width, False),
                  spec(0, True), spec(2 * width, True), spec(3 * width, True),
                  pl.BlockSpec((2, cw), lambda i, h, j: (0, h))],
        out_specs=(spec(0, False), spec(0, True)),
        scratch_shapes=[pltpu.VMEM((2, hb, HEAD, HEAD), F32)],
        compiler_params=_cparams("parallel", "parallel", "arbitrary"),
        name="hgrn_scan",
    )(p, p, p, p, p, p, lb)


def _hgrn_readout_kernel(of_ref, ob_ref, g_ref, gn_ref, o_ref):
    for h in range(o_ref.shape[-1] // HEAD):
        sl = slice(h * HEAD, (h + 1) * HEAD)
        o = of_ref[:, sl] + ob_ref[:, sl]
        o_ref[:, sl] = (_rms(o) * gn_ref[...] * _silu(g_ref[:, sl])).astype(BF16)


def _hgrn_readout(o_f, o_b, p, g_norm, width):
    b, ta, _ = o_f.shape
    spec = pl.BlockSpec((None, TOK_TILE, width), lambda i, j: (i, j, 0))
    return pl.pallas_call(
        _hgrn_readout_kernel,
        out_shape=jax.ShapeDtypeStruct((b, ta, width), BF16),
        grid=(b, ta // TOK_TILE),
        in_specs=[spec, spec,
                  pl.BlockSpec((None, TOK_TILE, width), lambda i, j: (i, j, 4)),
                  pl.BlockSpec((1, HEAD), lambda i, j: (0, 0))],
        out_specs=spec,
        compiler_params=_cparams("parallel", "parallel"),
        name="hgrn_readout",
    )(o_f, o_b, p, g_norm.reshape(1, HEAD))


def _natten_bias(rpb):
    nrow, ncol = 2 * NA_ROWS - 1, 2 * NA_COLS - 1
    col = jnp.arange(GRID_W)
    col_start = jnp.clip(col - NA_COLS // 2, 0, GRID_W - NA_COLS)
    cmask = (col[None, :] >= col_start[:, None]) & (col[None, :] < col_start[:, None] + NA_COLS)
    dc = col[None, :] - col[:, None] + NA_COLS - 1
    cm = (cmask[:, :, None] & (dc[:, :, None] == jnp.arange(ncol))).astype(F32)
    cls = jnp.arange(3)[:, None, None]
    i = jnp.arange(NA_GROUP)[None, :, None]
    w = jnp.arange(NA_SPAN)[None, None, :]
    first_w = jnp.where(cls == 0, 0, jnp.where(cls == 1, i, NA_SPAN - NA_ROWS))
    valid = (w >= first_w) & (w < first_w + NA_ROWS)
    span0 = jnp.where(cls == 0, 0, jnp.where(cls == 1, -(NA_ROWS // 2), NA_GROUP - NA_SPAN))
    dr = span0 + w - i + NA_ROWS - 1
    rm = (valid[..., None] & (dr[..., None] == jnp.arange(nrow))).astype(F32)
    t = jnp.einsum('qkb,hab->haqk', cm, rpb.astype(F32), precision=HIGHEST)
    t = jnp.einsum('ciwa,haqk->hciqwk', rm, t, precision=HIGHEST)
    ok = valid[None, :, :, None, :, None] & cmask[None, None, None, :, None, :]
    t = jnp.where(ok, t, NEG_BIG)
    return t.reshape(rpb.shape[0], 3, NA_GROUP * GRID_W, NA_SPAN * GRID_W)


def _natten_kernel(q_ref, k_ref, v_ref, bias_ref, o_ref, *, ctx_len, rows):
    j = pl.program_id(2)
    nst = rows // NA_GROUP
    scale = HEAD ** -0.5
    qb = q_ref[...].astype(BF16)
    kc = k_ref[0:ctx_len, :].astype(BF16)
    vc = v_ref[0:ctx_len, :].astype(BF16)
    s_ctx = _dot_nt(qb, kc) * scale
    m_ctx = jnp.max(s_ctx, axis=1, keepdims=True)

    @pl.when(j < nst)
    def _():
        last = j == nst - 1
        cls = jnp.where(j == 0, 0, jnp.where(last, 2, 1))
        span_row = jnp.where(j == 0, 0, jnp.where(last, rows - NA_SPAN, NA_GROUP * j - NA_ROWS // 2))
        start = pl.multiple_of(ctx_len + span_row * GRID_W, GRID_W)
        kw = k_ref[pl.ds(start, NA_SPAN * GRID_W), :].astype(BF16)
        vw = v_ref[pl.ds(start, NA_SPAN * GRID_W), :].astype(BF16)
        s_win = _dot_nt(qb, kw) * scale + bias_ref[cls]
        m = jnp.maximum(m_ctx, jnp.max(s_win, axis=1, keepdims=True))
        pw = jnp.exp(s_win - m)
        pc = jnp.exp(s_ctx - m)
        den = jnp.sum(pw, axis=1, keepdims=True) + jnp.sum(pc, axis=1, keepdims=True)
        o = _dot(pw.astype(BF16), vw) + _dot(pc.astype(BF16), vc)
        o_ref[...] = (o / den).astype(BF16)

    @pl.when(j >= nst)
    def _():
        pc = jnp.exp(s_ctx - m_ctx)
        o = _dot(pc.astype(BF16), vc) / jnp.sum(pc, axis=1, keepdims=True)
        o_ref[...] = o.astype(BF16)


def _natten(p, bias, ctx_len, col0, width):
    b, ta, _ = p.shape
    nh = width // HEAD
    rows = (ta - ctx_len) // GRID_W
    tq = NA_GROUP * GRID_W
    nst = rows // NA_GROUP
    ncq = ctx_len // tq
    assert rows % NA_GROUP == 0 and rows >= NA_SPAN and ctx_len % tq == 0

    def qmap(cb):
        return lambda i, h, j: (i, jnp.where(j < nst, ncq + j, j - nst), cb + h)

    return pl.pallas_call(
        functools.partial(_natten_kernel, ctx_len=ctx_len, rows=rows),
        out_shape=jax.ShapeDtypeStruct((b, ta, width), BF16),
        grid=(b, nh, nst + ncq),
        in_specs=[pl.BlockSpec((None, tq, HEAD), qmap(col0 // HEAD)),
                  pl.BlockSpec((None, ta, HEAD), lambda i, h, j: (i, 0, (col0 + width) // HEAD + h)),
                  pl.BlockSpec((None, ta, HEAD), lambda i, h, j: (i, 0, (col0 + 2 * width) // HEAD + h)),
                  pl.BlockSpec((None, 3, tq, NA_SPAN * GRID_W), lambda i, h, j: (h, 0, 0, 0))],
        out_specs=pl.BlockSpec((None, tq, HEAD), qmap(0)),
        compiler_params=_cparams("parallel", "parallel", "arbitrary"),
        name="natten",
    )(p, p, p, bias)


def _rope_tables(ctx_len, t):
    pos = jnp.arange(t)
    row = (pos // GRID_W).astype(F32)
    col = (pos % GRID_W).astype(F32)
    half = HEAD // 2
    inv = ROPE_THETA ** (-jnp.arange(0, half, 2, dtype=F32) / half)
    ang = jnp.concatenate([row[:, None] * inv, col[:, None] * inv], axis=-1)
    cos, sin = jnp.cos(ang), jnp.sin(ang)
    cosf = jnp.repeat(cos, 2, axis=-1)
    sinf = jnp.stack([-sin, sin], axis=-1).reshape(t, HEAD)
    cosf = jnp.concatenate([jnp.ones((ctx_len, HEAD), F32), cosf], axis=0)
    sinf = jnp.concatenate([jnp.zeros((ctx_len, HEAD), F32), sinf], axis=0)
    return cosf, sinf


def _qkprep_kernel(q_ref, k_ref, v_ref, cos_ref, sin_ref, qn_ref, kn_ref, qo_ref, ko_ref, vo_ref):
    cosf = cos_ref[...]
    sinf = sin_ref[...]
    even = (lax.broadcasted_iota(jnp.int32, cosf.shape, 1) & 1) == 0

    def rope(x):
        partner = jnp.where(even, pltpu.roll(x, LANE - 1, 1), pltpu.roll(x, 1, 1))
        return x * cosf + partner * sinf

    for h in range(q_ref.shape[-1] // HEAD):
        sl = slice(h * HEAD, (h + 1) * HEAD)
        qo_ref[:, sl] = (rope(_rms(q_ref[:, sl]) * qn_ref[...]) * (HEAD ** -0.5 * LOG2E)).astype(BF16)
    for h in range(k_ref.shape[-1] // HEAD):
        sl = slice(h * HEAD, (h + 1) * HEAD)
        ko_ref[:, sl] = rope(_rms(k_ref[:, sl]) * kn_ref[...]).astype(BF16)
    vo_ref[...] = v_ref[...].astype(BF16)


def _qkprep(p, cosf, sinf, q_norm, k_norm, qw, kw):
    b, ta, _ = p.shape
    tt = TOK_TILE

    def spec(w, cb):
        return pl.BlockSpec((None, tt, w), lambda i, j: (i, j, cb))

    return pl.pallas_call(
        _qkprep_kernel,
        out_shape=(jax.ShapeDtypeStruct((b, ta, qw), BF16),
                   jax.ShapeDtypeStruct((b, ta, kw), BF16),
                   jax.ShapeDtypeStruct((b, ta, kw), BF16)),
        grid=(b, ta // tt),
        in_specs=[spec(qw, 0), spec(kw, qw // kw), spec(kw, qw // kw + 1),
                  pl.BlockSpec((tt, HEAD), lambda i, j: (j, 0)),
                  pl.BlockSpec((tt, HEAD), lambda i, j: (j, 0)),
                  pl.BlockSpec((1, HEAD), lambda i, j: (0, 0)),
                  pl.BlockSpec((1, HEAD), lambda i, j: (0, 0))],
        out_specs=(spec(qw, 0), spec(kw, 0), spec(kw, 0)),
        compiler_params=_cparams("parallel", "parallel"),
        name="qk_prep",
    )(p, p, p, cosf, sinf, q_norm.reshape(1, HEAD), k_norm.reshape(1, HEAD))


def _attn_kernel(q_ref, k_ref, v_ref, o_ref):
    s = _dot_nt(q_ref[...], k_ref[...])
    p = jnp.exp2(s - jnp.max(s, axis=1, keepdims=True))
    o = _dot(p.astype(BF16), v_ref[...]) / jnp.sum(p, axis=1, keepdims=True)
    o_ref[...] = o.astype(BF16)


def _gqa_attention(qn, kn, vb, group):
    b, ta, qw = qn.shape
    nkv = kn.shape[-1] // HEAD
    tq = TOK_TILE
    return pl.pallas_call(
        _attn_kernel,
        out_shape=jax.ShapeDtypeStruct((b, ta, qw), BF16),
        grid=(b, nkv, group, ta // tq),
        in_specs=[pl.BlockSpec((None, tq, HEAD), lambda i, h, g, j: (i, j, h * group + g)),
                  pl.BlockSpec((None, ta, HEAD), lambda i, h, g, j: (i, 0, h)),
                  pl.BlockSpec((None, ta, HEAD), lambda i, h, g, j: (i, 0, h))],
        out_specs=pl.BlockSpec((None, tq, HEAD), lambda i, h, g, j: (i, j, h * group + g)),
        compiler_params=_cparams("parallel", "parallel", "parallel", "parallel"),
        name="gqa_attention",
    )(qn, kn, vb)


def _s5_operators(a_re, a_im, log_dt, b_re, b_im, c_re, c_im):
    lc, pp, ns = S5_CHUNK, S5_GROUP, S5_STATE
    a_c = lax.complex(a_re.astype(F32), a_im.astype(F32))
    adt = a_c * jnp.exp(log_dt.astype(F32))[..., None]
    a_bar = jnp.exp(adt)
    b_bar = ((a_bar - 1.0) / a_c)[..., None] * lax.complex(b_re.astype(F32), b_im.astype(F32))
    c_mat = lax.complex(c_re.astype(F32), c_im.astype(F32))
    nd, g = a_re.shape[0], a_re.shape[1]
    pos = jnp.arange(lc)
    lag = pos[None, :] - pos[:, None]
    lag = jnp.stack([lag, -lag])
    live = (lag >= 0)[:, None, :, :, None]
    apl = jnp.where(live, jnp.exp(adt[:, :, None, None, :] * jnp.maximum(lag, 0).astype(F32)[:, None, :, :, None]), 0.0)
    tm = jnp.real(jnp.einsum('dgpn,dgion,dgnq->dgiqop', c_mat, apl, b_bar, precision=HIGHEST))
    tmat = tm.reshape(nd, g, lc * pp, lc * pp)
    steps_after = jnp.stack([lc - 1 - pos, pos]).astype(F32)
    gm = jnp.exp(adt[:, :, None, :] * steps_after[:, None, :, None])[:, :, :, None, :] * jnp.swapaxes(b_bar, 2, 3)[:, :, None]
    gm = gm.reshape(nd, g, lc * pp, ns)
    steps_upto = jnp.stack([pos + 1, lc - pos]).astype(F32)
    hm = c_mat[:, :, None] * jnp.exp(adt[:, :, None, :] * steps_upto[:, None, :, None])[:, :, :, None, :]
    hm = jnp.transpose(hm, (0, 1, 4, 2, 3)).reshape(nd, g, ns, lc * pp)
    hmat = jnp.concatenate([jnp.real(hm), -jnp.imag(hm)], axis=2)
    al = jnp.exp(adt * float(lc))
    apow = jnp.concatenate([jnp.real(al), jnp.imag(al)], axis=-1)[:, :, None, :]
    gd = lambda x: jnp.swapaxes(x, 0, 1)
    return (gd(tmat).astype(BF16), gd(jnp.real(gm)).astype(BF16), gd(jnp.imag(gm)).astype(BF16),
            gd(hmat).astype(BF16), gd(apow))


def _s5_kernel(u_ref, t_ref, gr_ref, gi_ref, h_ref, a_ref, d_ref, y_ref, er_ref, ei_ref, xr_ref, xi_ref,
               *, nchunk, nctx, rpc):
    ns = S5_STATE
    u = u_ref[...]
    for dr in range(2):
        er_ref[dr] = _dot(u, gr_ref[dr])
        ei_ref[dr] = _dot(u, gi_ref[dr])

    coef = [(jnp.broadcast_to(a_ref[dr, :, 0:ns], (rpc, ns)), jnp.broadcast_to(a_ref[dr, :, ns:2 * ns], (rpc, ns)))
            for dr in range(2)]

    def body(c, carry):
        out = []
        for dr in range(2):
            xr, xi = carry[2 * dr], carry[2 * dr + 1]
            pc = c if dr == 0 else jnp.where(c < nctx, nctx - 1 - c, nchunk - 1 + nctx - c)
            off = pl.multiple_of(pc * rpc, rpc)
            xr_ref[dr, pl.ds(off, rpc), :] = xr
            xi_ref[dr, pl.ds(off, rpc), :] = xi
            ar, ai = coef[dr]
            out.append(ar * xr - ai * xi + er_ref[dr, pl.ds(off, rpc), :])
            out.append(ar * xi + ai * xr + ei_ref[dr, pl.ds(off, rpc), :])
        return tuple(out)

    zero = jnp.zeros((rpc, ns), F32)
    lax.fori_loop(0, nchunk, body, (zero, zero, zero, zero), unroll=4)
    y = d_ref[...] * u.astype(F32)
    for dr in range(2):
        y = y + _dot(u, t_ref[dr])
        y = y + _dot(xr_ref[dr].astype(BF16), h_ref[dr, 0:ns, :]) + _dot(xi_ref[dr].astype(BF16), h_ref[dr, ns:2 * ns, :])
    y_ref[...] = y.astype(BF16)


def _s5_scan(u, ops, d_skip, ctx_len):
    tmat, g_re, g_im, hmat, apow = ops
    b, ta, w = u.shape
    lc, pp, ns = S5_CHUNK, S5_GROUP, S5_STATE
    g = w // pp
    nchunk = ta // lc
    rpc = 8
    cw = lc * pp
    rows = nchunk * rpc
    ug = jnp.transpose(u.astype(BF16).reshape(b, nchunk, lc, g, pp), (3, 1, 0, 2, 4))
    ug = jnp.pad(ug, ((0, 0), (0, 0), (0, rpc - b), (0, 0), (0, 0))).reshape(g, rows, cw)
    dvec = jnp.tile(d_skip.astype(F32).reshape(g, 1, pp), (1, lc, 1)).reshape(g, 1, cw)
    op = lambda r, c: pl.BlockSpec((None, 2, r, c), lambda i: (i, 0, 0, 0))
    y = pl.pallas_call(
        functools.partial(_s5_kernel, nchunk=nchunk, nctx=ctx_len // lc, rpc=rpc),
        out_shape=jax.ShapeDtypeStruct((g, rows, cw), BF16),
        grid=(g,),
        in_specs=[pl.BlockSpec((None, rows, cw), lambda i: (i, 0, 0)),
                  op(cw, cw), op(cw, ns), op(cw, ns), op(2 * ns, cw), op(1, 2 * ns),
                  pl.BlockSpec((None, 1, cw), lambda i: (i, 0, 0))],
        out_specs=pl.BlockSpec((None, rows, cw), lambda i: (i, 0, 0)),
        scratch_shapes=[pltpu.VMEM((2, rows, ns), F32) for _ in range(4)],
        compiler_params=_cparams("parallel"),
        name="s5_scan",
    )(ug, tmat, g_re, g_im, hmat, apow, dvec)
    y = y.reshape(g, nchunk, rpc, lc, pp)[:, :, :b]
    return jnp.transpose(y, (2, 1, 3, 0, 4)).reshape(b, ta, w)


def _s5_glu_kernel(y_ref, w_ref, b_ref, o_ref):
    y = y_ref[...].astype(F32)
    y = 0.5 * y * (1.0 + jnp.tanh(math.sqrt(2.0 / math.pi) * (y + 0.044715 * (y * y * y))))
    z = _dot(y.astype(BF16), w_ref[...]) + b_ref[...]
    o_ref[...] = (y * jax.nn.sigmoid(z)).astype(BF16)


def _s5_glu(y, w_glu, b_glu):
    b, ta, w = y.shape
    spec = pl.BlockSpec((None, TOK_TILE, w), lambda i, j: (i, j, 0))
    return pl.pallas_call(
        _s5_glu_kernel,
        out_shape=jax.ShapeDtypeStruct((b, ta, w), BF16),
        grid=(b, ta // TOK_TILE),
        in_specs=[spec, pl.BlockSpec((w, w), lambda i, j: (0, 0)), pl.BlockSpec((1, w), lambda i, j: (0, 0))],
        out_specs=spec,
        compiler_params=_cparams("parallel", "parallel"),
        name="s5_glu",
    )(y, w_glu.astype(BF16), b_glu.reshape(1, w))


def _s5(p, ucol, width, ctx_len, ops, d_skip, w_glu, b_glu):
    y = _s5_scan(p[..., ucol:ucol + width], ops, d_skip, ctx_len)
    return _s5_glu(y, w_glu, b_glu)


def _postmix_kernel(a_ref, b_ref, w_ref, x_ref, m_ref, gpost_ref, gpre_ref, wr_ref, xo_ref, h_ref, lg_ref):
    wa = a_ref.shape[-1]
    y = _dot(a_ref[...], w_ref[0:wa, :]) + _dot(b_ref[...], w_ref[wa:, :])
    m = m_ref[...]
    xn = x_ref[...] + m[2:3] * (_rms(y) * gpost_ref[...])
    xo_ref[...] = xn
    h2 = _rms(xn) * gpre_ref[...] * (1.0 + m[4:5]) + m[3:4]
    hi = h2.astype(BF16)
    h_ref[...] = hi
    lo = (h2 - hi.astype(F32)).astype(BF16)
    ne = lg_ref.shape[-1]
    both = _dot(hi, wr_ref[...])
    lg_ref[...] = both[:, 0:ne] + both[:, ne:2 * ne] + _dot(lo, wr_ref[:, 0:ne])


def _postmix(mix_a, mix_b, w_out, xa, modtab, g_post, g_pre, w_router):
    b, ta, d = xa.shape
    wa, wb = mix_a.shape[-1], mix_b.shape[-1]
    ne = w_router.shape[-1]
    tok = lambda w: pl.BlockSpec((None, TOK_TILE, w), lambda i, j: (i, j, 0))
    vec = pl.BlockSpec((1, d), lambda i, j: (0, 0))
    wr_hi = w_router.astype(BF16)
    wr2 = jnp.concatenate([wr_hi, (w_router - wr_hi.astype(F32)).astype(BF16)], axis=1)
    return pl.pallas_call(
        _postmix_kernel,
        out_shape=(jax.ShapeDtypeStruct((b, ta, d), F32),
                   jax.ShapeDtypeStruct((b, ta, d), BF16),
                   jax.ShapeDtypeStruct((b, ta, ne), F32)),
        grid=(b, ta // TOK_TILE),
        in_specs=[tok(wa), tok(wb),
                  pl.BlockSpec((wa + wb, d), lambda i, j: (0, 0)),
                  tok(d), _mod_spec(d), vec, vec,
                  pl.BlockSpec((d, 2 * ne), lambda i, j: (0, 0))],
        out_specs=(tok(d), tok(d), tok(ne)),
        compiler_params=_cparams("parallel", "parallel"),
        name="mix_out",
    )(mix_a, mix_b, w_out.astype(BF16), xa, modtab, g_post.reshape(1, d), g_pre.reshape(1, d), wr2)


def _select_kernel(lg_ref, pos_ref, gate_ref, *, ctx_len, cap_ctx, cap_lat):
    lg = lg_ref[...]
    ne, ta = lg.shape
    ex = jnp.exp(lg - jnp.max(lg, axis=0, keepdims=True))
    probs = ex / jnp.sum(ex, axis=0, keepdims=True)
    bits = pltpu.bitcast(probs, jnp.int32)
    is_ctx = lax.broadcasted_iota(jnp.int32, (ne, ta), 1) < ctx_len

    def counts(mask):
        mf = jnp.where(mask, 1.0, 0.0)
        return (jnp.sum(jnp.where(is_ctx, mf, 0.0), axis=1, keepdims=True),
                jnp.sum(jnp.where(is_ctx, 0.0, mf), axis=1, keepdims=True))

    def search(i, carry):
        pc, pt = carry
        bit = jnp.left_shift(jnp.int32(1), 30 - i)
        cc, ct = counts(bits >= jnp.where(is_ctx, pc | bit, pt | bit))
        return jnp.where(cc >= cap_ctx, pc | bit, pc), jnp.where(ct >= cap_lat, pt | bit, pt)

    z = jnp.zeros((ne, 1), jnp.int32)
    pc, pt = lax.fori_loop(0, 31, search, (z, z))
    thr = jnp.where(is_ctx, pc, pt)
    gt = bits > thr
    eq = bits == thr
    gc, gl = counts(gt)
    need = jnp.where(is_ctx, cap_ctx - gc, cap_lat - gl)

    nb = ta // LANE
    ut = jnp.where(lax.broadcasted_iota(jnp.int32, (LANE, LANE), 0) <= lax.broadcasted_iota(jnp.int32, (LANE, LANE), 1),
                   1.0, 0.0).astype(BF16)

    def lane_prefix(mask):
        mf = jnp.where(mask, 1.0, 0.0).astype(BF16)
        blocks = jnp.concatenate([mf[:, j * LANE:(j + 1) * LANE] for j in range(nb)], axis=0)
        inc = _dot(blocks, ut)
        outs = []
        off = jnp.zeros((ne, 1), F32)
        for j in range(nb):
            if j * LANE == ctx_len:
                off = jnp.zeros((ne, 1), F32)
            blk = inc[j * ne:(j + 1) * ne]
            outs.append(blk + off)
            off = off + blk[:, LANE - 1:LANE]
        return jnp.concatenate(outs, axis=1)

    sel = gt | (eq & (lane_prefix(eq) <= need))
    slot = lane_prefix(sel) - 1.0 + jnp.where(is_ctx, 0.0, float(cap_ctx))
    pos_ref[...] = jnp.where(sel, slot, -1.0)
    gate_ref[...] = jnp.where(sel, probs, 0.0)


def _moe_select(logits_t, ctx_len, cap_ctx, cap_lat):
    b, ne, ta = logits_t.shape
    spec = pl.BlockSpec((None, ne, ta), lambda i: (i, 0, 0))
    return pl.pallas_call(
        functools.partial(_select_kernel, ctx_len=ctx_len, cap_ctx=cap_ctx, cap_lat=cap_lat),
        out_shape=(jax.ShapeDtypeStruct((b, ne, ta), F32), jax.ShapeDtypeStruct((b, ne, ta), F32)),
        grid=(b,),
        in_specs=[spec],
        out_specs=(spec, spec),
        compiler_params=_cparams("parallel"),
        name="moe_select",
    )(logits_t)


def _gather_kernel(pos_ref, h_ref, o_ref, acc_ref):
    k = pl.program_id(2)
    nslot = o_ref.shape[0]
    tk = h_ref.shape[0]
    slot = lax.broadcasted_iota(jnp.int32, (nslot, tk), 0)
    onehot = jnp.where(slot == pos_ref[...].astype(jnp.int32), 1.0, 0.0).astype(BF16)
    part = _dot(onehot, h_ref[...])

    @pl.when(k == 0)
    def _():
        acc_ref[...] = part

    @pl.when(k > 0)
    def _():
        acc_ref[...] += part

    @pl.when(k == pl.num_programs(2) - 1)
    def _():
        o_ref[...] = acc_ref[...].astype(BF16)


def _moe_gather(pos_t, h, nslot):
    b, ne, ta = pos_t.shape
    d = h.shape[-1]
    kt = 2
    tk = ta // kt
    return pl.pallas_call(
        _gather_kernel,
        out_shape=jax.ShapeDtypeStruct((ne, b, nslot, d), BF16),
        grid=(b, ne, kt),
        in_specs=[pl.BlockSpec((None, None, 1, tk), lambda i, e, k: (i, e, 0, k)),
                  pl.BlockSpec((None, tk, d), lambda i, e, k: (i, k, 0))],
        out_specs=pl.BlockSpec((None, None, nslot, d), lambda i, e, k: (e, i, 0, 0)),
        scratch_shapes=[pltpu.VMEM((nslot, d), F32)],
        compiler_params=_cparams("parallel", "parallel", "arbitrary"),
        name="moe_gather",
    )(pos_t.reshape(b, ne, 1, ta), h)


def _ffn_kernel(x_ref, wg_ref, wu_ref, wd_ref, o_ref, hid_ref, *, nf):
    s = pl.program_id(2)
    tf = wg_ref.shape[-1]

    @pl.when(s < nf)
    def _():
        x = x_ref[...]
        g = _dot(x, wg_ref[...].astype(BF16))
        u = _dot(x, wu_ref[...].astype(BF16))
        hid_ref[s] = (_silu(g) * u).astype(BF16)

    @pl.when(s >= nf)
    def _():
        acc = _dot(hid_ref[0], wd_ref[0:tf, :].astype(BF16))
        for f in range(1, nf):
            acc = acc + _dot(hid_ref[f], wd_ref[f * tf:(f + 1) * tf, :].astype(BF16))
        o_ref[...] = acc.astype(BF16)


def _moe_ffn(xin, w_gate, w_up, w_down, layer, tf=256):
    ne, ns, m, d = xin.shape
    ff = w_gate.shape[-1]
    nf = ff // tf
    nd = d // tf
    up = lambda e, s, f: (layer, e, 0, jnp.minimum(f, nf - 1))
    return pl.pallas_call(
        functools.partial(_ffn_kernel, nf=nf),
        out_shape=jax.ShapeDtypeStruct((ne, ns, m, d), BF16),
        grid=(ne, ns, nf + nd),
        in_specs=[pl.BlockSpec((None, None, m, d), lambda e, s, f: (e, s, 0, 0)),
                  pl.BlockSpec((None, None, d, tf), up),
                  pl.BlockSpec((None, None, d, tf), up),
                  pl.BlockSpec((None, None, ff, tf), lambda e, s, f: (layer, e, 0, jnp.maximum(f - nf, 0)))],
        out_specs=pl.BlockSpec((None, None, m, tf), lambda e, s, f: (e, s, 0, jnp.maximum(f - nf, 0))),
        scratch_shapes=[pltpu.VMEM((nf, m, tf), BF16)],
        compiler_params=_cparams("parallel", "parallel", "arbitrary"),
        name="moe_ffn",
    )(xin, w_gate, w_up, w_down)


def _combine_kernel(start_ref, npass_ref, pos_ref, gate_ref, y_hbm, x_ref, m_ref, g_ref, o_ref, ybuf, acc_ref, sem,
                    *, ctx_len):
    i = pl.program_id(0)
    j = pl.program_id(1)
    tt, ne = pos_ref.shape
    nslot = y_hbm.shape[2]
    win = COMBINE_WIN
    tile = i * pl.num_programs(1) + j
    pos = pos_ref[...].astype(jnp.int32)
    gate = gate_ref[...]
    lane = lax.broadcasted_iota(jnp.int32, (tt, win), 1)

    def window_copy(e, src):
        return pltpu.make_async_copy(y_hbm.at[e, i, pl.ds(src, win), :], ybuf.at[pl.ds(e * win, win), :], sem.at[e])

    def one_pass(p):
        own_lo, src = [], []
        for e in range(ne):
            lo = start_ref[tile * ne + e] + p * win
            own_lo.append(lo)
            src.append(pl.multiple_of(jnp.minimum(lo, nslot - win), 16))
            window_copy(e, src[e]).start()
        blocks = []
        for e in range(ne):
            pe = pos[:, e:e + 1]
            hit = (pe >= own_lo[e]) & (pe < own_lo[e] + win) & (lane == pe - src[e])
            blocks.append(jnp.where(hit, gate[:, e:e + 1], 0.0).astype(BF16))
        w = jnp.concatenate(blocks, axis=1)
        for e in range(ne):
            window_copy(e, src[e]).wait()
        return _dot(w, ybuf[...])

    acc_ref[...] = one_pass(0)

    def extra(p, carry):
        acc_ref[...] += one_pass(p)
        return carry

    lax.fori_loop(1, npass_ref[tile], extra, 0)
    m = m_ref[...]
    is_ctx = (lax.broadcasted_iota(jnp.int32, (tt, 1), 0) + j * tt) < ctx_len
    mod_gate = jnp.where(is_ctx, m[0, 5:6], m[1, 5:6])
    o_ref[...] = x_ref[...] + mod_gate * (_rms(acc_ref[...]) * g_ref[...])


def _moe_combine(pos_t, gate_t, yout, xa, modtab, g_post, ctx_len):
    b, ta, d = xa.shape
    ne, _, nslot, _ = yout.shape
    nt = 8
    tt = ta // nt
    win = COMBINE_WIN
    pt = pos_t.reshape(b, ne, nt, tt)
    hi = jnp.max(pt, axis=-1).astype(jnp.int32) + 1
    lo = jnp.min(jnp.where(pt >= 0, pt, float(nslot)), axis=-1).astype(jnp.int32)
    lo = jnp.where(hi > 0, lo // 16 * 16, 0)
    npass = jnp.maximum(jnp.max((hi - lo + win - 1) // win, axis=1), 1)
    start = jnp.transpose(lo, (0, 2, 1)).reshape(-1)
    tok = lambda w: pl.BlockSpec((None, tt, w), lambda i, j, *_: (i, j, 0))
    return pl.pallas_call(
        functools.partial(_combine_kernel, ctx_len=ctx_len),
        out_shape=jax.ShapeDtypeStruct((b, ta, d), F32),
        grid_spec=pltpu.PrefetchScalarGridSpec(
            num_scalar_prefetch=2,
            grid=(b, nt),
            in_specs=[tok(ne), tok(ne),
                      pl.BlockSpec(memory_space=pl.ANY),
                      tok(d),
                      pl.BlockSpec((None, 2, 6, d), lambda i, j, *_: (i, 0, 0, 0)),
                      pl.BlockSpec((1, d), lambda i, j, *_: (0, 0))],
            out_specs=tok(d),
            scratch_shapes=[pltpu.VMEM((ne * win, d), BF16),
                            pltpu.VMEM((tt, d), F32),
                            pltpu.SemaphoreType.DMA((ne,))]),
        compiler_params=_cparams("arbitrary", "arbitrary"),
        name="moe_combine",
    )(start, npass.reshape(-1), jnp.swapaxes(pos_t, 1, 2), jnp.swapaxes(gate_t, 1, 2), yout, xa, modtab,
      g_post.reshape(1, d))


def _ec_moe(logits_t, h, xa, modtab, g_post, w_gate, w_up, w_down, layer, ctx_len):
    b, ne, ta = logits_t.shape
    d = h.shape[-1]
    cap_ctx = max(1, EC_CAPACITY * ctx_len // ne)
    cap_lat = max(1, EC_CAPACITY * (ta - ctx_len) // ne)
    nslot = cap_ctx + cap_lat
    pos_t, gate_t = _moe_select(logits_t, ctx_len, cap_ctx, cap_lat)
    xin = _moe_gather(pos_t, h, nslot)
    pair = 2 if b % 2 == 0 else 1
    yout = _moe_ffn(xin.reshape(ne, b // pair, pair * nslot, d), w_gate, w_up, w_down, layer)
    yout = yout.reshape(ne, b, nslot, d)
    return _moe_combine(pos_t, gate_t, yout, xa, modtab, g_post, ctx_len)


def kernel(x, c, ctx, c_ctx, w_mod, b_mod, g_mix_pre, g_mix_post, g_ffn_pre, g_ffn_post, w_router, w_exp_gate, w_exp_up, w_exp_down, ev_w_in, ev_w_out, hgrn_lb, hgrn_g_norm, na_rpb, od_w_in, od_w_out, q_norm, k_norm, s5_a_re, s5_a_im, s5_log_dt, s5_b_re, s5_b_im, s5_c_re, s5_c_im, s5_d, s5_w_glu, s5_b_glu):
    b, t, d = x.shape
    ctx_len = ctx.shape[1]
    depth = w_mod.shape[0]
    assert depth == 2 and b <= 7
    ta = ctx_len + t
    a_width = d // 2
    s5_width = d // 4
    cq_width = d - s5_width
    ckv_width = cq_width // 3

    xa = jnp.concatenate([ctx, x], axis=1)
    cc = jnp.concatenate([c, c_ctx[None], jnp.zeros((7 - b, d), F32)], axis=0)
    mod = _modulation(cc, w_mod, b_mod)
    mod_lat = mod[:, :b].reshape(depth, b, 1, 6, d)
    mod_ctx = jnp.broadcast_to(mod[:, b].reshape(depth, 1, 1, 6, d), (depth, b, 1, 6, d))
    modtab = jnp.concatenate([mod_ctx, mod_lat], axis=2)

    lb_all = jnp.cumsum(jax.nn.softmax(hgrn_lb.astype(F32), axis=0), axis=0)
    hx = _prenorm(xa, g_mix_pre[0], modtab[0])
    p = _matmul(hx.reshape(b * ta, d), ev_w_in[0]).reshape(b, ta, -1)
    o_f, o_b = _hgrn(p, lb_all[0], ctx_len, a_width)
    mix_a = _hgrn_readout(o_f, o_b, p, hgrn_g_norm[0], a_width)
    mix_b = _natten(p, _natten_bias(na_rpb[0]), ctx_len, 5 * a_width, d - a_width)
    xa, h2, logits = _postmix(mix_a, mix_b, ev_w_out[0], xa, modtab[0], g_mix_post[0], g_ffn_pre[0], w_router[0])
    xa = _ec_moe(jnp.swapaxes(logits, 1, 2), h2, xa, modtab[0], g_ffn_post[0], w_exp_gate, w_exp_up, w_exp_down, 0, ctx_len)

    hx = _prenorm(xa, g_mix_pre[1], modtab[1])
    p = _matmul(hx.reshape(b * ta, d), od_w_in[0]).reshape(b, ta, -1)
    cosf, sinf = _rope_tables(ctx_len, t)
    qn, kn, vb = _qkprep(p, cosf, sinf, q_norm[0], k_norm[0], cq_width, ckv_width)
    mix_a = _gqa_attention(qn, kn, vb, cq_width // ckv_width)
    ops = _s5_operators(s5_a_re[0], s5_a_im[0], s5_log_dt[0], s5_b_re[0], s5_b_im[0], s5_c_re[0], s5_c_im[0])
    mix_b = _s5(p, cq_width + 2 * ckv_width, s5_width, ctx_len, ops, s5_d[0], s5_w_glu[0], s5_b_glu[0])
    xa, h2, logits = _postmix(mix_a, mix_b, od_w_out[0], xa, modtab[1], g_mix_post[1], g_ffn_pre[1], w_router[1])
    xa = _ec_moe(jnp.swapaxes(logits, 1, 2), h2, xa, modtab[1], g_ffn_post[1], w_exp_gate, w_exp_up, w_exp_down, 1, ctx_len)
    return xa[:, ctx_len:]
```

```python
import functools
import math

import jax
import jax.numpy as jnp
from jax import lax
from jax.experimental import pallas as pl
from jax.experimental.pallas import tpu as pltpu

F32 = jnp.float32
BF16 = jnp.bfloat16
HIGHEST = lax.Precision.HIGHEST
EPS = 1e-6

LANE = 128
TOK_TILE = 256
VMEM_LIMIT = 52 << 20

GRID_W = 64
NA_ROWS = 8
NA_COLS = 16
NA_GROUP = 4
NA_SPAN = 12
HEAD = 128
A_CHUNK = 64
N_EXPERTS = 16
EC_CAPACITY = 2
S5_GROUP = 16
S5_STATE = 64
S5_CHUNK = 16
COMBINE_WIN = 128
GATHER_WIN = 64
ROPE_THETA = 10000.0
NEG_BIG = -1e30
LOG2E = 1.4426950408889634


def _cparams(*sem):
    return pltpu.CompilerParams(dimension_semantics=sem, vmem_limit_bytes=VMEM_LIMIT)


def _dot(a, b):
    return jnp.dot(a, b, preferred_element_type=F32)


def _dot_nt(a, b):
    return lax.dot_general(a, b, (((1,), (1,)), ((), ())), preferred_element_type=F32)


def _dot_tn(a, b):
    return lax.dot_general(a, b, (((0,), (0,)), ((), ())), preferred_element_type=F32)


def _rms(x):
    return x * lax.rsqrt(jnp.mean(x * x, axis=-1, keepdims=True) + EPS)


def _silu(x):
    return x * jax.nn.sigmoid(x)


def _mod_kernel(c_ref, w_ref, b_ref, o_ref):
    o_ref[...] = jnp.dot(_silu(c_ref[...]), w_ref[...], preferred_element_type=F32, precision=HIGHEST) + b_ref[...]


def _modulation(cc, w_mod, b_mod):
    depth, d, n = w_mod.shape
    tn = 1024
    return pl.pallas_call(
        _mod_kernel,
        out_shape=jax.ShapeDtypeStruct((depth, cc.shape[0], n), F32),
        grid=(depth, n // tn),
        in_specs=[pl.BlockSpec(cc.shape, lambda l, j: (0, 0)),
                  pl.BlockSpec((None, d, tn), lambda l, j: (l, 0, j)),
                  pl.BlockSpec((None, 1, tn), lambda l, j: (l, 0, j))],
        out_specs=pl.BlockSpec((None, cc.shape[0], tn), lambda l, j: (l, 0, j)),
        compiler_params=_cparams("arbitrary", "arbitrary"),
        name="modulation",
    )(cc, w_mod, b_mod.reshape(depth, 1, n))


def _mod_spec(d):
    return pl.BlockSpec((None, None, 6, d), lambda b, j: (b, jnp.minimum(j, 1), 0, 0))


def _prenorm_kernel(x_ref, g_ref, m_ref, o_ref):
    m = m_ref[...]
    y = _rms(x_ref[...]) * g_ref[...]
    o_ref[...] = (y * (1.0 + m[1:2]) + m[0:1]).astype(BF16)


def _prenorm(xa, g, modtab):
    b, ta, d = xa.shape
    return pl.pallas_call(
        _prenorm_kernel,
        out_shape=jax.ShapeDtypeStruct((b, ta, d), BF16),
        grid=(b, ta // TOK_TILE),
        in_specs=[pl.BlockSpec((None, TOK_TILE, d), lambda i, j: (i, j, 0)),
                  pl.BlockSpec((1, d), lambda i, j: (0, 0)),
                  _mod_spec(d)],
        out_specs=pl.BlockSpec((None, TOK_TILE, d), lambda i, j: (i, j, 0)),
        compiler_params=_cparams("parallel", "parallel"),
        name="prenorm",
    )(xa, g.reshape(1, d), modtab)


def _mm_kernel(a_ref, w_ref, o_ref, wb_ref):
    @pl.when(pl.program_id(1) == 0)
    def _():
        wb_ref[...] = w_ref[...].astype(BF16)

    o_ref[...] = _dot(a_ref[...], wb_ref[...]).astype(o_ref.dtype)


def _matmul(a, w, tm=512, tn=1024, out_dtype=F32):
    m, k = a.shape
    n = w.shape[1]
    return pl.pallas_call(
        _mm_kernel,
        out_shape=jax.ShapeDtypeStruct((m, n), out_dtype),
        grid=(n // tn, m // tm),
        in_specs=[pl.BlockSpec((tm, k), lambda j, i: (i, 0)),
                  pl.BlockSpec((k, tn), lambda j, i: (0, j))],
        out_specs=pl.BlockSpec((tm, tn), lambda j, i: (i, j)),
        scratch_shapes=[pltpu.VMEM((k, tn), BF16)],
        compiler_params=_cparams("arbitrary", "arbitrary"),
        name="proj_in",
    )(a, w)


def _hgrn_masks(chunk):
    row = lax.broadcasted_iota(jnp.int32, (chunk, LANE), 0)
    ti = lax.broadcasted_iota(jnp.int32, (chunk, chunk), 0)
    si = lax.broadcasted_iota(jnp.int32, (chunk, chunk), 1)
    levels = [(((row >> lvl) & 1) == 1, (ti >> (lvl + 1)) == (si >> (lvl + 1))) for lvl in range(chunk.bit_length() - 1)]
    return ti == si, levels


def _hgrn_chunk(q, fr, v, lb, st, rev, masks):
    f = lb + (1.0 - lb) * jax.nn.sigmoid(fr)
    kk = 1.0 - f
    lf = jnp.log(f)
    chunk = q.shape[0]
    diag, levels = masks
    att = jnp.where(diag, _dot_nt(q.astype(BF16), kk.astype(BF16)), 0.0)
    p_in = lf
    r_ex = jnp.zeros_like(lf)
    tot = lf
    for lvl, (bit, same) in enumerate(levels):
        step = 1 << lvl
        up = pltpu.roll(tot, step, 0)
        dn = pltpu.roll(tot, chunk - step, 0)
        is_q = jnp.logical_not(bit) if rev else bit
        qf = jnp.where(is_q, jnp.exp(p_in) * q, 0.0).astype(BF16)
        kf = jnp.where(is_q, 0.0, jnp.exp(r_ex) * kk).astype(BF16)
        att = att + jnp.where(same, _dot_nt(qf, kf), 0.0)
        if rev:
            p_in = p_in + jnp.where(bit, 0.0, dn)
            r_ex = r_ex + jnp.where(bit, up, 0.0)
        else:
            p_in = p_in + jnp.where(bit, up, 0.0)
            r_ex = r_ex + jnp.where(bit, 0.0, dn)
        tot = tot + jnp.where(bit, up, dn)
    vb = v.astype(BF16)
    o = _dot_nt((q * jnp.exp(p_in)).astype(BF16), st.astype(BF16)) + _dot(att.astype(BF16), vb)
    kd = (kk * jnp.exp(r_ex)).astype(BF16)
    st_new = st * jnp.exp(tot[0:1]) + _dot_tn(vb, kd)
    return o, st_new


def _hgrn_kernel(qf_ref, ff_ref, vf_ref, qb_ref, fb_ref, vb_ref, lb_ref, of_ref, ob_ref, st_ref, *, chunk, hb):
    @pl.when(pl.program_id(2) == 0)
    def _():
        st_ref[...] = jnp.zeros_like(st_ref)

    masks = _hgrn_masks(chunk)
    for h in range(hb):
        sl = slice(h * HEAD, (h + 1) * HEAD)
        o, st = _hgrn_chunk(qf_ref[:, sl], ff_ref[:, sl], vf_ref[:, sl], lb_ref[0:1, sl], st_ref[0, h], False, masks)
        of_ref[:, sl] = o
        st_ref[0, h] = st
        o, st = _hgrn_chunk(qb_ref[:, sl], fb_ref[:, sl], vb_ref[:, sl], lb_ref[1:2, sl], st_ref[1, h], True, masks)
        ob_ref[:, sl] = o
        st_ref[1, h] = st


def _hgrn(p, lb, ctx_len, width, hb=2):
    b, ta, _ = p.shape
    chunk = A_CHUNK
    n = ta // chunk
    nc = ctx_len // chunk
    cw = HEAD * hb
    nh = width // cw
    bwd = lambda j: jnp.where(j < nc, nc - 1 - j, n - 1 + nc - j)

    def spec(base, rev):
        return pl.BlockSpec((None, chunk, cw), lambda i, h, j: (i, bwd(j) if rev else j, base // cw + h))

    out = jax.ShapeDtypeStruct((b, ta, width), F32)
    return pl.pallas_call(
        functools.partial(_hgrn_kernel, chunk=chunk, hb=hb),
        out_shape=(out, out),
        grid=(b, nh, n),
        in_specs=[spec(0, False), spec(width, False), spec(3 * width, False),
                  spec(0, True), spec(2 * width, True), spec(3 * width, True),
                  pl.BlockSpec((2, cw), lambda i, h, j: (0, h))],
        out_specs=(spec(0, False), spec(0, True)),
        scratch_shapes=[pltpu.VMEM((2, hb, HEAD, HEAD), F32)],
        compiler_params=_cparams("parallel", "parallel", "arbitrary"),
        name="hgrn_scan",
    )(p, p, p, p, p, p, lb)


def _hgrn_readout_kernel(of_ref, ob_ref, g_ref, gn_ref, o_ref):
    for h in range(o_ref.shape[-1] // HEAD):
        sl = slice(h * HEAD, (h + 1) * HEAD)
        o = of_ref[:, sl] + ob_ref[:, sl]
        o_ref[:, sl] = (_rms(o) * gn_ref[...] * _silu(g_ref[:, sl])).astype(BF16)


def _hgrn_readout(o_f, o_b, p, g_norm, width):
    b, ta, _ = o_f.shape
    spec = pl.BlockSpec((None, TOK_TILE, width), lambda i, j: (i, j, 0))
    return pl.pallas_call(
        _hgrn_readout_kernel,
        out_shape=jax.ShapeDtypeStruct((b, ta, width), BF16),
        grid=(b, ta // TOK_TILE),
        in_specs=[spec, spec,
                  pl.BlockSpec((None, TOK_TILE, width), lambda i, j: (i, j, 4)),
                  pl.BlockSpec((1, HEAD), lambda i, j: (0, 0))],
        out_specs=spec,
        compiler_params=_cparams("parallel", "parallel"),
        name="hgrn_readout",
    )(o_f, o_b, p, g_norm.reshape(1, HEAD))


def _natten_bias(rpb):
    nrow, ncol = 2 * NA_ROWS - 1, 2 * NA_COLS - 1
    col = jnp.arange(GRID_W)
    col_start = jnp.clip(col - NA_COLS // 2, 0, GRID_W - NA_COLS)
    cmask = (col[None, :] >= col_start[:, None]) & (col[None, :] < col_start[:, None] + NA_COLS)
    dc = col[None, :] - col[:, None] + NA_COLS - 1
    cm = (cmask[:, :, None] & (dc[:, :, None] == jnp.arange(ncol))).astype(F32)
    cls = jnp.arange(3)[:, None, None]
    i = jnp.arange(NA_GROUP)[None, :, None]
    w = jnp.arange(NA_SPAN)[None, None, :]
    first_w = jnp.where(cls == 0, 0, jnp.where(cls == 1, i, NA_SPAN - NA_ROWS))
    valid = (w >= first_w) & (w < first_w + NA_ROWS)
    span0 = jnp.where(cls == 0, 0, jnp.where(cls == 1, -(NA_ROWS // 2), NA_GROUP - NA_SPAN))
    dr = span0 + w - i + NA_ROWS - 1
    rm = (valid[..., None] & (dr[..., None] == jnp.arange(nrow))).astype(F32)
    t = jnp.einsum('qkb,hab->haqk', cm, rpb.astype(F32), precision=HIGHEST)
    t = jnp.einsum('ciwa,haqk->hciqwk', rm, t, precision=HIGHEST)
    ok = valid[None, :, :, None, :, None] & cmask[None, None, None, :, None, :]
    t = jnp.where(ok, t, NEG_BIG)
    return t.reshape(rpb.shape[0], 3, NA_GROUP * GRID_W, NA_SPAN * GRID_W)


def _natten_kernel(q_ref, k_ref, v_ref, bias_ref, o_ref, *, ctx_len, rows):
    j = pl.program_id(2)
    nst = rows // NA_GROUP
    scale = HEAD ** -0.5
    heads = [slice(h * HEAD, (h + 1) * HEAD) for h in range(q_ref.shape[-1] // HEAD)]

    def ctx_scores(sl):
        qb = q_ref[:, sl].astype(BF16)
        vc = v_ref[0:ctx_len, sl].astype(BF16)
        s_ctx = _dot_nt(qb, k_ref[0:ctx_len, sl].astype(BF16)) * scale
        return qb, vc, s_ctx, jnp.max(s_ctx, axis=1, keepdims=True)

    @pl.when(j < nst)
    def _():
        last = j == nst - 1
        cls = jnp.where(j == 0, 0, jnp.where(last, 2, 1))
        span_row = jnp.where(j == 0, 0, jnp.where(last, rows - NA_SPAN, NA_GROUP * j - NA_ROWS // 2))
        start = pl.multiple_of(ctx_len + span_row * GRID_W, GRID_W)
        for h, sl in enumerate(heads):
            qb, vc, s_ctx, m_ctx = ctx_scores(sl)
            kw = k_ref[pl.ds(start, NA_SPAN * GRID_W), sl].astype(BF16)
            vw = v_ref[pl.ds(start, NA_SPAN * GRID_W), sl].astype(BF16)
            s_win = _dot_nt(qb, kw) * scale + bias_ref[h, cls]
            m = jnp.maximum(m_ctx, jnp.max(s_win, axis=1, keepdims=True))
            pw = jnp.exp(s_win - m)
            pc = jnp.exp(s_ctx - m)
            den = jnp.sum(pw, axis=1, keepdims=True) + jnp.sum(pc, axis=1, keepdims=True)
            o = _dot(pw.astype(BF16), vw) + _dot(pc.astype(BF16), vc)
            o_ref[:, sl] = (o / den).astype(BF16)

    @pl.when(j >= nst)
    def _():
        for sl in heads:
            _, vc, s_ctx, m_ctx = ctx_scores(sl)
            pc = jnp.exp(s_ctx - m_ctx)
            o = _dot(pc.astype(BF16), vc) / jnp.sum(pc, axis=1, keepdims=True)
            o_ref[:, sl] = o.astype(BF16)


def _natten(p, bias, ctx_len, col0, width, hb=2):
    b, ta, _ = p.shape
    cw = HEAD * hb
    nh = width // cw
    rows = (ta - ctx_len) // GRID_W
    tq = NA_GROUP * GRID_W
    nst = rows // NA_GROUP
    ncq = ctx_len // tq
    assert rows % NA_GROUP == 0 and rows >= NA_SPAN and ctx_len % tq == 0

    def qmap(cb):
        return lambda i, h, j: (i, jnp.where(j < nst, ncq + j, j - nst), cb + h)

    return pl.pallas_call(
        functools.partial(_natten_kernel, ctx_len=ctx_len, rows=rows),
        out_shape=jax.ShapeDtypeStruct((b, ta, width), BF16),
        grid=(b, nh, nst + ncq),
        in_specs=[pl.BlockSpec((None, tq, cw), qmap(col0 // cw)),
                  pl.BlockSpec((None, ta, cw), lambda i, h, j: (i, 0, (col0 + width) // cw + h)),
                  pl.BlockSpec((None, ta, cw), lambda i, h, j: (i, 0, (col0 + 2 * width) // cw + h)),
                  pl.BlockSpec((hb, 3, tq, NA_SPAN * GRID_W), lambda i, h, j: (h, 0, 0, 0))],
        out_specs=pl.BlockSpec((None, tq, cw), qmap(0)),
        compiler_params=_cparams("parallel", "parallel", "arbitrary"),
        name="natten",
    )(p, p, p, bias)


def _rope_tables(ctx_len, t):
    pos = jnp.arange(t)
    row = (pos // GRID_W).astype(F32)
    col = (pos % GRID_W).astype(F32)
    half = HEAD // 2
    inv = ROPE_THETA ** (-jnp.arange(0, half, 2, dtype=F32) / half)
    ang = jnp.concatenate([row[:, None] * inv, col[:, None] * inv], axis=-1)
    cos, sin = jnp.cos(ang), jnp.sin(ang)
    cosf = jnp.repeat(cos, 2, axis=-1)
    sinf = jnp.stack([-sin, sin], axis=-1).reshape(t, HEAD)
    cosf = jnp.concatenate([jnp.ones((ctx_len, HEAD), F32), cosf], axis=0)
    sinf = jnp.concatenate([jnp.zeros((ctx_len, HEAD), F32), sinf], axis=0)
    return cosf, sinf


def _qkprep_kernel(q_ref, k_ref, v_ref, cos_ref, sin_ref, qn_ref, kn_ref, qo_ref, ko_ref, vo_ref):
    cosf = cos_ref[...]
    sinf = sin_ref[...]
    even = (lax.broadcasted_iota(jnp.int32, cosf.shape, 1) & 1) == 0

    def rope(x):
        partner = jnp.where(even, pltpu.roll(x, LANE - 1, 1), pltpu.roll(x, 1, 1))
        return x * cosf + partner * sinf

    for h in range(q_ref.shape[-1] // HEAD):
        sl = slice(h * HEAD, (h + 1) * HEAD)
        qo_ref[:, sl] = (rope(_rms(q_ref[:, sl]) * qn_ref[...]) * (HEAD ** -0.5 * LOG2E)).astype(BF16)
    for h in range(k_ref.shape[-1] // HEAD):
        sl = slice(h * HEAD, (h + 1) * HEAD)
        ko_ref[:, sl] = rope(_rms(k_ref[:, sl]) * kn_ref[...]).astype(BF16)
    vo_ref[...] = v_ref[...].astype(BF16)


def _qkprep(p, cosf, sinf, q_norm, k_norm, qw, kw):
    b, ta, _ = p.shape
    tt = TOK_TILE

    def spec(w, cb):
        return pl.BlockSpec((None, tt, w), lambda i, j: (i, j, cb))

    return pl.pallas_call(
        _qkprep_kernel,
        out_shape=(jax.ShapeDtypeStruct((b, ta, qw), BF16),
                   jax.ShapeDtypeStruct((b, ta, kw), BF16),
                   jax.ShapeDtypeStruct((b, ta, kw), BF16)),
        grid=(b, ta // tt),
        in_specs=[spec(qw, 0), spec(kw, qw // kw), spec(kw, qw // kw + 1),
                  pl.BlockSpec((tt, HEAD), lambda i, j: (j, 0)),
                  pl.BlockSpec((tt, HEAD), lambda i, j: (j, 0)),
                  pl.BlockSpec((1, HEAD), lambda i, j: (0, 0)),
                  pl.BlockSpec((1, HEAD), lambda i, j: (0, 0))],
        out_specs=(spec(qw, 0), spec(kw, 0), spec(kw, 0)),
        compiler_params=_cparams("parallel", "parallel"),
        name="qk_prep",
    )(p, p, p, cosf, sinf, q_norm.reshape(1, HEAD), k_norm.reshape(1, HEAD))


def _attn_kernel(q_ref, k_ref, v_ref, o_ref):
    s = _dot_nt(q_ref[...], k_ref[...])
    p = jnp.exp2(s - jnp.max(s, axis=1, keepdims=True))
    o = _dot(p.astype(BF16), v_ref[...]) / jnp.sum(p, axis=1, keepdims=True)
    o_ref[...] = o.astype(BF16)


def _gqa_attention(qn, kn, vb, group):
    b, ta, qw = qn.shape
    nkv = kn.shape[-1] // HEAD
    tq = ta // 8
    return pl.pallas_call(
        _attn_kernel,
        out_shape=jax.ShapeDtypeStruct((b, ta, qw), BF16),
        grid=(b, nkv, group, ta // tq),
        in_specs=[pl.BlockSpec((None, tq, HEAD), lambda i, h, g, j: (i, j, h * group + g)),
                  pl.BlockSpec((None, ta, HEAD), lambda i, h, g, j: (i, 0, h)),
                  pl.BlockSpec((None, ta, HEAD), lambda i, h, g, j: (i, 0, h))],
        out_specs=pl.BlockSpec((None, tq, HEAD), lambda i, h, g, j: (i, j, h * group + g)),
        compiler_params=_cparams("parallel", "parallel", "parallel", "parallel"),
        name="gqa_attention",
    )(qn, kn, vb)


def _s5_operators(a_re, a_im, log_dt, b_re, b_im, c_re, c_im):
    lc, pp, ns = S5_CHUNK, S5_GROUP, S5_STATE
    a_c = lax.complex(a_re.astype(F32), a_im.astype(F32))
    adt = a_c * jnp.exp(log_dt.astype(F32))[..., None]
    a_bar = jnp.exp(adt)
    b_bar = ((a_bar - 1.0) / a_c)[..., None] * lax.complex(b_re.astype(F32), b_im.astype(F32))
    c_mat = lax.complex(c_re.astype(F32), c_im.astype(F32))
    nd, g = a_re.shape[0], a_re.shape[1]
    pos = jnp.arange(lc)
    lag = pos[None, :] - pos[:, None]
    lag = jnp.stack([lag, -lag])
    live = (lag >= 0)[:, None, :, :, None]
    apl = jnp.where(live, jnp.exp(adt[:, :, None, None, :] * jnp.maximum(lag, 0).astype(F32)[:, None, :, :, None]), 0.0)
    tm = jnp.real(jnp.einsum('dgpn,dgion,dgnq->dgiqop', c_mat, apl, b_bar, precision=HIGHEST))
    tmat = tm.reshape(nd, g, lc * pp, lc * pp)
    steps_after = jnp.stack([lc - 1 - pos, pos]).astype(F32)
    gm = jnp.exp(adt[:, :, None, :] * steps_after[:, None, :, None])[:, :, :, None, :] * jnp.swapaxes(b_bar, 2, 3)[:, :, None]
    gm = gm.reshape(nd, g, lc * pp, ns)
    steps_upto = jnp.stack([pos + 1, lc - pos]).astype(F32)
    hm = c_mat[:, :, None] * jnp.exp(adt[:, :, None, :] * steps_upto[:, None, :, None])[:, :, :, None, :]
    hm = jnp.transpose(hm, (0, 1, 4, 2, 3)).reshape(nd, g, ns, lc * pp)
    hmat = jnp.concatenate([jnp.real(hm), -jnp.imag(hm)], axis=2)
    al = jnp.exp(adt * float(lc))
    al = al[:, :, None, :]
    gd = lambda x: jnp.swapaxes(x, 0, 1)
    return (gd(tmat).astype(BF16), gd(jnp.real(gm)).astype(BF16), gd(jnp.imag(gm)).astype(BF16),
            gd(hmat).astype(BF16), gd(jnp.real(al)), gd(jnp.imag(al)))


def _s5_kernel(u_ref, t_ref, gr_ref, gi_ref, h_ref, ar_ref, ai_ref, d_ref, y_ref, er_ref, ei_ref, xr_ref, xi_ref,
               *, nchunk, nctx, rpc):
    ns = S5_STATE
    u = u_ref[...]
    for dr in range(2):
        er_ref[dr] = _dot(u, gr_ref[dr])
        ei_ref[dr] = _dot(u, gi_ref[dr])

    coef = [(jnp.broadcast_to(ar_ref[dr], (rpc, ns)), jnp.broadcast_to(ai_ref[dr], (rpc, ns))) for dr in range(2)]

    def body(c, carry):
        out = []
        for dr in range(2):
            xr, xi = carry[2 * dr], carry[2 * dr + 1]
            pc = c if dr == 0 else jnp.where(c < nctx, nctx - 1 - c, nchunk - 1 + nctx - c)
            off = pl.multiple_of(pc * rpc, rpc)
            xr_ref[dr, pl.ds(off, rpc), :] = xr
            xi_ref[dr, pl.ds(off, rpc), :] = xi
            ar, ai = coef[dr]
            out.append(ar * xr - ai * xi + er_ref[dr, pl.ds(off, rpc), :])
            out.append(ar * xi + ai * xr + ei_ref[dr, pl.ds(off, rpc), :])
        return tuple(out)

    zero = jnp.zeros((rpc, ns), F32)
    lax.fori_loop(0, nchunk, body, (zero, zero, zero, zero), unroll=4)
    y = d_ref[...] * u.astype(F32)
    for dr in range(2):
        y = y + _dot(u, t_ref[dr])
        y = y + _dot(xr_ref[dr].astype(BF16), h_ref[dr, 0:ns, :]) + _dot(xi_ref[dr].astype(BF16), h_ref[dr, ns:2 * ns, :])
    y_ref[...] = y.astype(BF16)


def _s5_scan(u, ops, d_skip, ctx_len):
    tmat, g_re, g_im, hmat, a_re, a_im = ops
    b, ta, w = u.shape
    lc, pp, ns = S5_CHUNK, S5_GROUP, S5_STATE
    g = w // pp
    nchunk = ta // lc
    rpc = 8
    cw = lc * pp
    rows = nchunk * rpc
    ug = jnp.transpose(u.astype(BF16).reshape(b, nchunk, lc, g, pp), (3, 1, 0, 2, 4))
    ug = jnp.pad(ug, ((0, 0), (0, 0), (0, rpc - b), (0, 0), (0, 0))).reshape(g, rows, cw)
    dvec = jnp.tile(d_skip.astype(F32).reshape(g, 1, pp), (1, lc, 1)).reshape(g, 1, cw)
    op = lambda r, c: pl.BlockSpec((None, 2, r, c), lambda i: (i, 0, 0, 0))
    y = pl.pallas_call(
        functools.partial(_s5_kernel, nchunk=nchunk, nctx=ctx_len // lc, rpc=rpc),
        out_shape=jax.ShapeDtypeStruct((g, rows, cw), BF16),
        grid=(g,),
        in_specs=[pl.BlockSpec((None, rows, cw), lambda i: (i, 0, 0)),
                  op(cw, cw), op(cw, ns), op(cw, ns), op(2 * ns, cw), op(1, ns), op(1, ns),
                  pl.BlockSpec((None, 1, cw), lambda i: (i, 0, 0))],
        out_specs=pl.BlockSpec((None, rows, cw), lambda i: (i, 0, 0)),
        scratch_shapes=[pltpu.VMEM((2, rows, ns), F32) for _ in range(4)],
        compiler_params=_cparams("parallel"),
        name="s5_scan",
    )(ug, tmat, g_re, g_im, hmat, a_re, a_im, dvec)
    y = y.reshape(g, nchunk, rpc, lc, pp)[:, :, :b]
    return jnp.transpose(y, (2, 1, 3, 0, 4)).reshape(b, ta, w)


def _s5_glu_kernel(y_ref, w_ref, b_ref, o_ref):
    y = y_ref[...].astype(F32)
    y = 0.5 * y * (1.0 + jnp.tanh(math.sqrt(2.0 / math.pi) * (y + 0.044715 * (y * y * y))))
    z = _dot(y.astype(BF16), w_ref[...]) + b_ref[...]
    o_ref[...] = (y * jax.nn.sigmoid(z)).astype(BF16)


def _s5_glu(y, w_glu, b_glu):
    b, ta, w = y.shape
    spec = pl.BlockSpec((None, TOK_TILE, w), lambda i, j: (i, j, 0))
    return pl.pallas_call(
        _s5_glu_kernel,
        out_shape=jax.ShapeDtypeStruct((b, ta, w), BF16),
        grid=(b, ta // TOK_TILE),
        in_specs=[spec, pl.BlockSpec((w, w), lambda i, j: (0, 0)), pl.BlockSpec((1, w), lambda i, j: (0, 0))],
        out_specs=spec,
        compiler_params=_cparams("parallel", "parallel"),
        name="s5_glu",
    )(y, w_glu.astype(BF16), b_glu.reshape(1, w))


def _s5(p, ucol, width, ctx_len, ops, d_skip, w_glu, b_glu):
    y = _s5_scan(p[..., ucol:ucol + width], ops, d_skip, ctx_len)
    return _s5_glu(y, w_glu, b_glu)


def _postmix_kernel(a_ref, b_ref, w_ref, x_ref, m_ref, gpost_ref, gpre_ref, wr_ref, xo_ref, h_ref, lg_ref):
    wa = a_ref.shape[-1]
    y = _dot(a_ref[...], w_ref[0:wa, :]) + _dot(b_ref[...], w_ref[wa:, :])
    m = m_ref[...]
    xn = x_ref[...] + m[2:3] * (_rms(y) * gpost_ref[...])
    xo_ref[...] = xn
    h2 = _rms(xn) * gpre_ref[...] * (1.0 + m[4:5]) + m[3:4]
    hi = h2.astype(BF16)
    h_ref[...] = hi
    lo = (h2 - hi.astype(F32)).astype(BF16)
    ne = lg_ref.shape[-1]
    both = _dot(hi, wr_ref[...])
    lg_ref[...] = both[:, 0:ne] + both[:, ne:2 * ne] + _dot(lo, wr_ref[:, 0:ne])


def _postmix(mix_a, mix_b, w_out, xa, modtab, g_post, g_pre, w_router):
    b, ta, d = xa.shape
    wa, wb = mix_a.shape[-1], mix_b.shape[-1]
    ne = w_router.shape[-1]
    tok = lambda w: pl.BlockSpec((None, TOK_TILE, w), lambda i, j: (i, j, 0))
    vec = pl.BlockSpec((1, d), lambda i, j: (0, 0))
    wr_hi = w_router.astype(BF16)
    wr2 = jnp.concatenate([wr_hi, (w_router - wr_hi.astype(F32)).astype(BF16)], axis=1)
    return pl.pallas_call(
        _postmix_kernel,
        out_shape=(jax.ShapeDtypeStruct((b, ta, d), F32),
                   jax.ShapeDtypeStruct((b, ta, d), BF16),
                   jax.ShapeDtypeStruct((b, ta, ne), F32)),
        grid=(b, ta // TOK_TILE),
        in_specs=[tok(wa), tok(wb),
                  pl.BlockSpec((wa + wb, d), lambda i, j: (0, 0)),
                  tok(d), _mod_spec(d), vec, vec,
                  pl.BlockSpec((d, 2 * ne), lambda i, j: (0, 0))],
        out_specs=(tok(d), tok(d), tok(ne)),
        compiler_params=_cparams("parallel", "parallel"),
        name="mix_out",
    )(mix_a, mix_b, w_out.astype(BF16), xa, modtab, g_post.reshape(1, d), g_pre.reshape(1, d), wr2)


def _select_kernel(lg_ref, pos_ref, gate_ref, *, ctx_len, cap_ctx, cap_lat):
    lg = lg_ref[...]
    ne, ta = lg.shape
    ex = jnp.exp(lg - jnp.max(lg, axis=0, keepdims=True))
    probs = ex / jnp.sum(ex, axis=0, keepdims=True)
    bits = pltpu.bitcast(probs, jnp.int32)
    is_ctx = lax.broadcasted_iota(jnp.int32, (ne, ta), 1) < ctx_len

    def counts(mask):
        mf = jnp.where(mask, 1.0, 0.0)
        return (jnp.sum(jnp.where(is_ctx, mf, 0.0), axis=1, keepdims=True),
                jnp.sum(jnp.where(is_ctx, 0.0, mf), axis=1, keepdims=True))

    def search(i, carry):
        pc, pt = carry
        bit = jnp.left_shift(jnp.int32(1), 30 - i)
        cc, ct = counts(bits >= jnp.where(is_ctx, pc | bit, pt | bit))
        return jnp.where(cc >= cap_ctx, pc | bit, pc), jnp.where(ct >= cap_lat, pt | bit, pt)

    z = jnp.zeros((ne, 1), jnp.int32)
    pc, pt = lax.fori_loop(0, 31, search, (z, z))
    thr = jnp.where(is_ctx, pc, pt)
    gt = bits > thr
    eq = bits == thr
    gc, gl = counts(gt)
    need = jnp.where(is_ctx, cap_ctx - gc, cap_lat - gl)

    nb = ta // LANE
    ut = jnp.where(lax.broadcasted_iota(jnp.int32, (LANE, LANE), 0) <= lax.broadcasted_iota(jnp.int32, (LANE, LANE), 1),
                   1.0, 0.0).astype(BF16)

    def lane_prefix(mask):
        mf = jnp.where(mask, 1.0, 0.0).astype(BF16)
        blocks = jnp.concatenate([mf[:, j * LANE:(j + 1) * LANE] for j in range(nb)], axis=0)
        inc = _dot(blocks, ut)
        outs = []
        off = jnp.zeros((ne, 1), F32)
        for j in range(nb):
            if j * LANE == ctx_len:
                off = jnp.zeros((ne, 1), F32)
            blk = inc[j * ne:(j + 1) * ne]
            outs.append(blk + off)
            off = off + blk[:, LANE - 1:LANE]
        return jnp.concatenate(outs, axis=1)

    sel = gt | (eq & (lane_prefix(eq) <= need))
    slot = lane_prefix(sel) - 1.0 + jnp.where(is_ctx, 0.0, float(cap_ctx))
    pos_ref[...] = jnp.where(sel, slot, -1.0)
    gate_ref[...] = jnp.where(sel, probs, 0.0)


def _moe_select(logits_t, ctx_len, cap_ctx, cap_lat):
    b, ne, ta = logits_t.shape
    spec = pl.BlockSpec((None, ne, ta), lambda i: (i, 0, 0))
    return pl.pallas_call(
        functools.partial(_select_kernel, ctx_len=ctx_len, cap_ctx=cap_ctx, cap_lat=cap_lat),
        out_shape=(jax.ShapeDtypeStruct((b, ne, ta), F32), jax.ShapeDtypeStruct((b, ne, ta), F32)),
        grid=(b,),
        in_specs=[spec],
        out_specs=(spec, spec),
        compiler_params=_cparams("parallel"),
        name="moe_select",
    )(logits_t)


def _slot_windows(pos_t, nt, win, nslot):
    b, ne, ta = pos_t.shape
    pt = pos_t.reshape(b, ne, nt, ta // nt)
    hi = jnp.max(pt, axis=-1).astype(jnp.int32) + 1
    lo = jnp.min(jnp.where(pt >= 0, pt, float(nslot)), axis=-1).astype(jnp.int32)
    lo = jnp.where(hi > 0, lo // 16 * 16, 0)
    npass = jnp.maximum(jnp.max((hi - lo + win - 1) // win, axis=1), 1)
    return jnp.transpose(lo, (0, 2, 1)).reshape(-1), npass.reshape(-1)


def _gather_kernel(start_ref, npass_ref, pos_ref, h_ref, o_ref, acc_ref):
    i = pl.program_id(0)
    eg = pl.program_id(1)
    j = pl.program_id(2)
    ge, tk = pos_ref.shape
    ne = pl.num_programs(1) * ge
    nslot = o_ref.shape[1]
    win = GATHER_WIN
    tile = i * pl.num_programs(2) + j

    @pl.when(j == 0)
    def _():
        acc_ref[...] = jnp.zeros_like(acc_ref)

    pos = pos_ref[...].astype(jnp.int32)
    slot = lax.broadcasted_iota(jnp.int32, (win, tk), 0)

    def one_pass(p, carry):
        src, blocks = [], []
        for el in range(ge):
            lo = start_ref[tile * ne + eg * ge + el] + p * win
            src.append(pl.multiple_of(jnp.minimum(lo, nslot - win), 16))
            pe = pos[el:el + 1, :]
            hit = (pe >= lo) & (pe < lo + win) & (slot == pe - src[el])
            blocks.append(jnp.where(hit, 1.0, 0.0).astype(BF16))
        part = _dot(jnp.concatenate(blocks, axis=0), h_ref[...])
        for el in range(ge):
            acc_ref[el, pl.ds(src[el], win), :] += part[el * win:(el + 1) * win]
        return carry

    lax.fori_loop(0, npass_ref[tile], one_pass, 0)

    @pl.when(j == pl.num_programs(2) - 1)
    def _():
        o_ref[...] = acc_ref[...].astype(BF16)


def _moe_gather(pos_t, h, nslot):
    b, ne, ta = pos_t.shape
    d = h.shape[-1]
    ge = 4
    tk = TOK_TILE
    nt = ta // tk
    start, npass = _slot_windows(pos_t, nt, GATHER_WIN, nslot)
    return pl.pallas_call(
        _gather_kernel,
        out_shape=jax.ShapeDtypeStruct((ne, b, nslot, d), BF16),
        grid_spec=pltpu.PrefetchScalarGridSpec(
            num_scalar_prefetch=2,
            grid=(b, ne // ge, nt),
            in_specs=[pl.BlockSpec((None, None, ge, tk), lambda i, g, j, *_: (i, g, 0, j)),
                      pl.BlockSpec((None, tk, d), lambda i, g, j, *_: (i, j, 0))],
            out_specs=pl.BlockSpec((ge, None, nslot, d), lambda i, g, j, *_: (g, i, 0, 0)),
            scratch_shapes=[pltpu.VMEM((ge, nslot, d), F32)]),
        compiler_params=_cparams("arbitrary", "arbitrary", "arbitrary"),
        name="moe_gather",
    )(start, npass, pos_t.reshape(b, ne // ge, ge, ta), h)


def _ffn_kernel(x_ref, wg_ref, wu_ref, wd_ref, o_ref, hid_ref, *, nf):
    s = pl.program_id(2)
    tf = wg_ref.shape[-1]

    @pl.when(s < nf)
    def _():
        x = x_ref[...]
        g = _dot(x, wg_ref[...].astype(BF16))
        u = _dot(x, wu_ref[...].astype(BF16))
        hid_ref[s] = (_silu(g) * u).astype(BF16)

    @pl.when(s >= nf)
    def _():
        acc = _dot(hid_ref[0], wd_ref[0:tf, :].astype(BF16))
        for f in range(1, nf):
            acc = acc + _dot(hid_ref[f], wd_ref[f * tf:(f + 1) * tf, :].astype(BF16))
        o_ref[...] = acc.astype(BF16)


def _moe_ffn(xin, w_gate, w_up, w_down, layer, tf=512):
    ne, ns, m, d = xin.shape
    ff = w_gate.shape[-1]
    nf = ff // tf
    nd = d // tf
    up = lambda e, s, f: (layer, e, 0, jnp.minimum(f, nf - 1))
    return pl.pallas_call(
        functools.partial(_ffn_kernel, nf=nf),
        out_shape=jax.ShapeDtypeStruct((ne, ns, m, d), BF16),
        grid=(ne, ns, nf + nd),
        in_specs=[pl.BlockSpec((None, None, m, d), lambda e, s, f: (e, s, 0, 0)),
                  pl.BlockSpec((None, None, d, tf), up),
                  pl.BlockSpec((None, None, d, tf), up),
                  pl.BlockSpec((None, None, ff, tf), lambda e, s, f: (layer, e, 0, jnp.maximum(f - nf, 0)))],
        out_specs=pl.BlockSpec((None, None, m, tf), lambda e, s, f: (e, s, 0, jnp.maximum(f - nf, 0))),
        scratch_shapes=[pltpu.VMEM((nf, m, tf), BF16)],
        compiler_params=_cparams("parallel", "parallel", "arbitrary"),
        name="moe_ffn",
    )(xin, w_gate, w_up, w_down)


def _combine_kernel(start_ref, npass_ref, pos_ref, gate_ref, y_hbm, x_ref, m_ref, g_ref, o_ref, ybuf, acc_ref, sem,
                    *, ctx_len):
    i = pl.program_id(0)
    j = pl.program_id(1)
    tt, ne = pos_ref.shape
    nslot = y_hbm.shape[2]
    win = COMBINE_WIN
    tile = i * pl.num_programs(1) + j
    pos = pos_ref[...].astype(jnp.int32)
    gate = gate_ref[...]
    lane = lax.broadcasted_iota(jnp.int32, (tt, win), 1)

    def window_copy(e, src):
        return pltpu.make_async_copy(y_hbm.at[e, i, pl.ds(src, win), :], ybuf.at[pl.ds(e * win, win), :], sem.at[e])

    def one_pass(p):
        own_lo, src = [], []
        for e in range(ne):
            lo = start_ref[tile * ne + e] + p * win
            own_lo.append(lo)
            src.append(pl.multiple_of(jnp.minimum(lo, nslot - win), 16))
            window_copy(e, src[e]).start()
        blocks = []
        for e in range(ne):
            pe = pos[:, e:e + 1]
            hit = (pe >= own_lo[e]) & (pe < own_lo[e] + win) & (lane == pe - src[e])
            blocks.append(jnp.where(hit, gate[:, e:e + 1], 0.0).astype(BF16))
        w = jnp.concatenate(blocks, axis=1)
        for e in range(ne):
            window_copy(e, src[e]).wait()
        return _dot(w, ybuf[...])

    acc_ref[...] = one_pass(0)

    def extra(p, carry):
        acc_ref[...] += one_pass(p)
        return carry

    lax.fori_loop(1, npass_ref[tile], extra, 0)
    m = m_ref[...]
    is_ctx = (lax.broadcasted_iota(jnp.int32, (tt, 1), 0) + j * tt) < ctx_len
    mod_gate = jnp.where(is_ctx, m[0, 5:6], m[1, 5:6])
    o_ref[...] = x_ref[...] + mod_gate * (_rms(acc_ref[...]) * g_ref[...])


def _moe_combine(pos_t, gate_t, yout, xa, modtab, g_post, ctx_len):
    b, ta, d = xa.shape
    ne, _, nslot, _ = yout.shape
    nt = 8
    tt = ta // nt
    win = COMBINE_WIN
    start, npass = _slot_windows(pos_t, nt, win, nslot)
    tok = lambda w: pl.BlockSpec((None, tt, w), lambda i, j, *_: (i, j, 0))
    return pl.pallas_call(
        functools.partial(_combine_kernel, ctx_len=ctx_len),
        out_shape=jax.ShapeDtypeStruct((b, ta, d), F32),
        grid_spec=pltpu.PrefetchScalarGridSpec(
            num_scalar_prefetch=2,
            grid=(b, nt),
            in_specs=[tok(ne), tok(ne),
                      pl.BlockSpec(memory_space=pl.ANY),
                      tok(d),
                      pl.BlockSpec((None, 2, 6, d), lambda i, j, *_: (i, 0, 0, 0)),
                      pl.BlockSpec((1, d), lambda i, j, *_: (0, 0))],
            out_specs=tok(d),
            scratch_shapes=[pltpu.VMEM((ne * win, d), BF16),
                            pltpu.VMEM((tt, d), F32),
                            pltpu.SemaphoreType.DMA((ne,))]),
        compiler_params=_cparams("arbitrary", "arbitrary"),
        name="moe_combine",
    )(start, npass, jnp.swapaxes(pos_t, 1, 2), jnp.swapaxes(gate_t, 1, 2), yout, xa, modtab,
      g_post.reshape(1, d))


def _ec_moe(logits_t, h, xa, modtab, g_post, w_gate, w_up, w_down, layer, ctx_len):
    b, ne, ta = logits_t.shape
    d = h.shape[-1]
    cap_ctx = max(1, EC_CAPACITY * ctx_len // ne)
    cap_lat = max(1, EC_CAPACITY * (ta - ctx_len) // ne)
    nslot = cap_ctx + cap_lat
    pos_t, gate_t = _moe_select(logits_t, ctx_len, cap_ctx, cap_lat)
    xin = _moe_gather(pos_t, h, nslot)
    pair = 2 if b % 2 == 0 else 1
    yout = _moe_ffn(xin.reshape(ne, b // pair, pair * nslot, d), w_gate, w_up, w_down, layer)
    yout = yout.reshape(ne, b, nslot, d)
    return _moe_combine(pos_t, gate_t, yout, xa, modtab, g_post, ctx_len)


def kernel(x, c, ctx, c_ctx, w_mod, b_mod, g_mix_pre, g_mix_post, g_ffn_pre, g_ffn_post, w_router, w_exp_gate, w_exp_up, w_exp_down, ev_w_in, ev_w_out, hgrn_lb, hgrn_g_norm, na_rpb, od_w_in, od_w_out, q_norm, k_norm, s5_a_re, s5_a_im, s5_log_dt, s5_b_re, s5_b_im, s5_c_re, s5_c_im, s5_d, s5_w_glu, s5_b_glu):
    b, t, d = x.shape
    ctx_len = ctx.shape[1]
    depth = w_mod.shape[0]
    assert depth == 2 and b <= 7
    ta = ctx_len + t
    a_width = d // 2
    s5_width = d // 4
    cq_width = d - s5_width
    ckv_width = cq_width // 3

    xa = jnp.concatenate([ctx, x], axis=1)
    cc = jnp.concatenate([c, c_ctx[None], jnp.zeros((7 - b, d), F32)], axis=0)
    mod = _modulation(cc, w_mod, b_mod)
    mod_lat = mod[:, :b].reshape(depth, b, 1, 6, d)
    mod_ctx = jnp.broadcast_to(mod[:, b].reshape(depth, 1, 1, 6, d), (depth, b, 1, 6, d))
    modtab = jnp.concatenate([mod_ctx, mod_lat], axis=2)

    lb_all = jnp.cumsum(jax.nn.softmax(hgrn_lb.astype(F32), axis=0), axis=0)
    hx = _prenorm(xa, g_mix_pre[0], modtab[0])
    p = _matmul(hx.reshape(b * ta, d), ev_w_in[0]).reshape(b, ta, -1)
    o_f, o_b = _hgrn(p, lb_all[0], ctx_len, a_width)
    mix_a = _hgrn_readout(o_f, o_b, p, hgrn_g_norm[0], a_width)
    mix_b = _natten(p, _natten_bias(na_rpb[0]), ctx_len, 5 * a_width, d - a_width)
    xa, h2, logits = _postmix(mix_a, mix_b, ev_w_out[0], xa, modtab[0], g_mix_post[0], g_ffn_pre[0], w_router[0])
    xa = _ec_moe(jnp.swapaxes(logits, 1, 2), h2, xa, modtab[0], g_ffn_post[0], w_exp_gate, w_exp_up, w_exp_down, 0, ctx_len)

    hx = _prenorm(xa, g_mix_pre[1], modtab[1])
    p = _matmul(hx.reshape(b * ta, d), od_w_in[0]).reshape(b, ta, -1)
    cosf, sinf = _rope_tables(ctx_len, t)
    qn, kn, vb = _qkprep(p, cosf, sinf, q_norm[0], k_norm[0], cq_width, ckv_width)
    mix_a = _gqa_attention(qn, kn, vb, cq_width // ckv_width)
    ops = _s5_operators(s5_a_re[0], s5_a_im[0], s5_log_dt[0], s5_b_re[0], s5_b_im[0], s5_c_re[0], s5_c_im[0])
    mix_b = _s5(p, cq_width + 2 * ckv_width, s5_width, ctx_len, ops, s5_d[0], s5_w_glu[0], s5_b_glu[0])
    xa, h2, logits = _postmix(mix_a, mix_b, od_w_out[0], xa, modtab[1], g_mix_post[1], g_ffn_pre[1], w_router[1])
    xa = _ec_moe(jnp.swapaxes(logits, 1, 2), h2, xa, modtab[1], g_ffn_post[1], w_exp_gate, w_exp_up, w_exp_down, 1, ctx_len)
    return xa[:, ctx_len:]
```

```python
import functools
import math

import jax
import jax.numpy as jnp
from jax import lax
from jax.experimental import pallas as pl
from jax.experimental.pallas import tpu as pltpu

F32 = jnp.float32
BF16 = jnp.bfloat16
HIGHEST = lax.Precision.HIGHEST
EPS = 1e-6

LANE = 128
TOK_TILE = 256
VMEM_LIMIT = 52 << 20

GRID_W = 64
NA_ROWS = 8
NA_COLS = 16
NA_GROUP = 4
NA_SPAN = 12
HEAD = 128
A_CHUNK = 64
N_EXPERTS = 16
EC_CAPACITY = 2
S5_GROUP = 16
S5_STATE = 64
S5_CHUNK = 16
COMBINE_WIN = 128
GATHER_WIN = 64
ROPE_THETA = 10000.0
NEG_BIG = -1e30
LOG2E = 1.4426950408889634


def _cparams(*sem):
    return pltpu.CompilerParams(dimension_semantics=sem, vmem_limit_bytes=VMEM_LIMIT)


def _dot(a, b):
    return jnp.dot(a, b, preferred_element_type=F32)


def _dot_nt(a, b):
    return lax.dot_general(a, b, (((1,), (1,)), ((), ())), preferred_element_type=F32)


def _dot_tn(a, b):
    return lax.dot_general(a, b, (((0,), (0,)), ((), ())), preferred_element_type=F32)


def _rms(x):
    return x * lax.rsqrt(jnp.mean(x * x, axis=-1, keepdims=True) + EPS)


def _silu(x):
    return x * jax.nn.sigmoid(x)


def _mod_kernel(c_ref, w_ref, b_ref, o_ref):
    o_ref[...] = jnp.dot(_silu(c_ref[...]), w_ref[...], preferred_element_type=F32, precision=HIGHEST) + b_ref[...]


def _modulation(cc, w_mod, b_mod):
    depth, d, n = w_mod.shape
    tn = 1024
    return pl.pallas_call(
        _mod_kernel,
        out_shape=jax.ShapeDtypeStruct((depth, cc.shape[0], n), F32),
        grid=(depth, n // tn),
        in_specs=[pl.BlockSpec(cc.shape, lambda l, j: (0, 0)),
                  pl.BlockSpec((None, d, tn), lambda l, j: (l, 0, j)),
                  pl.BlockSpec((None, 1, tn), lambda l, j: (l, 0, j))],
        out_specs=pl.BlockSpec((None, cc.shape[0], tn), lambda l, j: (l, 0, j)),
        compiler_params=_cparams("arbitrary", "arbitrary"),
        name="modulation",
    )(cc, w_mod, b_mod.reshape(depth, 1, n))


def _mod_spec(d):
    return pl.BlockSpec((None, None, 6, d), lambda b, j: (b, jnp.minimum(j, 1), 0, 0))


def _stream_operands(xs):
    if not isinstance(xs, tuple):
        b, ta, d = xs.shape
        return (b, ta, d), [pl.BlockSpec((None, TOK_TILE, d), lambda i, j: (i, j, 0))], [xs]
    ctx, x = xs
    b, t, d = x.shape
    assert ctx.shape[1] == TOK_TILE
    return (b, TOK_TILE + t, d), [pl.BlockSpec((None, TOK_TILE, d), lambda i, j: (i, 0, 0)),
                                  pl.BlockSpec((None, TOK_TILE, d), lambda i, j: (i, jnp.maximum(j - 1, 0), 0))], [ctx, x]


def _stream_tile(refs):
    if len(refs) == 1:
        return refs[0][...]
    return jnp.where(pl.program_id(1) == 0, refs[0][...], refs[1][...])


def _prenorm_kernel(*refs, nx):
    g_ref, m_ref, o_ref = refs[nx:]
    m = m_ref[...]
    y = _rms(_stream_tile(refs[:nx])) * g_ref[...]
    o_ref[...] = (y * (1.0 + m[1:2]) + m[0:1]).astype(BF16)


def _prenorm(xs, g, modtab):
    (b, ta, d), x_specs, x_args = _stream_operands(xs)
    return pl.pallas_call(
        functools.partial(_prenorm_kernel, nx=len(x_args)),
        out_shape=jax.ShapeDtypeStruct((b, ta, d), BF16),
        grid=(b, ta // TOK_TILE),
        in_specs=x_specs + [pl.BlockSpec((1, d), lambda i, j: (0, 0)), _mod_spec(d)],
        out_specs=pl.BlockSpec((None, TOK_TILE, d), lambda i, j: (i, j, 0)),
        compiler_params=_cparams("parallel", "parallel"),
        name="prenorm",
    )(*x_args, g.reshape(1, d), modtab)


def _mm_kernel(a_ref, w_ref, o_ref, wb_ref):
    @pl.when(pl.program_id(1) == 0)
    def _():
        wb_ref[...] = w_ref[...].astype(BF16)

    o_ref[...] = _dot(a_ref[...], wb_ref[...]).astype(o_ref.dtype)


def _matmul(a, w, tm=512, tn=1024, out_dtype=F32):
    m, k = a.shape
    n = w.shape[1]
    return pl.pallas_call(
        _mm_kernel,
        out_shape=jax.ShapeDtypeStruct((m, n), out_dtype),
        grid=(n // tn, m // tm),
        in_specs=[pl.BlockSpec((tm, k), lambda j, i: (i, 0)),
                  pl.BlockSpec((k, tn), lambda j, i: (0, j))],
        out_specs=pl.BlockSpec((tm, tn), lambda j, i: (i, j)),
        scratch_shapes=[pltpu.VMEM((k, tn), BF16)],
        compiler_params=_cparams("arbitrary", "arbitrary"),
        name="proj_in",
    )(a, w)


def _hgrn_masks(chunk):
    row = lax.broadcasted_iota(jnp.int32, (chunk, LANE), 0)
    ti = lax.broadcasted_iota(jnp.int32, (chunk, chunk), 0)
    si = lax.broadcasted_iota(jnp.int32, (chunk, chunk), 1)
    levels = [(((row >> lvl) & 1) == 1, (ti >> (lvl + 1)) == (si >> (lvl + 1))) for lvl in range(chunk.bit_length() - 1)]
    return ti == si, levels


def _hgrn_chunk(q, fr, v, lb, st, rev, masks):
    f = lb + (1.0 - lb) * jax.nn.sigmoid(fr)
    kk = 1.0 - f
    lf = jnp.log(f)
    chunk = q.shape[0]
    diag, levels = masks
    att = jnp.where(diag, _dot_nt(q.astype(BF16), kk.astype(BF16)), 0.0)
    p_in = lf
    r_ex = jnp.zeros_like(lf)
    tot = lf
    for lvl, (bit, same) in enumerate(levels):
        step = 1 << lvl
        up = pltpu.roll(tot, step, 0)
        dn = pltpu.roll(tot, chunk - step, 0)
        is_q = jnp.logical_not(bit) if rev else bit
        qf = jnp.where(is_q, jnp.exp(p_in) * q, 0.0).astype(BF16)
        kf = jnp.where(is_q, 0.0, jnp.exp(r_ex) * kk).astype(BF16)
        att = att + jnp.where(same, _dot_nt(qf, kf), 0.0)
        if rev:
            p_in = p_in + jnp.where(bit, 0.0, dn)
            r_ex = r_ex + jnp.where(bit, up, 0.0)
        else:
            p_in = p_in + jnp.where(bit, up, 0.0)
            r_ex = r_ex + jnp.where(bit, 0.0, dn)
        tot = tot + jnp.where(bit, up, dn)
    vb = v.astype(BF16)
    o = _dot_nt((q * jnp.exp(p_in)).astype(BF16), st.astype(BF16)) + _dot(att.astype(BF16), vb)
    kd = (kk * jnp.exp(r_ex)).astype(BF16)
    st_new = st * jnp.exp(tot[0:1]) + _dot_tn(vb, kd)
    return o, st_new


def _hgrn_kernel(qf_ref, ff_ref, vf_ref, qb_ref, fb_ref, vb_ref, lb_ref, of_ref, ob_ref, st_ref, *, chunk, hb):
    @pl.when(pl.program_id(2) == 0)
    def _():
        st_ref[...] = jnp.zeros_like(st_ref)

    masks = _hgrn_masks(chunk)
    for h in range(hb):
        sl = slice(h * HEAD, (h + 1) * HEAD)
        o, st = _hgrn_chunk(qf_ref[:, sl], ff_ref[:, sl], vf_ref[:, sl], lb_ref[0:1, sl], st_ref[0, h], False, masks)
        of_ref[:, sl] = o
        st_ref[0, h] = st
        o, st = _hgrn_chunk(qb_ref[:, sl], fb_ref[:, sl], vb_ref[:, sl], lb_ref[1:2, sl], st_ref[1, h], True, masks)
        ob_ref[:, sl] = o
        st_ref[1, h] = st


def _hgrn(p, lb, ctx_len, width, hb=4):
    b, ta, _ = p.shape
    chunk = A_CHUNK
    n = ta // chunk
    nc = ctx_len // chunk
    cw = HEAD * hb
    nh = width // cw
    bwd = lambda j: jnp.where(j < nc, nc - 1 - j, n - 1 + nc - j)

    def spec(base, rev):
        return pl.BlockSpec((None, chunk, cw), lambda i, h, j: (i, bwd(j) if rev else j, base // cw + h))

    out = jax.ShapeDtypeStruct((b, ta, width), F32)
    return pl.pallas_call(
        functools.partial(_hgrn_kernel, chunk=chunk, hb=hb),
        out_shape=(out, out),
        grid=(b, nh, n),
        in_specs=[spec(0, False), spec(width, False), spec(3 * width, False),
                  spec(0, True), spec(2 * width, True), spec(3 * width, True),
                  pl.BlockSpec((2, cw), lambda i, h, j: (0, h))],
        out_specs=(spec(0, False), spec(0, True)),
        scratch_shapes=[pltpu.VMEM((2, hb, HEAD, HEAD), F32)],
        compiler_params=_cparams("parallel", "parallel", "arbitrary"),
        name="hgrn_scan",
    )(p, p, p, p, p, p, lb)


def _hgrn_readout_kernel(of_ref, ob_ref, g_ref, gn_ref, o_ref):
    for h in range(o_ref.shape[-1] // HEAD):
        sl = slice(h * HEAD, (h + 1) * HEAD)
        o = of_ref[:, sl] + ob_ref[:, sl]
        o_ref[:, sl] = (_rms(o) * gn_ref[...] * _silu(g_ref[:, sl])).astype(BF16)


def _hgrn_readout(o_f, o_b, p, g_norm, width):
    b, ta, _ = o_f.shape
    spec = pl.BlockSpec((None, TOK_TILE, width), lambda i, j: (i, j, 0))
    return pl.pallas_call(
        _hgrn_readout_kernel,
        out_shape=jax.ShapeDtypeStruct((b, ta, width), BF16),
        grid=(b, ta // TOK_TILE),
        in_specs=[spec, spec,
                  pl.BlockSpec((None, TOK_TILE, width), lambda i, j: (i, j, 4)),
                  pl.BlockSpec((1, HEAD), lambda i, j: (0, 0))],
        out_specs=spec,
        compiler_params=_cparams("parallel", "parallel"),
        name="hgrn_readout",
    )(o_f, o_b, p, g_norm.reshape(1, HEAD))


def _natten_bias(rpb):
    nrow, ncol = 2 * NA_ROWS - 1, 2 * NA_COLS - 1
    col = jnp.arange(GRID_W)
    col_start = jnp.clip(col - NA_COLS // 2, 0, GRID_W - NA_COLS)
    cmask = (col[None, :] >= col_start[:, None]) & (col[None, :] < col_start[:, None] + NA_COLS)
    dc = col[None, :] - col[:, None] + NA_COLS - 1
    cm = (cmask[:, :, None] & (dc[:, :, None] == jnp.arange(ncol))).astype(F32)
    cls = jnp.arange(3)[:, None, None]
    i = jnp.arange(NA_GROUP)[None, :, None]
    w = jnp.arange(NA_SPAN)[None, None, :]
    first_w = jnp.where(cls == 0, 0, jnp.where(cls == 1, i, NA_SPAN - NA_ROWS))
    valid = (w >= first_w) & (w < first_w + NA_ROWS)
    span0 = jnp.where(cls == 0, 0, jnp.where(cls == 1, -(NA_ROWS // 2), NA_GROUP - NA_SPAN))
    dr = span0 + w - i + NA_ROWS - 1
    rm = (valid[..., None] & (dr[..., None] == jnp.arange(nrow))).astype(F32)
    t = jnp.einsum('qkb,hab->haqk', cm, rpb.astype(F32), precision=HIGHEST)
    t = jnp.einsum('ciwa,haqk->hciqwk', rm, t, precision=HIGHEST)
    ok = valid[None, :, :, None, :, None] & cmask[None, None, None, :, None, :]
    t = jnp.where(ok, t, NEG_BIG)
    return t.reshape(rpb.shape[0], 3, NA_GROUP * GRID_W, NA_SPAN * GRID_W)


def _natten_kernel(q_ref, k_ref, v_ref, bias_ref, o_ref, *, ctx_len, rows):
    j = pl.program_id(2)
    nst = rows // NA_GROUP
    scale = HEAD ** -0.5
    heads = [slice(h * HEAD, (h + 1) * HEAD) for h in range(q_ref.shape[-1] // HEAD)]

    def ctx_scores(sl):
        qb = q_ref[:, sl].astype(BF16)
        vc = v_ref[0:ctx_len, sl].astype(BF16)
        s_ctx = _dot_nt(qb, k_ref[0:ctx_len, sl].astype(BF16)) * scale
        return qb, vc, s_ctx, jnp.max(s_ctx, axis=1, keepdims=True)

    @pl.when(j < nst)
    def _():
        last = j == nst - 1
        cls = jnp.where(j == 0, 0, jnp.where(last, 2, 1))
        span_row = jnp.where(j == 0, 0, jnp.where(last, rows - NA_SPAN, NA_GROUP * j - NA_ROWS // 2))
        start = pl.multiple_of(ctx_len + span_row * GRID_W, GRID_W)
        for h, sl in enumerate(heads):
            qb, vc, s_ctx, m_ctx = ctx_scores(sl)
            kw = k_ref[pl.ds(start, NA_SPAN * GRID_W), sl].astype(BF16)
            vw = v_ref[pl.ds(start, NA_SPAN * GRID_W), sl].astype(BF16)
            s_win = _dot_nt(qb, kw) * scale + bias_ref[h, cls]
            m = jnp.maximum(m_ctx, jnp.max(s_win, axis=1, keepdims=True))
            pw = jnp.exp(s_win - m)
            pc = jnp.exp(s_ctx - m)
            den = jnp.sum(pw, axis=1, keepdims=True) + jnp.sum(pc, axis=1, keepdims=True)
            o = _dot(pw.astype(BF16), vw) + _dot(pc.astype(BF16), vc)
            o_ref[:, sl] = (o / den).astype(BF16)

    @pl.when(j >= nst)
    def _():
        for sl in heads:
            _, vc, s_ctx, m_ctx = ctx_scores(sl)
            pc = jnp.exp(s_ctx - m_ctx)
            o = _dot(pc.astype(BF16), vc) / jnp.sum(pc, axis=1, keepdims=True)
            o_ref[:, sl] = o.astype(BF16)


def _natten(p, bias, ctx_len, col0, width, hb=2):
    b, ta, _ = p.shape
    cw = HEAD * hb
    nh = width // cw
    rows = (ta - ctx_len) // GRID_W
    tq = NA_GROUP * GRID_W
    nst = rows // NA_GROUP
    ncq = ctx_len // tq
    assert rows % NA_GROUP == 0 and rows >= NA_SPAN and ctx_len % tq == 0

    def qmap(cb):
        return lambda i, h, j: (i, jnp.where(j < nst, ncq + j, j - nst), cb + h)

    return pl.pallas_call(
        functools.partial(_natten_kernel, ctx_len=ctx_len, rows=rows),
        out_shape=jax.ShapeDtypeStruct((b, ta, width), BF16),
        grid=(b, nh, nst + ncq),
        in_specs=[pl.BlockSpec((None, tq, cw), qmap(col0 // cw)),
                  pl.BlockSpec((None, ta, cw), lambda i, h, j: (i, 0, (col0 + width) // cw + h)),
                  pl.BlockSpec((None, ta, cw), lambda i, h, j: (i, 0, (col0 + 2 * width) // cw + h)),
                  pl.BlockSpec((hb, 3, tq, NA_SPAN * GRID_W), lambda i, h, j: (h, 0, 0, 0))],
        out_specs=pl.BlockSpec((None, tq, cw), qmap(0)),
        compiler_params=_cparams("parallel", "parallel", "arbitrary"),
        name="natten",
    )(p, p, p, bias)


def _rope_tables(ctx_len, t):
    pos = jnp.arange(t)
    row = (pos // GRID_W).astype(F32)
    col = (pos % GRID_W).astype(F32)
    half = HEAD // 2
    inv = ROPE_THETA ** (-jnp.arange(0, half, 2, dtype=F32) / half)
    ang = jnp.concatenate([row[:, None] * inv, col[:, None] * inv], axis=-1)
    cos, sin = jnp.cos(ang), jnp.sin(ang)
    cosf = jnp.repeat(cos, 2, axis=-1)
    sinf = jnp.stack([-sin, sin], axis=-1).reshape(t, HEAD)
    cosf = jnp.concatenate([jnp.ones((ctx_len, HEAD), F32), cosf], axis=0)
    sinf = jnp.concatenate([jnp.zeros((ctx_len, HEAD), F32), sinf], axis=0)
    return cosf, sinf


def _qkprep_kernel(q_ref, k_ref, v_ref, cos_ref, sin_ref, qn_ref, kn_ref, qo_ref, ko_ref, vo_ref):
    cosf = cos_ref[...]
    sinf = sin_ref[...]
    even = (lax.broadcasted_iota(jnp.int32, cosf.shape, 1) & 1) == 0

    def rope(x):
        partner = jnp.where(even, pltpu.roll(x, LANE - 1, 1), pltpu.roll(x, 1, 1))
        return x * cosf + partner * sinf

    for h in range(q_ref.shape[-1] // HEAD):
        sl = slice(h * HEAD, (h + 1) * HEAD)
        qo_ref[:, sl] = (rope(_rms(q_ref[:, sl]) * qn_ref[...]) * (HEAD ** -0.5 * LOG2E)).astype(BF16)
    for h in range(k_ref.shape[-1] // HEAD):
        sl = slice(h * HEAD, (h + 1) * HEAD)
        ko_ref[:, sl] = rope(_rms(k_ref[:, sl]) * kn_ref[...]).astype(BF16)
    vo_ref[...] = v_ref[...].astype(BF16)


def _qkprep(p, cosf, sinf, q_norm, k_norm, qw, kw):
    b, ta, _ = p.shape
    tt = TOK_TILE

    def spec(w, cb):
        return pl.BlockSpec((None, tt, w), lambda i, j: (i, j, cb))

    return pl.pallas_call(
        _qkprep_kernel,
        out_shape=(jax.ShapeDtypeStruct((b, ta, qw), BF16),
                   jax.ShapeDtypeStruct((b, ta, kw), BF16),
                   jax.ShapeDtypeStruct((b, ta, kw), BF16)),
        grid=(b, ta // tt),
        in_specs=[spec(qw, 0), spec(kw, qw // kw), spec(kw, qw // kw + 1),
                  pl.BlockSpec((tt, HEAD), lambda i, j: (j, 0)),
                  pl.BlockSpec((tt, HEAD), lambda i, j: (j, 0)),
                  pl.BlockSpec((1, HEAD), lambda i, j: (0, 0)),
                  pl.BlockSpec((1, HEAD), lambda i, j: (0, 0))],
        out_specs=(spec(qw, 0), spec(kw, 0), spec(kw, 0)),
        compiler_params=_cparams("parallel", "parallel"),
        name="qk_prep",
    )(p, p, p, cosf, sinf, q_norm.reshape(1, HEAD), k_norm.reshape(1, HEAD))


def _attn_kernel(q_ref, k_ref, v_ref, o_ref, *, skip_tiles):
    @pl.when(pl.program_id(2) < skip_tiles)
    def _():
        o_ref[...] = jnp.zeros_like(o_ref)

    @pl.when(pl.program_id(2) >= skip_tiles)
    def _():
        k = k_ref[...]
        v = v_ref[...]
        for g in range(q_ref.shape[-1] // HEAD):
            sl = slice(g * HEAD, (g + 1) * HEAD)
            s = _dot_nt(q_ref[:, sl], k)
            p = jnp.exp2(s - jnp.max(s, axis=1, keepdims=True))
            o = _dot(p.astype(BF16), v) / jnp.sum(p, axis=1, keepdims=True)
            o_ref[:, sl] = o.astype(BF16)


def _gqa_attention(qn, kn, vb, group, skip_tiles):
    b, ta, qw = qn.shape
    nkv = kn.shape[-1] // HEAD
    tq = TOK_TILE
    gw = group * HEAD
    return pl.pallas_call(
        functools.partial(_attn_kernel, skip_tiles=skip_tiles),
        out_shape=jax.ShapeDtypeStruct((b, ta, qw), BF16),
        grid=(b, nkv, ta // tq),
        in_specs=[pl.BlockSpec((None, tq, gw), lambda i, h, j: (i, j, h)),
                  pl.BlockSpec((None, ta, HEAD), lambda i, h, j: (i, 0, h)),
                  pl.BlockSpec((None, ta, HEAD), lambda i, h, j: (i, 0, h))],
        out_specs=pl.BlockSpec((None, tq, gw), lambda i, h, j: (i, j, h)),
        compiler_params=_cparams("parallel", "parallel", "parallel"),
        name="gqa_attention",
    )(qn, kn, vb)


def _s5_operators(a_re, a_im, log_dt, b_re, b_im, c_re, c_im):
    lc, pp, ns = S5_CHUNK, S5_GROUP, S5_STATE
    a_c = lax.complex(a_re.astype(F32), a_im.astype(F32))
    adt = a_c * jnp.exp(log_dt.astype(F32))[..., None]
    a_bar = jnp.exp(adt)
    b_bar = ((a_bar - 1.0) / a_c)[..., None] * lax.complex(b_re.astype(F32), b_im.astype(F32))
    c_mat = lax.complex(c_re.astype(F32), c_im.astype(F32))
    nd, g = a_re.shape[0], a_re.shape[1]
    pos = jnp.arange(lc)
    lag = pos[None, :] - pos[:, None]
    lag = jnp.stack([lag, -lag])
    live = (lag >= 0)[:, None, :, :, None]
    apl = jnp.where(live, jnp.exp(adt[:, :, None, None, :] * jnp.maximum(lag, 0).astype(F32)[:, None, :, :, None]), 0.0)
    tm = jnp.real(jnp.einsum('dgpn,dgion,dgnq->dgiqop', c_mat, apl, b_bar, precision=HIGHEST))
    tmat = tm.reshape(nd, g, lc * pp, lc * pp)
    steps_after = jnp.stack([lc - 1 - pos, pos]).astype(F32)
    gm = jnp.exp(adt[:, :, None, :] * steps_after[:, None, :, None])[:, :, :, None, :] * jnp.swapaxes(b_bar, 2, 3)[:, :, None]
    gm = gm.reshape(nd, g, lc * pp, ns)
    steps_upto = jnp.stack([pos + 1, lc - pos]).astype(F32)
    hm = c_mat[:, :, None] * jnp.exp(adt[:, :, None, :] * steps_upto[:, None, :, None])[:, :, :, None, :]
    hm = jnp.transpose(hm, (0, 1, 4, 2, 3)).reshape(nd, g, ns, lc * pp)
    hmat = jnp.concatenate([jnp.real(hm), -jnp.imag(hm)], axis=2)
    al = jnp.exp(adt * float(lc))
    al = al[:, :, None, :]
    gd = lambda x: jnp.swapaxes(x, 0, 1)
    return (gd(tmat).astype(BF16), gd(jnp.real(gm)).astype(BF16), gd(jnp.imag(gm)).astype(BF16),
            gd(hmat).astype(BF16), gd(jnp.real(al)), gd(jnp.imag(al)))


def _s5_kernel(u_ref, t_ref, gr_ref, gi_ref, h_ref, ar_ref, ai_ref, d_ref, y_ref, er_ref, ei_ref, xr_ref, xi_ref,
               *, nchunk, nctx, rpc):
    ns = S5_STATE
    u = u_ref[...]
    for dr in range(2):
        er_ref[dr] = _dot(u, gr_ref[dr])
        ei_ref[dr] = _dot(u, gi_ref[dr])

    coef = [(jnp.broadcast_to(ar_ref[dr], (rpc, ns)), jnp.broadcast_to(ai_ref[dr], (rpc, ns))) for dr in range(2)]

    def body(c, carry):
        out = []
        for dr in range(2):
            xr, xi = carry[2 * dr], carry[2 * dr + 1]
            pc = c if dr == 0 else jnp.where(c < nctx, nctx - 1 - c, nchunk - 1 + nctx - c)
            off = pl.multiple_of(pc * rpc, rpc)
            xr_ref[dr, pl.ds(off, rpc), :] = xr
            xi_ref[dr, pl.ds(off, rpc), :] = xi
            ar, ai = coef[dr]
            out.append(ar * xr - ai * xi + er_ref[dr, pl.ds(off, rpc), :])
            out.append(ar * xi + ai * xr + ei_ref[dr, pl.ds(off, rpc), :])
        return tuple(out)

    zero = jnp.zeros((rpc, ns), F32)
    lax.fori_loop(0, nchunk, body, (zero, zero, zero, zero), unroll=4)
    y = d_ref[...] * u.astype(F32)
    for dr in range(2):
        y = y + _dot(u, t_ref[dr])
        y = y + _dot(xr_ref[dr].astype(BF16), h_ref[dr, 0:ns, :]) + _dot(xi_ref[dr].astype(BF16), h_ref[dr, ns:2 * ns, :])
    y_ref[...] = y.astype(BF16)


def _s5_scan(u, ops, d_skip, ctx_len):
    tmat, g_re, g_im, hmat, a_re, a_im = ops
    b, ta, w = u.shape
    lc, pp, ns = S5_CHUNK, S5_GROUP, S5_STATE
    g = w // pp
    nchunk = ta // lc
    rpc = 8
    cw = lc * pp
    rows = nchunk * rpc
    ug = jnp.transpose(u.astype(BF16).reshape(b, nchunk, lc, g, pp), (3, 1, 0, 2, 4))
    ug = jnp.pad(ug, ((0, 0), (0, 0), (0, rpc - b), (0, 0), (0, 0))).reshape(g, rows, cw)
    dvec = jnp.tile(d_skip.astype(F32).reshape(g, 1, pp), (1, lc, 1)).reshape(g, 1, cw)
    op = lambda r, c: pl.BlockSpec((None, 2, r, c), lambda i: (i, 0, 0, 0))
    y = pl.pallas_call(
        functools.partial(_s5_kernel, nchunk=nchunk, nctx=ctx_len // lc, rpc=rpc),
        out_shape=jax.ShapeDtypeStruct((g, rows, cw), BF16),
        grid=(g,),
        in_specs=[pl.BlockSpec((None, rows, cw), lambda i: (i, 0, 0)),
                  op(cw, cw), op(cw, ns), op(cw, ns), op(2 * ns, cw), op(1, ns), op(1, ns),
                  pl.BlockSpec((None, 1, cw), lambda i: (i, 0, 0))],
        out_specs=pl.BlockSpec((None, rows, cw), lambda i: (i, 0, 0)),
        scratch_shapes=[pltpu.VMEM((2, rows, ns), F32) for _ in range(4)],
        compiler_params=_cparams("parallel"),
        name="s5_scan",
    )(ug, tmat, g_re, g_im, hmat, a_re, a_im, dvec)
    y = y.reshape(g, nchunk, rpc, lc, pp)[:, :, :b]
    return jnp.transpose(y, (2, 1, 3, 0, 4)).reshape(b, ta, w)


def _s5_glu_kernel(y_ref, w_ref, b_ref, o_ref):
    y = y_ref[...].astype(F32)
    y = 0.5 * y * (1.0 + jnp.tanh(math.sqrt(2.0 / math.pi) * (y + 0.044715 * (y * y * y))))
    z = _dot(y.astype(BF16), w_ref[...]) + b_ref[...]
    o_ref[...] = (y * jax.nn.sigmoid(z)).astype(BF16)


def _s5_glu(y, w_glu, b_glu):
    b, ta, w = y.shape
    spec = pl.BlockSpec((None, TOK_TILE, w), lambda i, j: (i, j, 0))
    return pl.pallas_call(
        _s5_glu_kernel,
        out_shape=jax.ShapeDtypeStruct((b, ta, w), BF16),
        grid=(b, ta // TOK_TILE),
        in_specs=[spec, pl.BlockSpec((w, w), lambda i, j: (0, 0)), pl.BlockSpec((1, w), lambda i, j: (0, 0))],
        out_specs=spec,
        compiler_params=_cparams("parallel", "parallel"),
        name="s5_glu",
    )(y, w_glu.astype(BF16), b_glu.reshape(1, w))


def _s5(p, ucol, width, ctx_len, ops, d_skip, w_glu, b_glu):
    y = _s5_scan(p[..., ucol:ucol + width], ops, d_skip, ctx_len)
    return _s5_glu(y, w_glu, b_glu)


def _postmix_kernel(a_ref, b_ref, w_ref, *refs, nx):
    m_ref, gpost_ref, gpre_ref, wr_ref, xo_ref, h_ref, lg_ref = refs[nx:]
    wa = a_ref.shape[-1]
    y = _dot(a_ref[...], w_ref[0:wa, :]) + _dot(b_ref[...], w_ref[wa:, :])
    m = m_ref[...]
    xn = _stream_tile(refs[:nx]) + m[2:3] * (_rms(y) * gpost_ref[...])
    xo_ref[...] = xn
    h2 = _rms(xn) * gpre_ref[...] * (1.0 + m[4:5]) + m[3:4]
    hi = h2.astype(BF16)
    h_ref[...] = hi
    lo = (h2 - hi.astype(F32)).astype(BF16)
    ne = lg_ref.shape[-1]
    both = _dot(hi, wr_ref[...])
    lg_ref[...] = both[:, 0:ne] + both[:, ne:2 * ne] + _dot(lo, wr_ref[:, 0:ne])


def _postmix(mix_a, mix_b, w_out, xs, modtab, g_post, g_pre, w_router):
    (b, ta, d), x_specs, x_args = _stream_operands(xs)
    wa, wb = mix_a.shape[-1], mix_b.shape[-1]
    ne = w_router.shape[-1]
    tok = lambda w: pl.BlockSpec((None, TOK_TILE, w), lambda i, j: (i, j, 0))
    vec = pl.BlockSpec((1, d), lambda i, j: (0, 0))
    wr_hi = w_router.astype(BF16)
    wr2 = jnp.concatenate([wr_hi, (w_router - wr_hi.astype(F32)).astype(BF16)], axis=1)
    return pl.pallas_call(
        functools.partial(_postmix_kernel, nx=len(x_args)),
        out_shape=(jax.ShapeDtypeStruct((b, ta, d), F32),
                   jax.ShapeDtypeStruct((b, ta, d), BF16),
                   jax.ShapeDtypeStruct((b, ta, ne), F32)),
        grid=(b, ta // TOK_TILE),
        in_specs=[tok(wa), tok(wb),
                  pl.BlockSpec((wa + wb, d), lambda i, j: (0, 0))] + x_specs + [
                  _mod_spec(d), vec, vec,
                  pl.BlockSpec((d, 2 * ne), lambda i, j: (0, 0))],
        out_specs=(tok(d), tok(d), tok(ne)),
        compiler_params=_cparams("parallel", "parallel"),
        name="mix_out",
    )(mix_a, mix_b, w_out.astype(BF16), *x_args, modtab, g_post.reshape(1, d), g_pre.reshape(1, d), wr2)


def _select_kernel(lg_ref, pos_ref, gate_ref, *, ctx_len, cap_ctx, cap_lat):
    lg = lg_ref[...]
    ne, ta = lg.shape
    ex = jnp.exp(lg - jnp.max(lg, axis=0, keepdims=True))
    probs = ex / jnp.sum(ex, axis=0, keepdims=True)
    bits = pltpu.bitcast(probs, jnp.int32)
    is_ctx = lax.broadcasted_iota(jnp.int32, (ne, ta), 1) < ctx_len

    def counts(mask):
        mf = jnp.where(mask, 1.0, 0.0)
        return (jnp.sum(jnp.where(is_ctx, mf, 0.0), axis=1, keepdims=True),
                jnp.sum(jnp.where(is_ctx, 0.0, mf), axis=1, keepdims=True))

    def search(i, carry):
        pc, pt = carry
        bit = jnp.left_shift(jnp.int32(1), 30 - i)
        cc, ct = counts(bits >= jnp.where(is_ctx, pc | bit, pt | bit))
        return jnp.where(cc >= cap_ctx, pc | bit, pc), jnp.where(ct >= cap_lat, pt | bit, pt)

    z = jnp.zeros((ne, 1), jnp.int32)
    pc, pt = lax.fori_loop(0, 31, search, (z, z))
    thr = jnp.where(is_ctx, pc, pt)
    gt = bits > thr
    eq = bits == thr
    gc, gl = counts(gt)
    need = jnp.where(is_ctx, cap_ctx - gc, cap_lat - gl)

    nb = ta // LANE
    ut = jnp.where(lax.broadcasted_iota(jnp.int32, (LANE, LANE), 0) <= lax.broadcasted_iota(jnp.int32, (LANE, LANE), 1),
                   1.0, 0.0).astype(BF16)

    def lane_prefix(mask):
        mf = jnp.where(mask, 1.0, 0.0).astype(BF16)
        blocks = jnp.concatenate([mf[:, j * LANE:(j + 1) * LANE] for j in range(nb)], axis=0)
        inc = _dot(blocks, ut)
        outs = []
        off = jnp.zeros((ne, 1), F32)
        for j in range(nb):
            if j * LANE == ctx_len:
                off = jnp.zeros((ne, 1), F32)
            blk = inc[j * ne:(j + 1) * ne]
            outs.append(blk + off)
            off = off + blk[:, LANE - 1:LANE]
        return jnp.concatenate(outs, axis=1)

    sel = gt | (eq & (lane_prefix(eq) <= need))
    slot = lane_prefix(sel) - 1.0 + jnp.where(is_ctx, 0.0, float(cap_ctx))
    pos_ref[...] = jnp.where(sel, slot, -1.0)
    gate_ref[...] = jnp.where(sel, probs, 0.0)


def _moe_select(logits_t, ctx_len, cap_ctx, cap_lat):
    b, ne, ta = logits_t.shape
    spec = pl.BlockSpec((None, ne, ta), lambda i: (i, 0, 0))
    return pl.pallas_call(
        functools.partial(_select_kernel, ctx_len=ctx_len, cap_ctx=cap_ctx, cap_lat=cap_lat),
        out_shape=(jax.ShapeDtypeStruct((b, ne, ta), F32), jax.ShapeDtypeStruct((b, ne, ta), F32)),
        grid=(b,),
        in_specs=[spec],
        out_specs=(spec, spec),
        compiler_params=_cparams("parallel"),
        name="moe_select",
    )(logits_t)


def _slot_windows(pos_t, nt, win, nslot):
    b, ne, ta = pos_t.shape
    pt = pos_t.reshape(b, ne, nt, ta // nt)
    hi = jnp.max(pt, axis=-1).astype(jnp.int32) + 1
    lo = jnp.min(jnp.where(pt >= 0, pt, float(nslot)), axis=-1).astype(jnp.int32)
    lo = jnp.where(hi > 0, lo // 16 * 16, 0)
    npass = jnp.maximum(jnp.max((hi - lo + win - 1) // win, axis=1), 1)
    return jnp.transpose(lo, (0, 2, 1)).reshape(-1), npass.reshape(-1)


def _gather_kernel(start_ref, npass_ref, pos_ref, h_ref, o_ref, acc_ref):
    i = pl.program_id(0)
    eg = pl.program_id(1)
    j = pl.program_id(2)
    ge, tk = pos_ref.shape
    ne = pl.num_programs(1) * ge
    nslot = o_ref.shape[1]
    win = GATHER_WIN
    tile = i * pl.num_programs(2) + j

    @pl.when(j == 0)
    def _():
        acc_ref[...] = jnp.zeros_like(acc_ref)

    pos = pos_ref[...].astype(jnp.int32)
    slot = lax.broadcasted_iota(jnp.int32, (win, tk), 0)

    def one_pass(p, carry):
        src, blocks = [], []
        for el in range(ge):
            lo = start_ref[tile * ne + eg * ge + el] + p * win
            src.append(pl.multiple_of(jnp.minimum(lo, nslot - win), 16))
            pe = pos[el:el + 1, :]
            hit = (pe >= lo) & (pe < lo + win) & (slot == pe - src[el])
            blocks.append(jnp.where(hit, 1.0, 0.0).astype(BF16))
        part = _dot(jnp.concatenate(blocks, axis=0), h_ref[...])
        for el in range(ge):
            acc_ref[el, pl.ds(src[el], win), :] += part[el * win:(el + 1) * win]
        return carry

    lax.fori_loop(0, npass_ref[tile], one_pass, 0)

    @pl.when(j == pl.num_programs(2) - 1)
    def _():
        o_ref[...] = acc_ref[...].astype(BF16)


def _moe_gather(pos_t, h, nslot):
    b, ne, ta = pos_t.shape
    d = h.shape[-1]
    ge = 4
    tk = TOK_TILE
    nt = ta // tk
    start, npass = _slot_windows(pos_t, nt, GATHER_WIN, nslot)
    return pl.pallas_call(
        _gather_kernel,
        out_shape=jax.ShapeDtypeStruct((ne, b, nslot, d), BF16),
        grid_spec=pltpu.PrefetchScalarGridSpec(
            num_scalar_prefetch=2,
            grid=(b, ne // ge, nt),
            in_specs=[pl.BlockSpec((None, None, ge, tk), lambda i, g, j, *_: (i, g, 0, j)),
                      pl.BlockSpec((None, tk, d), lambda i, g, j, *_: (i, j, 0))],
            out_specs=pl.BlockSpec((ge, None, nslot, d), lambda i, g, j, *_: (g, i, 0, 0)),
            scratch_shapes=[pltpu.VMEM((ge, nslot, d), F32)]),
        compiler_params=_cparams("arbitrary", "arbitrary", "arbitrary"),
        name="moe_gather",
    )(start, npass, pos_t.reshape(b, ne // ge, ge, ta), h)


def _ffn_kernel(x_ref, wg_ref, wu_ref, wd_ref, o_ref, hid_ref, *, nf):
    s = pl.program_id(2)
    tf = wg_ref.shape[-1]

    @pl.when(s < nf)
    def _():
        x = x_ref[...]
        g = _dot(x, wg_ref[...].astype(BF16))
        u = _dot(x, wu_ref[...].astype(BF16))
        hid_ref[s] = (_silu(g) * u).astype(BF16)

    @pl.when(s >= nf)
    def _():
        acc = _dot(hid_ref[0], wd_ref[0:tf, :].astype(BF16))
        for f in range(1, nf):
            acc = acc + _dot(hid_ref[f], wd_ref[f * tf:(f + 1) * tf, :].astype(BF16))
        o_ref[...] = acc.astype(BF16)


def _moe_ffn(xin, w_gate, w_up, w_down, layer, tf=512):
    ne, ns, m, d = xin.shape
    ff = w_gate.shape[-1]
    nf = ff // tf
    nd = d // tf
    up = lambda e, s, f: (layer, e, 0, jnp.minimum(f, nf - 1))
    return pl.pallas_call(
        functools.partial(_ffn_kernel, nf=nf),
        out_shape=jax.ShapeDtypeStruct((ne, ns, m, d), BF16),
        grid=(ne, ns, nf + nd),
        in_specs=[pl.BlockSpec((None, None, m, d), lambda e, s, f: (e, s, 0, 0)),
                  pl.BlockSpec((None, None, d, tf), up),
                  pl.BlockSpec((None, None, d, tf), up),
                  pl.BlockSpec((None, None, ff, tf), lambda e, s, f: (layer, e, 0, jnp.maximum(f - nf, 0)))],
        out_specs=pl.BlockSpec((None, None, m, tf), lambda e, s, f: (e, s, 0, jnp.maximum(f - nf, 0))),
        scratch_shapes=[pltpu.VMEM((nf, m, tf), BF16)],
        compiler_params=_cparams("parallel", "parallel", "arbitrary"),
        name="moe_ffn",
    )(xin, w_gate, w_up, w_down)


def _combine_kernel(start_ref, npass_ref, pos_ref, gate_ref, y_hbm, x_ref, m_ref, g_ref, o_ref, ybuf, acc_ref, sem,
                    *, ctx_len):
    i = pl.program_id(0)
    j = pl.program_id(1)
    tt, ne = pos_ref.shape
    nslot = y_hbm.shape[2]
    win = COMBINE_WIN
    tile = i * pl.num_programs(1) + j
    pos = pos_ref[...].astype(jnp.int32)
    gate = gate_ref[...]
    lane = lax.broadcasted_iota(jnp.int32, (tt, win), 1)

    def window_copy(e, src):
        return pltpu.make_async_copy(y_hbm.at[e, i, pl.ds(src, win), :], ybuf.at[pl.ds(e * win, win), :], sem.at[e])

    def one_pass(p):
        own_lo, src = [], []
        for e in range(ne):
            lo = start_ref[tile * ne + e] + p * win
            own_lo.append(lo)
            src.append(pl.multiple_of(jnp.minimum(lo, nslot - win), 16))
            window_copy(e, src[e]).start()
        blocks = []
        for e in range(ne):
            pe = pos[:, e:e + 1]
            hit = (pe >= own_lo[e]) & (pe < own_lo[e] + win) & (lane == pe - src[e])
            blocks.append(jnp.where(hit, gate[:, e:e + 1], 0.0).astype(BF16))
        w = jnp.concatenate(blocks, axis=1)
        for e in range(ne):
            window_copy(e, src[e]).wait()
        return _dot(w, ybuf[...])

    acc_ref[...] = one_pass(0)

    def extra(p, carry):
        acc_ref[...] += one_pass(p)
        return carry

    lax.fori_loop(1, npass_ref[tile], extra, 0)
    m = m_ref[...]
    is_ctx = (lax.broadcasted_iota(jnp.int32, (tt, 1), 0) + j * tt) < ctx_len
    mod_gate = jnp.where(is_ctx, m[0, 5:6], m[1, 5:6])
    o_ref[...] = x_ref[...] + mod_gate * (_rms(acc_ref[...]) * g_ref[...])


def _moe_combine(pos_t, gate_t, yout, xa, modtab, g_post, ctx_len):
    b, ta, d = xa.shape
    ne, _, nslot, _ = yout.shape
    nt = 8
    tt = ta // nt
    win = COMBINE_WIN
    start, npass = _slot_windows(pos_t, nt, win, nslot)
    tok = lambda w: pl.BlockSpec((None, tt, w), lambda i, j, *_: (i, j, 0))
    return pl.pallas_call(
        functools.partial(_combine_kernel, ctx_len=ctx_len),
        out_shape=jax.ShapeDtypeStruct((b, ta, d), F32),
        grid_spec=pltpu.PrefetchScalarGridSpec(
            num_scalar_prefetch=2,
            grid=(b, nt),
            in_specs=[tok(ne), tok(ne),
                      pl.BlockSpec(memory_space=pl.ANY),
                      tok(d),
                      pl.BlockSpec((None, 2, 6, d), lambda i, j, *_: (i, 0, 0, 0)),
                      pl.BlockSpec((1, d), lambda i, j, *_: (0, 0))],
            out_specs=tok(d),
            scratch_shapes=[pltpu.VMEM((ne * win, d), BF16),
                            pltpu.VMEM((tt, d), F32),
                            pltpu.SemaphoreType.DMA((ne,))]),
        compiler_params=_cparams("arbitrary", "arbitrary"),
        name="moe_combine",
    )(start, npass, jnp.swapaxes(pos_t, 1, 2), jnp.swapaxes(gate_t, 1, 2), yout, xa, modtab,
      g_post.reshape(1, d))


def _ec_moe(logits_t, h, xa, modtab, g_post, w_gate, w_up, w_down, layer, ctx_len):
    b, ne, ta = logits_t.shape
    d = h.shape[-1]
    cap_ctx = max(1, EC_CAPACITY * ctx_len // ne)
    cap_lat = max(1, EC_CAPACITY * (ta - ctx_len) // ne)
    nslot = cap_ctx + cap_lat
    pos_t, gate_t = _moe_select(logits_t, ctx_len, cap_ctx, cap_lat)
    xin = _moe_gather(pos_t, h, nslot)
    pair = 2 if b % 2 == 0 else 1
    yout = _moe_ffn(xin.reshape(ne, b // pair, pair * nslot, d), w_gate, w_up, w_down, layer)
    yout = yout.reshape(ne, b, nslot, d)
    return _moe_combine(pos_t, gate_t, yout, xa, modtab, g_post, ctx_len)


def kernel(x, c, ctx, c_ctx, w_mod, b_mod, g_mix_pre, g_mix_post, g_ffn_pre, g_ffn_post, w_router, w_exp_gate, w_exp_up, w_exp_down, ev_w_in, ev_w_out, hgrn_lb, hgrn_g_norm, na_rpb, od_w_in, od_w_out, q_norm, k_norm, s5_a_re, s5_a_im, s5_log_dt, s5_b_re, s5_b_im, s5_c_re, s5_c_im, s5_d, s5_w_glu, s5_b_glu):
    b, t, d = x.shape
    ctx_len = ctx.shape[1]
    depth = w_mod.shape[0]
    assert depth == 2 and b <= 7
    ta = ctx_len + t
    a_width = d // 2
    s5_width = d // 4
    cq_width = d - s5_width
    ckv_width = cq_width // 3

    cc = jnp.concatenate([c, c_ctx[None], jnp.zeros((7 - b, d), F32)], axis=0)
    mod = _modulation(cc, w_mod, b_mod)
    mod_lat = mod[:, :b].reshape(depth, b, 1, 6, d)
    mod_ctx = jnp.broadcast_to(mod[:, b].reshape(depth, 1, 1, 6, d), (depth, b, 1, 6, d))
    modtab = jnp.concatenate([mod_ctx, mod_lat], axis=2)

    lb_all = jnp.cumsum(jax.nn.softmax(hgrn_lb.astype(F32), axis=0), axis=0)
    hx = _prenorm((ctx, x), g_mix_pre[0], modtab[0])
    p = _matmul(hx.reshape(b * ta, d), ev_w_in[0]).reshape(b, ta, -1)
    o_f, o_b = _hgrn(p, lb_all[0], ctx_len, a_width)
    mix_a = _hgrn_readout(o_f, o_b, p, hgrn_g_norm[0], a_width)
    mix_b = _natten(p, _natten_bias(na_rpb[0]), ctx_len, 5 * a_width, d - a_width)
    xa, h2, logits = _postmix(mix_a, mix_b, ev_w_out[0], (ctx, x), modtab[0], g_mix_post[0], g_ffn_pre[0], w_router[0])
    xa = _ec_moe(jnp.swapaxes(logits, 1, 2), h2, xa, modtab[0], g_ffn_post[0], w_exp_gate, w_exp_up, w_exp_down, 0, ctx_len)

    hx = _prenorm(xa, g_mix_pre[1], modtab[1])
    p = _matmul(hx.reshape(b * ta, d), od_w_in[0]).reshape(b, ta, -1)
    cosf, sinf = _rope_tables(ctx_len, t)
    qn, kn, vb = _qkprep(p, cosf, sinf, q_norm[0], k_norm[0], cq_width, ckv_width)
    mix_a = _gqa_attention(qn, kn, vb, cq_width // ckv_width, ctx_len // TOK_TILE)
    ops = _s5_operators(s5_a_re[0], s5_a_im[0], s5_log_dt[0], s5_b_re[0], s5_b_im[0], s5_c_re[0], s5_c_im[0])
    mix_b = _s5(p, cq_width + 2 * ckv_width, s5_width, ctx_len, ops, s5_d[0], s5_w_glu[0], s5_b_glu[0])
    xa, h2, logits = _postmix(mix_a, mix_b, od_w_out[0], xa, modtab[1], g_mix_post[1], g_ffn_pre[1], w_router[1])
    xa = _ec_moe(jnp.swapaxes(logits, 1, 2), h2, xa, modtab[1], g_ffn_post[1], w_exp_gate, w_exp_up, w_exp_down, 1, ctx_len)
    return xa[:, ctx_len:]
```

```python
import functools
import math

import jax
import jax.numpy as jnp
from jax import lax
from jax.experimental import pallas as pl
from jax.experimental.pallas import tpu as pltpu

F32 = jnp.float32
BF16 = jnp.bfloat16
HIGHEST = lax.Precision.HIGHEST
EPS = 1e-6

LANE = 128
TOK_TILE = 256
VMEM_LIMIT = 52 << 20

GRID_W = 64
NA_ROWS = 8
NA_COLS = 16
NA_GROUP = 4
NA_SPAN = 12
HEAD = 128
A_CHUNK = 64
N_EXPERTS = 16
EC_CAPACITY = 2
S5_GROUP = 16
S5_STATE = 64
S5_CHUNK = 16
COMBINE_WIN = 64
GATHER_WIN = 64
ROPE_THETA = 10000.0
NEG_BIG = -1e30
LOG2E = 1.4426950408889634


def _cparams(*sem):
    return pltpu.CompilerParams(dimension_semantics=sem, vmem_limit_bytes=VMEM_LIMIT)


def _dot(a, b):
    return jnp.dot(a, b, preferred_element_type=F32)


def _dot_nt(a, b):
    return lax.dot_general(a, b, (((1,), (1,)), ((), ())), preferred_element_type=F32)


def _dot_tn(a, b):
    return lax.dot_general(a, b, (((0,), (0,)), ((), ())), preferred_element_type=F32)


def _rms(x):
    return x * lax.rsqrt(jnp.mean(x * x, axis=-1, keepdims=True) + EPS)


def _silu(x):
    return x * jax.nn.sigmoid(x)


def _mod_kernel(c_ref, w_ref, b_ref, o_ref):
    o_ref[...] = jnp.dot(_silu(c_ref[...]), w_ref[...], preferred_element_type=F32, precision=HIGHEST) + b_ref[...]


def _modulation(cc, w_mod, b_mod):
    depth, d, n = w_mod.shape
    tn = 1024
    return pl.pallas_call(
        _mod_kernel,
        out_shape=jax.ShapeDtypeStruct((depth, cc.shape[0], n), F32),
        grid=(depth, n // tn),
        in_specs=[pl.BlockSpec(cc.shape, lambda l, j: (0, 0)),
                  pl.BlockSpec((None, d, tn), lambda l, j: (l, 0, j)),
                  pl.BlockSpec((None, 1, tn), lambda l, j: (l, 0, j))],
        out_specs=pl.BlockSpec((None, cc.shape[0], tn), lambda l, j: (l, 0, j)),
        compiler_params=_cparams("arbitrary", "arbitrary"),
        name="modulation",
    )(cc, w_mod, b_mod.reshape(depth, 1, n))


def _mod_spec(d):
    return pl.BlockSpec((None, None, 6, d), lambda b, j: (b, jnp.minimum(j, 1), 0, 0))


def _stream_operands(xs):
    if not isinstance(xs, tuple):
        b, ta, d = xs.shape
        return (b, ta, d), [pl.BlockSpec((None, TOK_TILE, d), lambda i, j: (i, j, 0))], [xs]
    ctx, x = xs
    b, t, d = x.shape
    assert ctx.shape[1] == TOK_TILE
    return (b, TOK_TILE + t, d), [pl.BlockSpec((None, TOK_TILE, d), lambda i, j: (i, 0, 0)),
                                  pl.BlockSpec((None, TOK_TILE, d), lambda i, j: (i, jnp.maximum(j - 1, 0), 0))], [ctx, x]


def _stream_tile(refs):
    if len(refs) == 1:
        return refs[0][...]
    return jnp.where(pl.program_id(1) == 0, refs[0][...], refs[1][...])


def _prenorm_kernel(*refs, nx):
    g_ref, m_ref, o_ref = refs[nx:]
    m = m_ref[...]
    y = _rms(_stream_tile(refs[:nx])) * g_ref[...]
    o_ref[...] = (y * (1.0 + m[1:2]) + m[0:1]).astype(BF16)


def _prenorm(xs, g, modtab):
    (b, ta, d), x_specs, x_args = _stream_operands(xs)
    return pl.pallas_call(
        functools.partial(_prenorm_kernel, nx=len(x_args)),
        out_shape=jax.ShapeDtypeStruct((b, ta, d), BF16),
        grid=(b, ta // TOK_TILE),
        in_specs=x_specs + [pl.BlockSpec((1, d), lambda i, j: (0, 0)), _mod_spec(d)],
        out_specs=pl.BlockSpec((None, TOK_TILE, d), lambda i, j: (i, j, 0)),
        compiler_params=_cparams("parallel", "parallel"),
        name="prenorm",
    )(*x_args, g.reshape(1, d), modtab)


def _mm_kernel(a_ref, w_ref, o_ref, wb_ref):
    @pl.when(pl.program_id(1) == 0)
    def _():
        wb_ref[...] = w_ref[...].astype(BF16)

    o_ref[...] = _dot(a_ref[...], wb_ref[...]).astype(o_ref.dtype)


def _matmul(a, w, tm=512, tn=1024, out_dtype=F32):
    m, k = a.shape
    n = w.shape[1]
    return pl.pallas_call(
        _mm_kernel,
        out_shape=jax.ShapeDtypeStruct((m, n), out_dtype),
        grid=(n // tn, m // tm),
        in_specs=[pl.BlockSpec((tm, k), lambda j, i: (i, 0)),
                  pl.BlockSpec((k, tn), lambda j, i: (0, j))],
        out_specs=pl.BlockSpec((tm, tn), lambda j, i: (i, j)),
        scratch_shapes=[pltpu.VMEM((k, tn), BF16)],
        compiler_params=_cparams("arbitrary", "arbitrary"),
        name="proj_in",
    )(a, w)


def _hgrn_masks(chunk):
    row = lax.broadcasted_iota(jnp.int32, (chunk, LANE), 0)
    ti = lax.broadcasted_iota(jnp.int32, (chunk, chunk), 0)
    si = lax.broadcasted_iota(jnp.int32, (chunk, chunk), 1)
    levels = [(((row >> lvl) & 1) == 1, (ti >> (lvl + 1)) == (si >> (lvl + 1))) for lvl in range(chunk.bit_length() - 1)]
    return ti == si, levels


def _hgrn_chunk(q, fr, v, lb, st, rev, masks):
    f = lb + (1.0 - lb) * jax.nn.sigmoid(fr)
    kk = 1.0 - f
    lf = jnp.log(f)
    chunk = q.shape[0]
    diag, levels = masks
    att = jnp.where(diag, _dot_nt(q.astype(BF16), kk.astype(BF16)), 0.0)
    p_in = lf
    r_ex = jnp.zeros_like(lf)
    tot = lf
    for lvl, (bit, same) in enumerate(levels):
        step = 1 << lvl
        up = pltpu.roll(tot, step, 0)
        dn = pltpu.roll(tot, chunk - step, 0)
        is_q = jnp.logical_not(bit) if rev else bit
        qf = jnp.where(is_q, jnp.exp(p_in) * q, 0.0).astype(BF16)
        kf = jnp.where(is_q, 0.0, jnp.exp(r_ex) * kk).astype(BF16)
        att = att + jnp.where(same, _dot_nt(qf, kf), 0.0)
        if rev:
            p_in = p_in + jnp.where(bit, 0.0, dn)
            r_ex = r_ex + jnp.where(bit, up, 0.0)
        else:
            p_in = p_in + jnp.where(bit, up, 0.0)
            r_ex = r_ex + jnp.where(bit, 0.0, dn)
        tot = tot + jnp.where(bit, up, dn)
    vb = v.astype(BF16)
    o = _dot_nt((q * jnp.exp(p_in)).astype(BF16), st.astype(BF16)) + _dot(att.astype(BF16), vb)
    kd = (kk * jnp.exp(r_ex)).astype(BF16)
    st_new = st * jnp.exp(tot[0:1]) + _dot_tn(vb, kd)
    return o, st_new


def _hgrn_kernel(qf_ref, ff_ref, vf_ref, qb_ref, fb_ref, vb_ref, lb_ref, of_ref, ob_ref, st_ref, *, chunk, hb):
    @pl.when(pl.program_id(2) == 0)
    def _():
        st_ref[...] = jnp.zeros_like(st_ref)

    masks = _hgrn_masks(chunk)
    for h in range(hb):
        sl = slice(h * HEAD, (h + 1) * HEAD)
        o, st = _hgrn_chunk(qf_ref[:, sl], ff_ref[:, sl], vf_ref[:, sl], lb_ref[0:1, sl], st_ref[0, h], False, masks)
        of_ref[:, sl] = o
        st_ref[0, h] = st
        o, st = _hgrn_chunk(qb_ref[:, sl], fb_ref[:, sl], vb_ref[:, sl], lb_ref[1:2, sl], st_ref[1, h], True, masks)
        ob_ref[:, sl] = o
        st_ref[1, h] = st


def _hgrn(p, lb, ctx_len, width, hb=4):
    b, ta, _ = p.shape
    chunk = A_CHUNK
    n = ta // chunk
    nc = ctx_len // chunk
    cw = HEAD * hb
    nh = width // cw
    bwd = lambda j: jnp.where(j < nc, nc - 1 - j, n - 1 + nc - j)

    def spec(base, rev):
        return pl.BlockSpec((None, chunk, cw), lambda i, h, j: (i, bwd(j) if rev else j, base // cw + h))

    out = jax.ShapeDtypeStruct((b, ta, width), F32)
    return pl.pallas_call(
        functools.partial(_hgrn_kernel, chunk=chunk, hb=hb),
        out_shape=(out, out),
        grid=(b, nh, n),
        in_specs=[spec(0, False), spec(width, False), spec(3 * width, False),
                  spec(0, True), spec(2 * width, True), spec(3 * width, True),
                  pl.BlockSpec((2, cw), lambda i, h, j: (0, h))],
        out_specs=(spec(0, False), spec(0, True)),
        scratch_shapes=[pltpu.VMEM((2, hb, HEAD, HEAD), F32)],
        compiler_params=_cparams("parallel", "parallel", "arbitrary"),
        name="hgrn_scan",
    )(p, p, p, p, p, p, lb)


def _hgrn_readout_kernel(of_ref, ob_ref, g_ref, gn_ref, o_ref):
    for h in range(o_ref.shape[-1] // HEAD):
        sl = slice(h * HEAD, (h + 1) * HEAD)
        o = of_ref[:, sl] + ob_ref[:, sl]
        o_ref[:, sl] = (_rms(o) * gn_ref[...] * _silu(g_ref[:, sl])).astype(BF16)


def _hgrn_readout(o_f, o_b, p, g_norm, width):
    b, ta, _ = o_f.shape
    spec = pl.BlockSpec((None, TOK_TILE, width), lambda i, j: (i, j, 0))
    return pl.pallas_call(
        _hgrn_readout_kernel,
        out_shape=jax.ShapeDtypeStruct((b, ta, width), BF16),
        grid=(b, ta // TOK_TILE),
        in_specs=[spec, spec,
                  pl.BlockSpec((None, TOK_TILE, width), lambda i, j: (i, j, 4)),
                  pl.BlockSpec((1, HEAD), lambda i, j: (0, 0))],
        out_specs=spec,
        compiler_params=_cparams("parallel", "parallel"),
        name="hgrn_readout",
    )(o_f, o_b, p, g_norm.reshape(1, HEAD))


def _natten_bias(rpb):
    nrow, ncol = 2 * NA_ROWS - 1, 2 * NA_COLS - 1
    col = jnp.arange(GRID_W)
    col_start = jnp.clip(col - NA_COLS // 2, 0, GRID_W - NA_COLS)
    cmask = (col[None, :] >= col_start[:, None]) & (col[None, :] < col_start[:, None] + NA_COLS)
    dc = col[None, :] - col[:, None] + NA_COLS - 1
    cm = (cmask[:, :, None] & (dc[:, :, None] == jnp.arange(ncol))).astype(F32)
    cls = jnp.arange(3)[:, None, None]
    i = jnp.arange(NA_GROUP)[None, :, None]
    w = jnp.arange(NA_SPAN)[None, None, :]
    first_w = jnp.where(cls == 0, 0, jnp.where(cls == 1, i, NA_SPAN - NA_ROWS))
    valid = (w >= first_w) & (w < first_w + NA_ROWS)
    span0 = jnp.where(cls == 0, 0, jnp.where(cls == 1, -(NA_ROWS // 2), NA_GROUP - NA_SPAN))
    dr = span0 + w - i + NA_ROWS - 1
    rm = (valid[..., None] & (dr[..., None] == jnp.arange(nrow))).astype(F32)
    t = jnp.einsum('qkb,hab->haqk', cm, rpb.astype(F32), precision=HIGHEST)
    t = jnp.einsum('ciwa,haqk->hciqwk', rm, t, precision=HIGHEST)
    ok = valid[None, :, :, None, :, None] & cmask[None, None, None, :, None, :]
    t = jnp.where(ok, t, NEG_BIG)
    return t.reshape(rpb.shape[0], 3, NA_GROUP * GRID_W, NA_SPAN * GRID_W)


def _natten_kernel(q_ref, k_ref, v_ref, bias_ref, o_ref, *, ctx_len, rows):
    j = pl.program_id(2)
    nst = rows // NA_GROUP
    scale = HEAD ** -0.5
    heads = [slice(h * HEAD, (h + 1) * HEAD) for h in range(q_ref.shape[-1] // HEAD)]

    def ctx_scores(sl):
        qb = q_ref[:, sl].astype(BF16)
        vc = v_ref[0:ctx_len, sl].astype(BF16)
        s_ctx = _dot_nt(qb, k_ref[0:ctx_len, sl].astype(BF16)) * scale
        return qb, vc, s_ctx, jnp.max(s_ctx, axis=1, keepdims=True)

    @pl.when(j < nst)
    def _():
        last = j == nst - 1
        cls = jnp.where(j == 0, 0, jnp.where(last, 2, 1))
        span_row = jnp.where(j == 0, 0, jnp.where(last, rows - NA_SPAN, NA_GROUP * j - NA_ROWS // 2))
        start = pl.multiple_of(ctx_len + span_row * GRID_W, GRID_W)
        for h, sl in enumerate(heads):
            qb, vc, s_ctx, m_ctx = ctx_scores(sl)
            kw = k_ref[pl.ds(start, NA_SPAN * GRID_W), sl].astype(BF16)
            vw = v_ref[pl.ds(start, NA_SPAN * GRID_W), sl].astype(BF16)
            s_win = _dot_nt(qb, kw) * scale + bias_ref[h, cls]
            m = jnp.maximum(m_ctx, jnp.max(s_win, axis=1, keepdims=True))
            pw = jnp.exp(s_win - m)
            pc = jnp.exp(s_ctx - m)
            den = jnp.sum(pw, axis=1, keepdims=True) + jnp.sum(pc, axis=1, keepdims=True)
            o = _dot(pw.astype(BF16), vw) + _dot(pc.astype(BF16), vc)
            o_ref[:, sl] = (o / den).astype(BF16)

    @pl.when(j >= nst)
    def _():
        for sl in heads:
            _, vc, s_ctx, m_ctx = ctx_scores(sl)
            pc = jnp.exp(s_ctx - m_ctx)
            o = _dot(pc.astype(BF16), vc) / jnp.sum(pc, axis=1, keepdims=True)
            o_ref[:, sl] = o.astype(BF16)


def _natten(p, bias, ctx_len, col0, width, hb=2):
    b, ta, _ = p.shape
    cw = HEAD * hb
    nh = width // cw
    rows = (ta - ctx_len) // GRID_W
    tq = NA_GROUP * GRID_W
    nst = rows // NA_GROUP
    ncq = ctx_len // tq
    assert rows % NA_GROUP == 0 and rows >= NA_SPAN and ctx_len % tq == 0

    def qmap(cb):
        return lambda i, h, j: (i, jnp.where(j < nst, ncq + j, j - nst), cb + h)

    return pl.pallas_call(
        functools.partial(_natten_kernel, ctx_len=ctx_len, rows=rows),
        out_shape=jax.ShapeDtypeStruct((b, ta, width), BF16),
        grid=(b, nh, nst + ncq),
        in_specs=[pl.BlockSpec((None, tq, cw), qmap(col0 // cw)),
                  pl.BlockSpec((None, ta, cw), lambda i, h, j: (i, 0, (col0 + width) // cw + h)),
                  pl.BlockSpec((None, ta, cw), lambda i, h, j: (i, 0, (col0 + 2 * width) // cw + h)),
                  pl.BlockSpec((hb, 3, tq, NA_SPAN * GRID_W), lambda i, h, j: (h, 0, 0, 0))],
        out_specs=pl.BlockSpec((None, tq, cw), qmap(0)),
        compiler_params=_cparams("parallel", "parallel", "arbitrary"),
        name="natten",
    )(p, p, p, bias)


def _rope_tables(ctx_len, t):
    pos = jnp.arange(t)
    row = (pos // GRID_W).astype(F32)
    col = (pos % GRID_W).astype(F32)
    half = HEAD // 2
    inv = ROPE_THETA ** (-jnp.arange(0, half, 2, dtype=F32) / half)
    ang = jnp.concatenate([row[:, None] * inv, col[:, None] * inv], axis=-1)
    cos, sin = jnp.cos(ang), jnp.sin(ang)
    cosf = jnp.repeat(cos, 2, axis=-1)
    sinf = jnp.stack([-sin, sin], axis=-1).reshape(t, HEAD)
    cosf = jnp.concatenate([jnp.ones((ctx_len, HEAD), F32), cosf], axis=0)
    sinf = jnp.concatenate([jnp.zeros((ctx_len, HEAD), F32), sinf], axis=0)
    return cosf, sinf


def _rope(x, cosf, sinf):
    even = (lax.broadcasted_iota(jnp.int32, x.shape, 1) & 1) == 0
    partner = jnp.where(even, pltpu.roll(x, LANE - 1, 1), pltpu.roll(x, 1, 1))
    return x * cosf + partner * sinf


def _attn_kernel(q_ref, k_ref, v_ref, cq_ref, sq_ref, ck_ref, sk_ref, qn_ref, kn_ref, o_ref, kb_ref, vb_ref,
                 *, skip_tiles):
    j = pl.program_id(2)

    @pl.when(j == 0)
    def _():
        kb_ref[...] = _rope(_rms(k_ref[...]) * kn_ref[...], ck_ref[...], sk_ref[...]).astype(BF16)
        vb_ref[...] = v_ref[...].astype(BF16)

    @pl.when(j < skip_tiles)
    def _():
        o_ref[...] = jnp.zeros_like(o_ref)

    @pl.when(j >= skip_tiles)
    def _():
        k = kb_ref[...]
        v = vb_ref[...]
        cq = cq_ref[...]
        sq = sq_ref[...]
        for g in range(q_ref.shape[-1] // HEAD):
            sl = slice(g * HEAD, (g + 1) * HEAD)
            q = (_rope(_rms(q_ref[:, sl]) * qn_ref[...], cq, sq) * (HEAD ** -0.5 * LOG2E)).astype(BF16)
            s = _dot_nt(q, k)
            p = jnp.exp2(s - jnp.max(s, axis=1, keepdims=True))
            o = _dot(p.astype(BF16), v) / jnp.sum(p, axis=1, keepdims=True)
            o_ref[:, sl] = o.astype(BF16)


def _gqa_attention(p, cosf, sinf, q_norm, k_norm, qw, kw, skip_tiles):
    b, ta, _ = p.shape
    nkv = kw // HEAD
    tq = TOK_TILE
    gw = qw // nkv
    tab_q = pl.BlockSpec((tq, HEAD), lambda i, h, j: (j, 0))
    tab_k = pl.BlockSpec((ta, HEAD), lambda i, h, j: (0, 0))
    vec = pl.BlockSpec((1, HEAD), lambda i, h, j: (0, 0))
    return pl.pallas_call(
        functools.partial(_attn_kernel, skip_tiles=skip_tiles),
        out_shape=jax.ShapeDtypeStruct((b, ta, qw), BF16),
        grid=(b, nkv, ta // tq),
        in_specs=[pl.BlockSpec((None, tq, gw), lambda i, h, j: (i, j, h)),
                  pl.BlockSpec((None, ta, HEAD), lambda i, h, j: (i, 0, qw // HEAD + h)),
                  pl.BlockSpec((None, ta, HEAD), lambda i, h, j: (i, 0, (qw + kw) // HEAD + h)),
                  tab_q, tab_q, tab_k, tab_k, vec, vec],
        out_specs=pl.BlockSpec((None, tq, gw), lambda i, h, j: (i, j, h)),
        scratch_shapes=[pltpu.VMEM((ta, HEAD), BF16), pltpu.VMEM((ta, HEAD), BF16)],
        compiler_params=_cparams("parallel", "parallel", "arbitrary"),
        name="gqa_attention",
    )(p, p, p, cosf, sinf, cosf, sinf, q_norm.reshape(1, HEAD), k_norm.reshape(1, HEAD))


def _s5_operators(a_re, a_im, log_dt, b_re, b_im, c_re, c_im):
    lc, pp, ns = S5_CHUNK, S5_GROUP, S5_STATE
    a_c = lax.complex(a_re.astype(F32), a_im.astype(F32))
    adt = a_c * jnp.exp(log_dt.astype(F32))[..., None]
    a_bar = jnp.exp(adt)
    b_bar = ((a_bar - 1.0) / a_c)[..., None] * lax.complex(b_re.astype(F32), b_im.astype(F32))
    c_mat = lax.complex(c_re.astype(F32), c_im.astype(F32))
    nd, g = a_re.shape[0], a_re.shape[1]
    pos = jnp.arange(lc)
    lag = pos[None, :] - pos[:, None]
    lag = jnp.stack([lag, -lag])
    live = (lag >= 0)[:, None, :, :, None]
    apl = jnp.where(live, jnp.exp(adt[:, :, None, None, :] * jnp.maximum(lag, 0).astype(F32)[:, None, :, :, None]), 0.0)
    tm = jnp.real(jnp.einsum('dgpn,dgion,dgnq->dgiqop', c_mat, apl, b_bar, precision=HIGHEST))
    tmat = tm.reshape(nd, g, lc * pp, lc * pp)
    steps_after = jnp.stack([lc - 1 - pos, pos]).astype(F32)
    gm = jnp.exp(adt[:, :, None, :] * steps_after[:, None, :, None])[:, :, :, None, :] * jnp.swapaxes(b_bar, 2, 3)[:, :, None]
    gm = gm.reshape(nd, g, lc * pp, ns)
    steps_upto = jnp.stack([pos + 1, lc - pos]).astype(F32)
    hm = c_mat[:, :, None] * jnp.exp(adt[:, :, None, :] * steps_upto[:, None, :, None])[:, :, :, None, :]
    hm = jnp.transpose(hm, (0, 1, 4, 2, 3)).reshape(nd, g, ns, lc * pp)
    hmat = jnp.concatenate([jnp.real(hm), -jnp.imag(hm)], axis=2)
    al = jnp.exp(adt * float(lc))
    al = al[:, :, None, :]
    gd = lambda x: jnp.swapaxes(x, 0, 1)
    return (gd(tmat).astype(BF16), gd(jnp.real(gm)).astype(BF16), gd(jnp.imag(gm)).astype(BF16),
            gd(hmat).astype(BF16), gd(jnp.real(al)), gd(jnp.imag(al)))


def _s5_kernel(u_ref, t_ref, gr_ref, gi_ref, h_ref, ar_ref, ai_ref, d_ref, y_ref, er_ref, ei_ref, xr_ref, xi_ref,
               *, nchunk, nctx, rpc):
    ns = S5_STATE
    u = u_ref[...]
    for dr in range(2):
        er_ref[dr] = _dot(u, gr_ref[dr])
        ei_ref[dr] = _dot(u, gi_ref[dr])

    coef = [(jnp.broadcast_to(ar_ref[dr], (rpc, ns)), jnp.broadcast_to(ai_ref[dr], (rpc, ns))) for dr in range(2)]

    def body(c, carry):
        out = []
        for dr in range(2):
            xr, xi = carry[2 * dr], carry[2 * dr + 1]
            pc = c if dr == 0 else jnp.where(c < nctx, nctx - 1 - c, nchunk - 1 + nctx - c)
            off = pl.multiple_of(pc * rpc, rpc)
            xr_ref[dr, pl.ds(off, rpc), :] = xr
            xi_ref[dr, pl.ds(off, rpc), :] = xi
            ar, ai = coef[dr]
            out.append(ar * xr - ai * xi + er_ref[dr, pl.ds(off, rpc), :])
            out.append(ar * xi + ai * xr + ei_ref[dr, pl.ds(off, rpc), :])
        return tuple(out)

    zero = jnp.zeros((rpc, ns), F32)
    lax.fori_loop(0, nchunk, body, (zero, zero, zero, zero), unroll=4)
    y = d_ref[...] * u.astype(F32)
    for dr in range(2):
        y = y + _dot(u, t_ref[dr])
        y = y + _dot(xr_ref[dr].astype(BF16), h_ref[dr, 0:ns, :]) + _dot(xi_ref[dr].astype(BF16), h_ref[dr, ns:2 * ns, :])
    y_ref[...] = y.astype(BF16)


def _s5_scan(u, ops, d_skip, ctx_len):
    tmat, g_re, g_im, hmat, a_re, a_im = ops
    b, ta, w = u.shape
    lc, pp, ns = S5_CHUNK, S5_GROUP, S5_STATE
    g = w // pp
    nchunk = ta // lc
    rpc = 8
    cw = lc * pp
    rows = nchunk * rpc
    ug = jnp.transpose(u.astype(BF16).reshape(b, nchunk, lc, g, pp), (3, 1, 0, 2, 4))
    ug = jnp.pad(ug, ((0, 0), (0, 0), (0, rpc - b), (0, 0), (0, 0))).reshape(g, rows, cw)
    dvec = jnp.tile(d_skip.astype(F32).reshape(g, 1, pp), (1, lc, 1)).reshape(g, 1, cw)
    op = lambda r, c: pl.BlockSpec((None, 2, r, c), lambda i: (i, 0, 0, 0))
    y = pl.pallas_call(
        functools.partial(_s5_kernel, nchunk=nchunk, nctx=ctx_len // lc, rpc=rpc),
        out_shape=jax.ShapeDtypeStruct((g, rows, cw), BF16),
        grid=(g,),
        in_specs=[pl.BlockSpec((None, rows, cw), lambda i: (i, 0, 0)),
                  op(cw, cw), op(cw, ns), op(cw, ns), op(2 * ns, cw), op(1, ns), op(1, ns),
                  pl.BlockSpec((None, 1, cw), lambda i: (i, 0, 0))],
        out_specs=pl.BlockSpec((None, rows, cw), lambda i: (i, 0, 0)),
        scratch_shapes=[pltpu.VMEM((2, rows, ns), F32) for _ in range(4)],
        compiler_params=_cparams("parallel"),
        name="s5_scan",
    )(ug, tmat, g_re, g_im, hmat, a_re, a_im, dvec)
    y = y.reshape(g, nchunk, rpc, lc, pp)[:, :, :b]
    return jnp.transpose(y, (2, 1, 3, 0, 4)).reshape(b, ta, w)


def _s5_glu_kernel(y_ref, w_ref, b_ref, o_ref):
    y = y_ref[...].astype(F32)
    y = 0.5 * y * (1.0 + jnp.tanh(math.sqrt(2.0 / math.pi) * (y + 0.044715 * (y * y * y))))
    z = _dot(y.astype(BF16), w_ref[...]) + b_ref[...]
    o_ref[...] = (y * jax.nn.sigmoid(z)).astype(BF16)


def _s5_glu(y, w_glu, b_glu):
    b, ta, w = y.shape
    spec = pl.BlockSpec((None, TOK_TILE, w), lambda i, j: (i, j, 0))
    return pl.pallas_call(
        _s5_glu_kernel,
        out_shape=jax.ShapeDtypeStruct((b, ta, w), BF16),
        grid=(b, ta // TOK_TILE),
        in_specs=[spec, pl.BlockSpec((w, w), lambda i, j: (0, 0)), pl.BlockSpec((1, w), lambda i, j: (0, 0))],
        out_specs=spec,
        compiler_params=_cparams("parallel", "parallel"),
        name="s5_glu",
    )(y, w_glu.astype(BF16), b_glu.reshape(1, w))


def _s5(p, ucol, width, ctx_len, ops, d_skip, w_glu, b_glu):
    y = _s5_scan(p[..., ucol:ucol + width], ops, d_skip, ctx_len)
    return _s5_glu(y, w_glu, b_glu)


def _postmix_kernel(a_ref, b_ref, w_ref, *refs, nx):
    m_ref, gpost_ref, gpre_ref, wr_ref, xo_ref, h_ref, lg_ref = refs[nx:]
    wa = a_ref.shape[-1]
    y = _dot(a_ref[...], w_ref[0:wa, :]) + _dot(b_ref[...], w_ref[wa:, :])
    m = m_ref[...]
    xn = _stream_tile(refs[:nx]) + m[2:3] * (_rms(y) * gpost_ref[...])
    xo_ref[...] = xn
    h2 = _rms(xn) * gpre_ref[...] * (1.0 + m[4:5]) + m[3:4]
    hi = h2.astype(BF16)
    h_ref[...] = hi
    lo = (h2 - hi.astype(F32)).astype(BF16)
    ne = lg_ref.shape[-1]
    both = _dot(hi, wr_ref[...])
    lg_ref[...] = both[:, 0:ne] + both[:, ne:2 * ne] + _dot(lo, wr_ref[:, 0:ne])


def _postmix(mix_a, mix_b, w_out, xs, modtab, g_post, g_pre, w_router):
    (b, ta, d), x_specs, x_args = _stream_operands(xs)
    wa, wb = mix_a.shape[-1], mix_b.shape[-1]
    ne = w_router.shape[-1]
    tok = lambda w: pl.BlockSpec((None, TOK_TILE, w), lambda i, j: (i, j, 0))
    vec = pl.BlockSpec((1, d), lambda i, j: (0, 0))
    wr_hi = w_router.astype(BF16)
    wr2 = jnp.concatenate([wr_hi, (w_router - wr_hi.astype(F32)).astype(BF16)], axis=1)
    return pl.pallas_call(
        functools.partial(_postmix_kernel, nx=len(x_args)),
        out_shape=(jax.ShapeDtypeStruct((b, ta, d), F32),
                   jax.ShapeDtypeStruct((b, ta, d), BF16),
                   jax.ShapeDtypeStruct((b, ta, ne), F32)),
        grid=(b, ta // TOK_TILE),
        in_specs=[tok(wa), tok(wb),
                  pl.BlockSpec((wa + wb, d), lambda i, j: (0, 0))] + x_specs + [
                  _mod_spec(d), vec, vec,
                  pl.BlockSpec((d, 2 * ne), lambda i, j: (0, 0))],
        out_specs=(tok(d), tok(d), tok(ne)),
        compiler_params=_cparams("parallel", "parallel"),
        name="mix_out",
    )(mix_a, mix_b, w_out.astype(BF16), *x_args, modtab, g_post.reshape(1, d), g_pre.reshape(1, d), wr2)


def _select_kernel(lg_ref, pos_ref, gate_ref, *, ctx_len, cap_ctx, cap_lat):
    lg = lg_ref[...]
    ne, ta = lg.shape
    ex = jnp.exp(lg - jnp.max(lg, axis=0, keepdims=True))
    probs = ex / jnp.sum(ex, axis=0, keepdims=True)
    bits = pltpu.bitcast(probs, jnp.int32)
    is_ctx = lax.broadcasted_iota(jnp.int32, (ne, ta), 1) < ctx_len

    def counts(mask):
        mf = jnp.where(mask, 1.0, 0.0)
        return (jnp.sum(jnp.where(is_ctx, mf, 0.0), axis=1, keepdims=True),
                jnp.sum(jnp.where(is_ctx, 0.0, mf), axis=1, keepdims=True))

    def search(i, carry):
        pc, pt = carry
        bit = jnp.left_shift(jnp.int32(1), 30 - i)
        cc, ct = counts(bits >= jnp.where(is_ctx, pc | bit, pt | bit))
        return jnp.where(cc >= cap_ctx, pc | bit, pc), jnp.where(ct >= cap_lat, pt | bit, pt)

    z = jnp.zeros((ne, 1), jnp.int32)
    pc, pt = lax.fori_loop(0, 31, search, (z, z))
    thr = jnp.where(is_ctx, pc, pt)
    gt = bits > thr
    eq = bits == thr
    gc, gl = counts(gt)
    need = jnp.where(is_ctx, cap_ctx - gc, cap_lat - gl)

    nb = ta // LANE
    ut = jnp.where(lax.broadcasted_iota(jnp.int32, (LANE, LANE), 0) <= lax.broadcasted_iota(jnp.int32, (LANE, LANE), 1),
                   1.0, 0.0).astype(BF16)

    def lane_prefix(mask):
        mf = jnp.where(mask, 1.0, 0.0).astype(BF16)
        blocks = jnp.concatenate([mf[:, j * LANE:(j + 1) * LANE] for j in range(nb)], axis=0)
        inc = _dot(blocks, ut)
        outs = []
        off = jnp.zeros((ne, 1), F32)
        for j in range(nb):
            if j * LANE == ctx_len:
                off = jnp.zeros((ne, 1), F32)
            blk = inc[j * ne:(j + 1) * ne]
            outs.append(blk + off)
            off = off + blk[:, LANE - 1:LANE]
        return jnp.concatenate(outs, axis=1)

    sel = gt | (eq & (lane_prefix(eq) <= need))
    slot = lane_prefix(sel) - 1.0 + jnp.where(is_ctx, 0.0, float(cap_ctx))
    pos_ref[...] = jnp.where(sel, slot, -1.0)
    gate_ref[...] = jnp.where(sel, probs, 0.0)


def _moe_select(logits_t, ctx_len, cap_ctx, cap_lat):
    b, ne, ta = logits_t.shape
    spec = pl.BlockSpec((None, ne, ta), lambda i: (i, 0, 0))
    return pl.pallas_call(
        functools.partial(_select_kernel, ctx_len=ctx_len, cap_ctx=cap_ctx, cap_lat=cap_lat),
        out_shape=(jax.ShapeDtypeStruct((b, ne, ta), F32), jax.ShapeDtypeStruct((b, ne, ta), F32)),
        grid=(b,),
        in_specs=[spec],
        out_specs=(spec, spec),
        compiler_params=_cparams("parallel"),
        name="moe_select",
    )(logits_t)


def _slot_windows(pos_t, nt, win, nslot):
    b, ne, ta = pos_t.shape
    pt = pos_t.reshape(b, ne, nt, ta // nt)
    hi = jnp.max(pt, axis=-1).astype(jnp.int32) + 1
    lo = jnp.min(jnp.where(pt >= 0, pt, float(nslot)), axis=-1).astype(jnp.int32)
    lo = jnp.where(hi > 0, lo // 16 * 16, 0)
    npass = jnp.maximum(jnp.max((hi - lo + win - 1) // win, axis=1), 1)
    return jnp.transpose(lo, (0, 2, 1)).reshape(-1), npass.reshape(-1)


def _gather_kernel(start_ref, npass_ref, pos_ref, h_ref, o_ref, acc_ref):
    i = pl.program_id(0)
    eg = pl.program_id(1)
    j = pl.program_id(2)
    ge, tk = pos_ref.shape
    ne = pl.num_programs(1) * ge
    nslot = o_ref.shape[1]
    win = GATHER_WIN
    tile = i * pl.num_programs(2) + j

    @pl.when(j == 0)
    def _():
        acc_ref[...] = jnp.zeros_like(acc_ref)

    pos = pos_ref[...].astype(jnp.int32)
    slot = lax.broadcasted_iota(jnp.int32, (win, tk), 0)

    def one_pass(p, carry):
        src, blocks = [], []
        for el in range(ge):
            lo = start_ref[tile * ne + eg * ge + el] + p * win
            src.append(pl.multiple_of(jnp.minimum(lo, nslot - win), 16))
            pe = pos[el:el + 1, :]
            hit = (pe >= lo) & (pe < lo + win) & (slot == pe - src[el])
            blocks.append(jnp.where(hit, 1.0, 0.0).astype(BF16))
        part = _dot(jnp.concatenate(blocks, axis=0), h_ref[...])
        for el in range(ge):
            acc_ref[el, pl.ds(src[el], win), :] += part[el * win:(el + 1) * win]
        return carry

    lax.fori_loop(0, npass_ref[tile], one_pass, 0)

    @pl.when(j == pl.num_programs(2) - 1)
    def _():
        o_ref[...] = acc_ref[...].astype(BF16)


def _moe_gather(pos_t, h, nslot):
    b, ne, ta = pos_t.shape
    d = h.shape[-1]
    ge = 4
    tk = TOK_TILE
    nt = ta // tk
    start, npass = _slot_windows(pos_t, nt, GATHER_WIN, nslot)
    return pl.pallas_call(
        _gather_kernel,
        out_shape=jax.ShapeDtypeStruct((ne, b, nslot, d), BF16),
        grid_spec=pltpu.PrefetchScalarGridSpec(
            num_scalar_prefetch=2,
            grid=(b, ne // ge, nt),
            in_specs=[pl.BlockSpec((None, None, ge, tk), lambda i, g, j, *_: (i, g, 0, j)),
                      pl.BlockSpec((None, tk, d), lambda i, g, j, *_: (i, j, 0))],
            out_specs=pl.BlockSpec((ge, None, nslot, d), lambda i, g, j, *_: (g, i, 0, 0)),
            scratch_shapes=[pltpu.VMEM((ge, nslot, d), F32)]),
        compiler_params=_cparams("arbitrary", "arbitrary", "arbitrary"),
        name="moe_gather",
    )(start, npass, pos_t.reshape(b, ne // ge, ge, ta), h)


def _ffn_kernel(x_ref, wg_ref, wu_ref, wd_ref, o_ref, hid_ref, *, nf):
    s = pl.program_id(2)
    tf = wg_ref.shape[-1]

    @pl.when(s < nf)
    def _():
        x = x_ref[...]
        g = _dot(x, wg_ref[...].astype(BF16))
        u = _dot(x, wu_ref[...].astype(BF16))
        hid_ref[s] = (_silu(g) * u).astype(BF16)

    @pl.when(s >= nf)
    def _():
        acc = _dot(hid_ref[0], wd_ref[0:tf, :].astype(BF16))
        for f in range(1, nf):
            acc = acc + _dot(hid_ref[f], wd_ref[f * tf:(f + 1) * tf, :].astype(BF16))
        o_ref[...] = acc.astype(BF16)


def _moe_ffn(xin, w_gate, w_up, w_down, layer, tf=512):
    ne, ns, m, d = xin.shape
    ff = w_gate.shape[-1]
    nf = ff // tf
    nd = d // tf
    up = lambda e, s, f: (layer, e, 0, jnp.minimum(f, nf - 1))
    return pl.pallas_call(
        functools.partial(_ffn_kernel, nf=nf),
        out_shape=jax.ShapeDtypeStruct((ne, ns, m, d), BF16),
        grid=(ne, ns, nf + nd),
        in_specs=[pl.BlockSpec((None, None, m, d), lambda e, s, f: (e, s, 0, 0)),
                  pl.BlockSpec((None, None, d, tf), up),
                  pl.BlockSpec((None, None, d, tf), up),
                  pl.BlockSpec((None, None, ff, tf), lambda e, s, f: (layer, e, 0, jnp.maximum(f - nf, 0)))],
        out_specs=pl.BlockSpec((None, None, m, tf), lambda e, s, f: (e, s, 0, jnp.maximum(f - nf, 0))),
        scratch_shapes=[pltpu.VMEM((nf, m, tf), BF16)],
        compiler_params=_cparams("parallel", "parallel", "arbitrary"),
        name="moe_ffn",
    )(xin, w_gate, w_up, w_down)


def _combine_kernel(start_ref, npass_ref, pos_ref, gate_ref, y_hbm, x_ref, m_ref, g_ref, *refs, ctx_len, with_next):
    if with_next:
        gn_ref, mn_ref, o_ref, hn_ref, ybuf, acc_ref, sem = refs
    else:
        o_ref, ybuf, acc_ref, sem = refs
    i = pl.program_id(0)
    j = pl.program_id(1)
    tt, ne = pos_ref.shape
    nslot = y_hbm.shape[2]
    win = COMBINE_WIN
    tile = i * pl.num_programs(1) + j
    pos = pos_ref[...].astype(jnp.int32)
    gate = gate_ref[...]
    lane = lax.broadcasted_iota(jnp.int32, (tt, win), 1)

    def window_copy(e, src):
        return pltpu.make_async_copy(y_hbm.at[e, i, pl.ds(src, win), :], ybuf.at[pl.ds(e * win, win), :], sem.at[e])

    def one_pass(p):
        own_lo, src = [], []
        for e in range(ne):
            lo = start_ref[tile * ne + e] + p * win
            own_lo.append(lo)
            src.append(pl.multiple_of(jnp.minimum(lo, nslot - win), 16))
            window_copy(e, src[e]).start()
        blocks = []
        for e in range(ne):
            pe = pos[:, e:e + 1]
            hit = (pe >= own_lo[e]) & (pe < own_lo[e] + win) & (lane == pe - src[e])
            blocks.append(jnp.where(hit, gate[:, e:e + 1], 0.0).astype(BF16))
        w = jnp.concatenate(blocks, axis=1)
        for e in range(ne):
            window_copy(e, src[e]).wait()
        return _dot(w, ybuf[...])

    acc_ref[...] = one_pass(0)

    def extra(p, carry):
        acc_ref[...] += one_pass(p)
        return carry

    lax.fori_loop(1, npass_ref[tile], extra, 0)
    m = m_ref[...]
    is_ctx = (lax.broadcasted_iota(jnp.int32, (tt, 1), 0) + j * tt) < ctx_len
    mod_gate = jnp.where(is_ctx, m[0, 5:6], m[1, 5:6])
    xn = x_ref[...] + mod_gate * (_rms(acc_ref[...]) * g_ref[...])
    o_ref[...] = xn
    if with_next:
        mn = mn_ref[...]
        scale = jnp.where(is_ctx, mn[0, 1:2], mn[1, 1:2])
        shift = jnp.where(is_ctx, mn[0, 0:1], mn[1, 0:1])
        hn_ref[...] = (_rms(xn) * gn_ref[...] * (1.0 + scale) + shift).astype(BF16)


def _moe_combine(pos_t, gate_t, yout, xa, modtab, g_post, ctx_len, nxt=None):
    b, ta, d = xa.shape
    ne, _, nslot, _ = yout.shape
    nt = 16
    tt = ta // nt
    win = COMBINE_WIN
    start, npass = _slot_windows(pos_t, nt, win, nslot)
    tok = lambda w: pl.BlockSpec((None, tt, w), lambda i, j, *_: (i, j, 0))
    mods = pl.BlockSpec((None, 2, 6, d), lambda i, j, *_: (i, 0, 0, 0))
    vec = pl.BlockSpec((1, d), lambda i, j, *_: (0, 0))
    xo = jax.ShapeDtypeStruct((b, ta, d), F32)
    extra_in, extra_args = ([vec, mods], [nxt[0].reshape(1, d), nxt[1]]) if nxt else ([], [])
    return pl.pallas_call(
        functools.partial(_combine_kernel, ctx_len=ctx_len, with_next=bool(nxt)),
        out_shape=(xo, jax.ShapeDtypeStruct((b, ta, d), BF16)) if nxt else xo,
        grid_spec=pltpu.PrefetchScalarGridSpec(
            num_scalar_prefetch=2,
            grid=(b, nt),
            in_specs=[tok(ne), tok(ne),
                      pl.BlockSpec(memory_space=pl.ANY),
                      tok(d), mods, vec] + extra_in,
            out_specs=(tok(d), tok(d)) if nxt else tok(d),
            scratch_shapes=[pltpu.VMEM((ne * win, d), BF16),
                            pltpu.VMEM((tt, d), F32),
                            pltpu.SemaphoreType.DMA((ne,))]),
        compiler_params=_cparams("arbitrary", "arbitrary"),
        name="moe_combine",
    )(start, npass, jnp.swapaxes(pos_t, 1, 2), jnp.swapaxes(gate_t, 1, 2), yout, xa, modtab,
      g_post.reshape(1, d), *extra_args)


def _ec_moe(logits_t, h, xa, modtab, g_post, w_gate, w_up, w_down, layer, ctx_len, nxt=None):
    b, ne, ta = logits_t.shape
    d = h.shape[-1]
    cap_ctx = max(1, EC_CAPACITY * ctx_len // ne)
    cap_lat = max(1, EC_CAPACITY * (ta - ctx_len) // ne)
    nslot = cap_ctx + cap_lat
    pos_t, gate_t = _moe_select(logits_t, ctx_len, cap_ctx, cap_lat)
    xin = _moe_gather(pos_t, h, nslot)
    pair = 2 if b % 2 == 0 else 1
    yout = _moe_ffn(xin.reshape(ne, b // pair, pair * nslot, d), w_gate, w_up, w_down, layer)
    yout = yout.reshape(ne, b, nslot, d)
    return _moe_combine(pos_t, gate_t, yout, xa, modtab, g_post, ctx_len, nxt)


def kernel(x, c, ctx, c_ctx, w_mod, b_mod, g_mix_pre, g_mix_post, g_ffn_pre, g_ffn_post, w_router, w_exp_gate, w_exp_up, w_exp_down, ev_w_in, ev_w_out, hgrn_lb, hgrn_g_norm, na_rpb, od_w_in, od_w_out, q_norm, k_norm, s5_a_re, s5_a_im, s5_log_dt, s5_b_re, s5_b_im, s5_c_re, s5_c_im, s5_d, s5_w_glu, s5_b_glu):
    b, t, d = x.shape
    ctx_len = ctx.shape[1]
    depth = w_mod.shape[0]
    assert depth == 2 and b <= 7
    ta = ctx_len + t
    a_width = d // 2
    s5_width = d // 4
    cq_width = d - s5_width
    ckv_width = cq_width // 3

    cc = jnp.concatenate([c, c_ctx[None], jnp.zeros((7 - b, d), F32)], axis=0)
    mod = _modulation(cc, w_mod, b_mod)
    mod_lat = mod[:, :b].reshape(depth, b, 1, 6, d)
    mod_ctx = jnp.broadcast_to(mod[:, b].reshape(depth, 1, 1, 6, d), (depth, b, 1, 6, d))
    modtab = jnp.concatenate([mod_ctx, mod_lat], axis=2)

    lb_all = jnp.cumsum(jax.nn.softmax(hgrn_lb.astype(F32), axis=0), axis=0)
    hx = _prenorm((ctx, x), g_mix_pre[0], modtab[0])
    p = _matmul(hx.reshape(b * ta, d), ev_w_in[0]).reshape(b, ta, -1)
    o_f, o_b = _hgrn(p, lb_all[0], ctx_len, a_width)
    mix_a = _hgrn_readout(o_f, o_b, p, hgrn_g_norm[0], a_width)
    mix_b = _natten(p, _natten_bias(na_rpb[0]), ctx_len, 5 * a_width, d - a_width)
    xa, h2, logits = _postmix(mix_a, mix_b, ev_w_out[0], (ctx, x), modtab[0], g_mix_post[0], g_ffn_pre[0], w_router[0])
    xa, hx = _ec_moe(jnp.swapaxes(logits, 1, 2), h2, xa, modtab[0], g_ffn_post[0], w_exp_gate, w_exp_up, w_exp_down, 0,
                     ctx_len, nxt=(g_mix_pre[1], modtab[1]))

    p = _matmul(hx.reshape(b * ta, d), od_w_in[0]).reshape(b, ta, -1)
    cosf, sinf = _rope_tables(ctx_len, t)
    mix_a = _gqa_attention(p, cosf, sinf, q_norm[0], k_norm[0], cq_width, ckv_width, ctx_len // TOK_TILE)
    ops = _s5_operators(s5_a_re[0], s5_a_im[0], s5_log_dt[0], s5_b_re[0], s5_b_im[0], s5_c_re[0], s5_c_im[0])
    mix_b = _s5(p, cq_width + 2 * ckv_width, s5_width, ctx_len, ops, s5_d[0], s5_w_glu[0], s5_b_glu[0])
    xa, h2, logits = _postmix(mix_a, mix_b, od_w_out[0], xa, modtab[1], g_mix_post[1], g_ffn_pre[1], w_router[1])
    xa = _ec_moe(jnp.swapaxes(logits, 1, 2), h2, xa, modtab[1], g_ffn_post[1], w_exp_gate, w_exp_up, w_exp_down, 1, ctx_len)
    return xa[:, ctx_len:]
```

```python
import functools
import math

import jax
import jax.numpy as jnp
from jax import lax
from jax.experimental import pallas as pl
from jax.experimental.pallas import tpu as pltpu

F32 = jnp.float32
BF16 = jnp.bfloat16
HIGHEST = lax.Precision.HIGHEST
EPS = 1e-6

LANE = 128
TOK_TILE = 256
VMEM_LIMIT = 52 << 20

GRID_W = 64
NA_ROWS = 8
NA_COLS = 16
NA_GROUP = 4
NA_SPAN = 12
HEAD = 128
A_CHUNK = 64
N_EXPERTS = 16
EC_CAPACITY = 2
S5_GROUP = 16
S5_STATE = 64
S5_CHUNK = 16
COMBINE_WIN = 64
GATHER_WIN = 64
ROPE_THETA = 10000.0
NEG_BIG = -1e30
LOG2E = 1.4426950408889634


def _cparams(*sem):
    return pltpu.CompilerParams(dimension_semantics=sem, vmem_limit_bytes=VMEM_LIMIT)


def _dot(a, b):
    return jnp.dot(a, b, preferred_element_type=F32)


def _dot_nt(a, b):
    return lax.dot_general(a, b, (((1,), (1,)), ((), ())), preferred_element_type=F32)


def _dot_tn(a, b):
    return lax.dot_general(a, b, (((0,), (0,)), ((), ())), preferred_element_type=F32)


def _rms(x):
    return x * lax.rsqrt(jnp.mean(x * x, axis=-1, keepdims=True) + EPS)


def _silu(x):
    return x * jax.nn.sigmoid(x)


def _mod_kernel(c_ref, w_ref, b_ref, o_ref):
    o_ref[...] = jnp.dot(_silu(c_ref[...]), w_ref[...], preferred_element_type=F32, precision=HIGHEST) + b_ref[...]


def _modulation(cc, w_mod, b_mod):
    depth, d, n = w_mod.shape
    tn = 1024
    return pl.pallas_call(
        _mod_kernel,
        out_shape=jax.ShapeDtypeStruct((depth, cc.shape[0], n), F32),
        grid=(depth, n // tn),
        in_specs=[pl.BlockSpec(cc.shape, lambda l, j: (0, 0)),
                  pl.BlockSpec((None, d, tn), lambda l, j: (l, 0, j)),
                  pl.BlockSpec((None, 1, tn), lambda l, j: (l, 0, j))],
        out_specs=pl.BlockSpec((None, cc.shape[0], tn), lambda l, j: (l, 0, j)),
        compiler_params=_cparams("arbitrary", "arbitrary"),
        name="modulation",
    )(cc, w_mod, b_mod.reshape(depth, 1, n))


def _mod_spec(d):
    return pl.BlockSpec((None, None, 6, d), lambda b, j: (b, jnp.minimum(j, 1), 0, 0))


def _stream_operands(xs):
    if not isinstance(xs, tuple):
        b, ta, d = xs.shape
        return (b, ta, d), [pl.BlockSpec((None, TOK_TILE, d), lambda i, j: (i, j, 0))], [xs]
    ctx, x = xs
    b, t, d = x.shape
    assert ctx.shape[1] == TOK_TILE
    return (b, TOK_TILE + t, d), [pl.BlockSpec((None, TOK_TILE, d), lambda i, j: (i, 0, 0)),
                                  pl.BlockSpec((None, TOK_TILE, d), lambda i, j: (i, jnp.maximum(j - 1, 0), 0))], [ctx, x]


def _stream_tile(refs):
    if len(refs) == 1:
        return refs[0][...]
    return jnp.where(pl.program_id(1) == 0, refs[0][...], refs[1][...])


def _prenorm_kernel(*refs, nx):
    g_ref, m_ref, o_ref = refs[nx:]
    m = m_ref[...]
    y = _rms(_stream_tile(refs[:nx])) * g_ref[...]
    o_ref[...] = (y * (1.0 + m[1:2]) + m[0:1]).astype(BF16)


def _prenorm(xs, g, modtab):
    (b, ta, d), x_specs, x_args = _stream_operands(xs)
    return pl.pallas_call(
        functools.partial(_prenorm_kernel, nx=len(x_args)),
        out_shape=jax.ShapeDtypeStruct((b, ta, d), BF16),
        grid=(b, ta // TOK_TILE),
        in_specs=x_specs + [pl.BlockSpec((1, d), lambda i, j: (0, 0)), _mod_spec(d)],
        out_specs=pl.BlockSpec((None, TOK_TILE, d), lambda i, j: (i, j, 0)),
        compiler_params=_cparams("parallel", "parallel"),
        name="prenorm",
    )(*x_args, g.reshape(1, d), modtab)


def _mm_kernel(a_ref, w_ref, o_ref, wb_ref):
    @pl.when(pl.program_id(1) == 0)
    def _():
        wb_ref[...] = w_ref[...].astype(BF16)

    o_ref[...] = _dot(a_ref[...], wb_ref[...]).astype(o_ref.dtype)


def _matmul(a, w, tm=512, tn=1024, out_dtype=F32):
    m, k = a.shape
    n = w.shape[1]
    return pl.pallas_call(
        _mm_kernel,
        out_shape=jax.ShapeDtypeStruct((m, n), out_dtype),
        grid=(n // tn, m // tm),
        in_specs=[pl.BlockSpec((tm, k), lambda j, i: (i, 0)),
                  pl.BlockSpec((k, tn), lambda j, i: (0, j))],
        out_specs=pl.BlockSpec((tm, tn), lambda j, i: (i, j)),
        scratch_shapes=[pltpu.VMEM((k, tn), BF16)],
        compiler_params=_cparams("arbitrary", "arbitrary"),
        name="proj_in",
    )(a, w)


def _hgrn_masks(chunk):
    row = lax.broadcasted_iota(jnp.int32, (chunk, LANE), 0)
    ti = lax.broadcasted_iota(jnp.int32, (chunk, chunk), 0)
    si = lax.broadcasted_iota(jnp.int32, (chunk, chunk), 1)
    levels = [(((row >> lvl) & 1) == 1, (ti >> (lvl + 1)) == (si >> (lvl + 1))) for lvl in range(chunk.bit_length() - 1)]
    return ti == si, levels


def _hgrn_chunk(q, fr, v, lb, st, rev, masks):
    f = lb + (1.0 - lb) * jax.nn.sigmoid(fr)
    kk = 1.0 - f
    lf = jnp.log(f)
    chunk = q.shape[0]
    diag, levels = masks
    att = jnp.where(diag, _dot_nt(q.astype(BF16), kk.astype(BF16)), 0.0)
    p_in = lf
    r_ex = jnp.zeros_like(lf)
    tot = lf
    for lvl, (bit, same) in enumerate(levels):
        step = 1 << lvl
        up = pltpu.roll(tot, step, 0)
        dn = pltpu.roll(tot, chunk - step, 0)
        is_q = jnp.logical_not(bit) if rev else bit
        qf = jnp.where(is_q, jnp.exp(p_in) * q, 0.0).astype(BF16)
        kf = jnp.where(is_q, 0.0, jnp.exp(r_ex) * kk).astype(BF16)
        att = att + jnp.where(same, _dot_nt(qf, kf), 0.0)
        if rev:
            p_in = p_in + jnp.where(bit, 0.0, dn)
            r_ex = r_ex + jnp.where(bit, up, 0.0)
        else:
            p_in = p_in + jnp.where(bit, up, 0.0)
            r_ex = r_ex + jnp.where(bit, 0.0, dn)
        tot = tot + jnp.where(bit, up, dn)
    vb = v.astype(BF16)
    o = _dot_nt((q * jnp.exp(p_in)).astype(BF16), st.astype(BF16)) + _dot(att.astype(BF16), vb)
    kd = (kk * jnp.exp(r_ex)).astype(BF16)
    st_new = st * jnp.exp(tot[0:1]) + _dot_tn(vb, kd)
    return o, st_new


def _hgrn_kernel(qf_ref, ff_ref, vf_ref, qb_ref, fb_ref, vb_ref, lb_ref, of_ref, ob_ref, st_ref, *, chunk, hb):
    @pl.when(pl.program_id(2) == 0)
    def _():
        st_ref[...] = jnp.zeros_like(st_ref)

    masks = _hgrn_masks(chunk)
    for h in range(hb):
        sl = slice(h * HEAD, (h + 1) * HEAD)
        o, st = _hgrn_chunk(qf_ref[:, sl], ff_ref[:, sl], vf_ref[:, sl], lb_ref[0:1, sl], st_ref[0, h], False, masks)
        of_ref[:, sl] = o
        st_ref[0, h] = st
        o, st = _hgrn_chunk(qb_ref[:, sl], fb_ref[:, sl], vb_ref[:, sl], lb_ref[1:2, sl], st_ref[1, h], True, masks)
        ob_ref[:, sl] = o
        st_ref[1, h] = st


def _hgrn(p, lb, ctx_len, width, hb=4):
    b, ta, _ = p.shape
    chunk = A_CHUNK
    n = ta // chunk
    nc = ctx_len // chunk
    cw = HEAD * hb
    nh = width // cw
    bwd = lambda j: jnp.where(j < nc, nc - 1 - j, n - 1 + nc - j)

    def spec(base, rev):
        return pl.BlockSpec((None, chunk, cw), lambda i, h, j: (i, bwd(j) if rev else j, base // cw + h))

    out = jax.ShapeDtypeStruct((b, ta, width), F32)
    return pl.pallas_call(
        functools.partial(_hgrn_kernel, chunk=chunk, hb=hb),
        out_shape=(out, out),
        grid=(b, nh, n),
        in_specs=[spec(0, False), spec(width, False), spec(3 * width, False),
                  spec(0, True), spec(2 * width, True), spec(3 * width, True),
                  pl.BlockSpec((2, cw), lambda i, h, j: (0, h))],
        out_specs=(spec(0, False), spec(0, True)),
        scratch_shapes=[pltpu.VMEM((2, hb, HEAD, HEAD), F32)],
        compiler_params=_cparams("parallel", "parallel", "arbitrary"),
        name="hgrn_scan",
    )(p, p, p, p, p, p, lb)


def _hgrn_readout_kernel(of_ref, ob_ref, g_ref, gn_ref, o_ref):
    for h in range(o_ref.shape[-1] // HEAD):
        sl = slice(h * HEAD, (h + 1) * HEAD)
        o = of_ref[:, sl] + ob_ref[:, sl]
        o_ref[:, sl] = (_rms(o) * gn_ref[...] * _silu(g_ref[:, sl])).astype(BF16)


def _hgrn_readout(o_f, o_b, p, g_norm, width):
    b, ta, _ = o_f.shape
    spec = pl.BlockSpec((None, TOK_TILE, width), lambda i, j: (i, j, 0))
    return pl.pallas_call(
        _hgrn_readout_kernel,
        out_shape=jax.ShapeDtypeStruct((b, ta, width), BF16),
        grid=(b, ta // TOK_TILE),
        in_specs=[spec, spec,
                  pl.BlockSpec((None, TOK_TILE, width), lambda i, j: (i, j, 4)),
                  pl.BlockSpec((1, HEAD), lambda i, j: (0, 0))],
        out_specs=spec,
        compiler_params=_cparams("parallel", "parallel"),
        name="hgrn_readout",
    )(o_f, o_b, p, g_norm.reshape(1, HEAD))


def _natten_bias(rpb):
    nrow, ncol = 2 * NA_ROWS - 1, 2 * NA_COLS - 1
    col = jnp.arange(GRID_W)
    col_start = jnp.clip(col - NA_COLS // 2, 0, GRID_W - NA_COLS)
    cmask = (col[None, :] >= col_start[:, None]) & (col[None, :] < col_start[:, None] + NA_COLS)
    dc = col[None, :] - col[:, None] + NA_COLS - 1
    cm = (cmask[:, :, None] & (dc[:, :, None] == jnp.arange(ncol))).astype(F32)
    cls = jnp.arange(3)[:, None, None]
    i = jnp.arange(NA_GROUP)[None, :, None]
    w = jnp.arange(NA_SPAN)[None, None, :]
    first_w = jnp.where(cls == 0, 0, jnp.where(cls == 1, i, NA_SPAN - NA_ROWS))
    valid = (w >= first_w) & (w < first_w + NA_ROWS)
    span0 = jnp.where(cls == 0, 0, jnp.where(cls == 1, -(NA_ROWS // 2), NA_GROUP - NA_SPAN))
    dr = span0 + w - i + NA_ROWS - 1
    rm = (valid[..., None] & (dr[..., None] == jnp.arange(nrow))).astype(F32)
    t = jnp.einsum('qkb,hab->haqk', cm, rpb.astype(F32), precision=HIGHEST)
    t = jnp.einsum('ciwa,haqk->hciqwk', rm, t, precision=HIGHEST)
    ok = valid[None, :, :, None, :, None] & cmask[None, None, None, :, None, :]
    t = jnp.where(ok, t, NEG_BIG)
    return t.reshape(rpb.shape[0], 3, NA_GROUP * GRID_W, NA_SPAN * GRID_W)


def _natten_kernel(q_ref, k_ref, v_ref, bias_ref, o_ref, *, ctx_len, rows):
    j = pl.program_id(2)
    nst = rows // NA_GROUP
    scale = HEAD ** -0.5
    heads = [slice(h * HEAD, (h + 1) * HEAD) for h in range(q_ref.shape[-1] // HEAD)]

    def ctx_scores(sl):
        qb = q_ref[:, sl].astype(BF16)
        vc = v_ref[0:ctx_len, sl].astype(BF16)
        s_ctx = _dot_nt(qb, k_ref[0:ctx_len, sl].astype(BF16)) * scale
        return qb, vc, s_ctx, jnp.max(s_ctx, axis=1, keepdims=True)

    @pl.when(j < nst)
    def _():
        last = j == nst - 1
        cls = jnp.where(j == 0, 0, jnp.where(last, 2, 1))
        span_row = jnp.where(j == 0, 0, jnp.where(last, rows - NA_SPAN, NA_GROUP * j - NA_ROWS // 2))
        start = pl.multiple_of(ctx_len + span_row * GRID_W, GRID_W)
        for h, sl in enumerate(heads):
            qb, vc, s_ctx, m_ctx = ctx_scores(sl)
            kw = k_ref[pl.ds(start, NA_SPAN * GRID_W), sl].astype(BF16)
            vw = v_ref[pl.ds(start, NA_SPAN * GRID_W), sl].astype(BF16)
            s_win = _dot_nt(qb, kw) * scale + bias_ref[h, cls]
            m = jnp.maximum(m_ctx, jnp.max(s_win, axis=1, keepdims=True))
            pw = jnp.exp(s_win - m)
            pc = jnp.exp(s_ctx - m)
            den = jnp.sum(pw, axis=1, keepdims=True) + jnp.sum(pc, axis=1, keepdims=True)
            o = _dot(pw.astype(BF16), vw) + _dot(pc.astype(BF16), vc)
            o_ref[:, sl] = (o / den).astype(BF16)

    @pl.when(j >= nst)
    def _():
        for sl in heads:
            _, vc, s_ctx, m_ctx = ctx_scores(sl)
            pc = jnp.exp(s_ctx - m_ctx)
            o = _dot(pc.astype(BF16), vc) / jnp.sum(pc, axis=1, keepdims=True)
            o_ref[:, sl] = o.astype(BF16)


def _natten(p, bias, ctx_len, col0, width, hb=2):
    b, ta, _ = p.shape
    cw = HEAD * hb
    nh = width // cw
    rows = (ta - ctx_len) // GRID_W
    tq = NA_GROUP * GRID_W
    nst = rows // NA_GROUP
    ncq = ctx_len // tq
    assert rows % NA_GROUP == 0 and rows >= NA_SPAN and ctx_len % tq == 0

    def qmap(cb):
        return lambda i, h, j: (i, jnp.where(j < nst, ncq + j, j - nst), cb + h)

    return pl.pallas_call(
        functools.partial(_natten_kernel, ctx_len=ctx_len, rows=rows),
        out_shape=jax.ShapeDtypeStruct((b, ta, width), BF16),
        grid=(b, nh, nst + ncq),
        in_specs=[pl.BlockSpec((None, tq, cw), qmap(col0 // cw)),
                  pl.BlockSpec((None, ta, cw), lambda i, h, j: (i, 0, (col0 + width) // cw + h)),
                  pl.BlockSpec((None, ta, cw), lambda i, h, j: (i, 0, (col0 + 2 * width) // cw + h)),
                  pl.BlockSpec((hb, 3, tq, NA_SPAN * GRID_W), lambda i, h, j: (h, 0, 0, 0))],
        out_specs=pl.BlockSpec((None, tq, cw), qmap(0)),
        compiler_params=_cparams("parallel", "parallel", "arbitrary"),
        name="natten",
    )(p, p, p, bias)


def _rope_tables(ctx_len, t):
    pos = jnp.arange(t)
    row = (pos // GRID_W).astype(F32)
    col = (pos % GRID_W).astype(F32)
    half = HEAD // 2
    inv = ROPE_THETA ** (-jnp.arange(0, half, 2, dtype=F32) / half)
    ang = jnp.concatenate([row[:, None] * inv, col[:, None] * inv], axis=-1)
    cos, sin = jnp.cos(ang), jnp.sin(ang)
    cosf = jnp.repeat(cos, 2, axis=-1)
    sinf = jnp.stack([-sin, sin], axis=-1).reshape(t, HEAD)
    cosf = jnp.concatenate([jnp.ones((ctx_len, HEAD), F32), cosf], axis=0)
    sinf = jnp.concatenate([jnp.zeros((ctx_len, HEAD), F32), sinf], axis=0)
    return cosf, sinf


def _rope(x, cosf, sinf):
    even = (lax.broadcasted_iota(jnp.int32, x.shape, 1) & 1) == 0
    partner = jnp.where(even, pltpu.roll(x, LANE - 1, 1), pltpu.roll(x, 1, 1))
    return x * cosf + partner * sinf


def _attn_kernel(q_ref, k_ref, v_ref, cq_ref, sq_ref, ck_ref, sk_ref, qn_ref, kn_ref, o_ref, kb_ref, vb_ref,
                 *, skip_tiles):
    j = pl.program_id(2)

    @pl.when(j == 0)
    def _():
        kb_ref[...] = _rope(_rms(k_ref[...]) * kn_ref[...], ck_ref[...], sk_ref[...]).astype(BF16)
        vb_ref[...] = v_ref[...].astype(BF16)

    @pl.when(j < skip_tiles)
    def _():
        o_ref[...] = jnp.zeros_like(o_ref)

    @pl.when(j >= skip_tiles)
    def _():
        k = kb_ref[...]
        v = vb_ref[...]
        cq = cq_ref[...]
        sq = sq_ref[...]
        for g in range(q_ref.shape[-1] // HEAD):
            sl = slice(g * HEAD, (g + 1) * HEAD)
            q = (_rope(_rms(q_ref[:, sl]) * qn_ref[...], cq, sq) * (HEAD ** -0.5 * LOG2E)).astype(BF16)
            s = _dot_nt(q, k)
            p = jnp.exp2(s - jnp.max(s, axis=1, keepdims=True))
            o = _dot(p.astype(BF16), v) / jnp.sum(p, axis=1, keepdims=True)
            o_ref[:, sl] = o.astype(BF16)


def _gqa_attention(p, cosf, sinf, q_norm, k_norm, qw, kw, skip_tiles):
    b, ta, _ = p.shape
    nkv = kw // HEAD
    tq = TOK_TILE
    gw = qw // nkv
    tab_q = pl.BlockSpec((tq, HEAD), lambda i, h, j: (j, 0))
    tab_k = pl.BlockSpec((ta, HEAD), lambda i, h, j: (0, 0))
    vec = pl.BlockSpec((1, HEAD), lambda i, h, j: (0, 0))
    return pl.pallas_call(
        functools.partial(_attn_kernel, skip_tiles=skip_tiles),
        out_shape=jax.ShapeDtypeStruct((b, ta, qw), BF16),
        grid=(b, nkv, ta // tq),
        in_specs=[pl.BlockSpec((None, tq, gw), lambda i, h, j: (i, j, h)),
                  pl.BlockSpec((None, ta, HEAD), lambda i, h, j: (i, 0, qw // HEAD + h)),
                  pl.BlockSpec((None, ta, HEAD), lambda i, h, j: (i, 0, (qw + kw) // HEAD + h)),
                  tab_q, tab_q, tab_k, tab_k, vec, vec],
        out_specs=pl.BlockSpec((None, tq, gw), lambda i, h, j: (i, j, h)),
        scratch_shapes=[pltpu.VMEM((ta, HEAD), BF16), pltpu.VMEM((ta, HEAD), BF16)],
        compiler_params=_cparams("parallel", "parallel", "arbitrary"),
        name="gqa_attention",
    )(p, p, p, cosf, sinf, cosf, sinf, q_norm.reshape(1, HEAD), k_norm.reshape(1, HEAD))


def _s5_operators(a_re, a_im, log_dt, b_re, b_im, c_re, c_im):
    lc, pp, ns = S5_CHUNK, S5_GROUP, S5_STATE
    a_c = lax.complex(a_re.astype(F32), a_im.astype(F32))
    adt = a_c * jnp.exp(log_dt.astype(F32))[..., None]
    a_bar = jnp.exp(adt)
    b_bar = ((a_bar - 1.0) / a_c)[..., None] * lax.complex(b_re.astype(F32), b_im.astype(F32))
    c_mat = lax.complex(c_re.astype(F32), c_im.astype(F32))
    nd, g = a_re.shape[0], a_re.shape[1]
    pos = jnp.arange(lc)
    cw = lc * pp
    apw = jnp.exp(adt[:, :, None, :] * pos.astype(F32)[None, None, :, None])
    kern = jnp.real(jnp.einsum('dgpn,dgtn,dgnq->dgqtp', c_mat, apw, b_bar, precision=HIGHEST))
    kk = kern[0].reshape(g, pp, cw)
    kkr = jnp.flip(kern[1], axis=2).reshape(g, pp, cw)
    fwd = jnp.stack([jnp.pad(kk[..., :cw - pp * i], ((0, 0), (0, 0), (pp * i, 0))) for i in range(lc)], axis=1)
    bwd = jnp.stack([jnp.pad(kkr[..., pp * (lc - 1 - i):], ((0, 0), (0, 0), (0, pp * (lc - 1 - i)))) for i in range(lc)],
                    axis=1)
    tmat = jnp.stack([fwd, bwd], axis=1).reshape(g, nd, cw, cw)
    steps_after = jnp.stack([lc - 1 - pos, pos]).astype(F32)
    gm = jnp.exp(adt[:, :, None, :] * steps_after[:, None, :, None])[:, :, :, None, :] * jnp.swapaxes(b_bar, 2, 3)[:, :, None]
    gm = gm.reshape(nd, g, lc * pp, ns)
    steps_upto = jnp.stack([pos + 1, lc - pos]).astype(F32)
    hm = c_mat[:, :, None] * jnp.exp(adt[:, :, None, :] * steps_upto[:, None, :, None])[:, :, :, None, :]
    hm = jnp.transpose(hm, (0, 1, 4, 2, 3)).reshape(nd, g, ns, lc * pp)
    hmat = jnp.concatenate([jnp.real(hm), -jnp.imag(hm)], axis=2)
    al = jnp.exp(adt * float(lc))
    al = al[:, :, None, :]
    gd = lambda x: jnp.swapaxes(x, 0, 1)
    return (tmat.astype(BF16), gd(jnp.real(gm)).astype(BF16), gd(jnp.imag(gm)).astype(BF16),
            gd(hmat).astype(BF16), gd(jnp.real(al)), gd(jnp.imag(al)))


def _s5_kernel(u_ref, t_ref, gr_ref, gi_ref, h_ref, ar_ref, ai_ref, d_ref, y_ref, er_ref, ei_ref, xr_ref, xi_ref,
               *, nchunk, nctx, rpc):
    ns = S5_STATE
    u = u_ref[...]
    for dr in range(2):
        er_ref[dr] = _dot(u, gr_ref[dr])
        ei_ref[dr] = _dot(u, gi_ref[dr])

    coef = [(jnp.broadcast_to(ar_ref[dr], (rpc, ns)), jnp.broadcast_to(ai_ref[dr], (rpc, ns))) for dr in range(2)]

    def body(c, carry):
        out = []
        for dr in range(2):
            xr, xi = carry[2 * dr], carry[2 * dr + 1]
            pc = c if dr == 0 else jnp.where(c < nctx, nctx - 1 - c, nchunk - 1 + nctx - c)
            off = pl.multiple_of(pc * rpc, rpc)
            xr_ref[dr, pl.ds(off, rpc), :] = xr
            xi_ref[dr, pl.ds(off, rpc), :] = xi
            ar, ai = coef[dr]
            out.append(ar * xr - ai * xi + er_ref[dr, pl.ds(off, rpc), :])
            out.append(ar * xi + ai * xr + ei_ref[dr, pl.ds(off, rpc), :])
        return tuple(out)

    zero = jnp.zeros((rpc, ns), F32)
    lax.fori_loop(0, nchunk, body, (zero, zero, zero, zero), unroll=4)
    y = d_ref[...] * u.astype(F32)
    for dr in range(2):
        y = y + _dot(u, t_ref[dr])
        y = y + _dot(xr_ref[dr].astype(BF16), h_ref[dr, 0:ns, :]) + _dot(xi_ref[dr].astype(BF16), h_ref[dr, ns:2 * ns, :])
    y_ref[...] = y.astype(BF16)


def _group_perm(lc, pp, ngl):
    n = lc * ngl * pp
    src = jnp.arange(n)
    dst = (src // pp) % ngl * (lc * pp) + src // (ngl * pp) * pp + src % pp
    return (dst[:, None] == jnp.arange(n)[None, :]).astype(BF16)


def _s5_pack_kernel(u_ref, perm_ref, o_ref, *, lc, width):
    ngl = LANE // S5_GROUP
    cw = o_ref.shape[-1]
    for gb in range(width // LANE):
        lhs = jnp.concatenate([u_ref[:, i * width + gb * LANE:i * width + (gb + 1) * LANE] for i in range(lc)], axis=1)
        res = _dot(lhs, perm_ref[...]).astype(BF16)
        for gl in range(ngl):
            o_ref[gb * ngl + gl] = res[:, gl * cw:(gl + 1) * cw]


def _s5_unpack_glu_kernel(y_ref, perm_ref, w_ref, b_ref, o_ref, *, lc, width):
    ngl = LANE // S5_GROUP
    pieces = []
    for gb in range(width // LANE):
        lhs = jnp.concatenate([y_ref[gb * ngl + gl] for gl in range(ngl)], axis=1)
        pieces.append(_dot(lhs, perm_ref[...]))
    for i in range(lc):
        y = jnp.concatenate([p[:, i * LANE:(i + 1) * LANE] for p in pieces], axis=1)
        y = 0.5 * y * (1.0 + jnp.tanh(math.sqrt(2.0 / math.pi) * (y + 0.044715 * (y * y * y))))
        z = _dot(y.astype(BF16), w_ref[...]) + b_ref[...]
        o_ref[:, i * width:(i + 1) * width] = (y * jax.nn.sigmoid(z)).astype(BF16)


def _s5(p, ucol, width, ctx_len, ops, d_skip, w_glu, b_glu):
    tmat, g_re, g_im, hmat, a_re, a_im = ops
    b, ta, _ = p.shape
    lc, pp, ns = S5_CHUNK, S5_GROUP, S5_STATE
    g = width // pp
    ngl = LANE // pp
    nchunk = ta // lc
    rpc = 8
    cw = lc * pp
    rows = nchunk * rpc
    perm = _group_perm(lc, pp, ngl)
    nperm = perm.shape[0]
    u2 = p[..., ucol:ucol + width].astype(BF16).reshape(b, nchunk, lc * width)
    ug = pl.pallas_call(
        functools.partial(_s5_pack_kernel, lc=lc, width=width),
        out_shape=jax.ShapeDtypeStruct((g, b, nchunk, cw), BF16),
        grid=(b,),
        in_specs=[pl.BlockSpec((None, nchunk, lc * width), lambda i: (i, 0, 0)),
                  pl.BlockSpec((nperm, nperm), lambda i: (0, 0))],
        out_specs=pl.BlockSpec((g, None, nchunk, cw), lambda i: (0, i, 0, 0)),
        compiler_params=_cparams("parallel"),
        name="s5_pack",
    )(u2, perm)
    ug = jnp.pad(jnp.swapaxes(ug, 1, 2), ((0, 0), (0, 0), (0, rpc - b), (0, 0))).reshape(g, rows, cw)
    dvec = jnp.tile(d_skip.astype(F32).reshape(g, 1, pp), (1, lc, 1)).reshape(g, 1, cw)
    op = lambda r, c: pl.BlockSpec((None, 2, r, c), lambda i: (i, 0, 0, 0))
    y = pl.pallas_call(
        functools.partial(_s5_kernel, nchunk=nchunk, nctx=ctx_len // lc, rpc=rpc),
        out_shape=jax.ShapeDtypeStruct((g, rows, cw), BF16),
        grid=(g,),
        in_specs=[pl.BlockSpec((None, rows, cw), lambda i: (i, 0, 0)),
                  op(cw, cw), op(cw, ns), op(cw, ns), op(2 * ns, cw), op(1, ns), op(1, ns),
                  pl.BlockSpec((None, 1, cw), lambda i: (i, 0, 0))],
        out_specs=pl.BlockSpec((None, rows, cw), lambda i: (i, 0, 0)),
        scratch_shapes=[pltpu.VMEM((2, rows, ns), F32) for _ in range(4)],
        compiler_params=_cparams("parallel"),
        name="s5_scan",
    )(ug, tmat, g_re, g_im, hmat, a_re, a_im, dvec)
    y = jnp.swapaxes(y.reshape(g, nchunk, rpc, cw)[:, :, :b], 1, 2)
    out = pl.pallas_call(
        functools.partial(_s5_unpack_glu_kernel, lc=lc, width=width),
        out_shape=jax.ShapeDtypeStruct((b, nchunk, lc * width), BF16),
        grid=(b,),
        in_specs=[pl.BlockSpec((g, None, nchunk, cw), lambda i: (0, i, 0, 0)),
                  pl.BlockSpec((nperm, nperm), lambda i: (0, 0)),
                  pl.BlockSpec((width, width), lambda i: (0, 0)),
                  pl.BlockSpec((1, width), lambda i: (0, 0))],
        out_specs=pl.BlockSpec((None, nchunk, lc * width), lambda i: (i, 0, 0)),
        compiler_params=_cparams("parallel"),
        name="s5_unpack_glu",
    )(y, perm.T, w_glu.astype(BF16), b_glu.reshape(1, width))
    return out.reshape(b, ta, width)


def _postmix_kernel(a_ref, b_ref, w_ref, *refs, nx):
    m_ref, gpost_ref, gpre_ref, wr_ref, xo_ref, h_ref, lg_ref = refs[nx:]
    wa = a_ref.shape[-1]
    y = _dot(a_ref[...], w_ref[0:wa, :]) + _dot(b_ref[...], w_ref[wa:, :])
    m = m_ref[...]
    xn = _stream_tile(refs[:nx]) + m[2:3] * (_rms(y) * gpost_ref[...])
    xo_ref[...] = xn
    h2 = _rms(xn) * gpre_ref[...] * (1.0 + m[4:5]) + m[3:4]
    hi = h2.astype(BF16)
    h_ref[...] = hi
    lo = (h2 - hi.astype(F32)).astype(BF16)
    ne = lg_ref.shape[-1]
    both = _dot(hi, wr_ref[...])
    lg_ref[...] = both[:, 0:ne] + both[:, ne:2 * ne] + _dot(lo, wr_ref[:, 0:ne])


def _postmix(mix_a, mix_b, w_out, xs, modtab, g_post, g_pre, w_router):
    (b, ta, d), x_specs, x_args = _stream_operands(xs)
    wa, wb = mix_a.shape[-1], mix_b.shape[-1]
    ne = w_router.shape[-1]
    tok = lambda w: pl.BlockSpec((None, TOK_TILE, w), lambda i, j: (i, j, 0))
    vec = pl.BlockSpec((1, d), lambda i, j: (0, 0))
    wr_hi = w_router.astype(BF16)
    wr2 = jnp.concatenate([wr_hi, (w_router - wr_hi.astype(F32)).astype(BF16)], axis=1)
    return pl.pallas_call(
        functools.partial(_postmix_kernel, nx=len(x_args)),
        out_shape=(jax.ShapeDtypeStruct((b, ta, d), F32),
                   jax.ShapeDtypeStruct((b, ta, d), BF16),
                   jax.ShapeDtypeStruct((b, ta, ne), F32)),
        grid=(b, ta // TOK_TILE),
        in_specs=[tok(wa), tok(wb),
                  pl.BlockSpec((wa + wb, d), lambda i, j: (0, 0))] + x_specs + [
                  _mod_spec(d), vec, vec,
                  pl.BlockSpec((d, 2 * ne), lambda i, j: (0, 0))],
        out_specs=(tok(d), tok(d), tok(ne)),
        compiler_params=_cparams("parallel", "parallel"),
        name="mix_out",
    )(mix_a, mix_b, w_out.astype(BF16), *x_args, modtab, g_post.reshape(1, d), g_pre.reshape(1, d), wr2)


def _select_kernel(lg_ref, pos_ref, gate_ref, *, ctx_len, cap_ctx, cap_lat):
    lg = lg_ref[...]
    ne, ta = lg.shape
    ex = jnp.exp(lg - jnp.max(lg, axis=0, keepdims=True))
    probs = ex / jnp.sum(ex, axis=0, keepdims=True)
    bits = pltpu.bitcast(probs, jnp.int32)
    is_ctx = lax.broadcasted_iota(jnp.int32, (ne, ta), 1) < ctx_len

    def counts(mask):
        mf = jnp.where(mask, 1.0, 0.0)
        return (jnp.sum(jnp.where(is_ctx, mf, 0.0), axis=1, keepdims=True),
                jnp.sum(jnp.where(is_ctx, 0.0, mf), axis=1, keepdims=True))

    def search(i, carry):
        pc, pt = carry
        bit = jnp.left_shift(jnp.int32(1), 30 - i)
        cc, ct = counts(bits >= jnp.where(is_ctx, pc | bit, pt | bit))
        return jnp.where(cc >= cap_ctx, pc | bit, pc), jnp.where(ct >= cap_lat, pt | bit, pt)

    z = jnp.zeros((ne, 1), jnp.int32)
    pc, pt = lax.fori_loop(0, 31, search, (z, z))
    thr = jnp.where(is_ctx, pc, pt)
    gt = bits > thr
    eq = bits == thr
    gc, gl = counts(gt)
    need = jnp.where(is_ctx, cap_ctx - gc, cap_lat - gl)

    nb = ta // LANE
    ut = jnp.where(lax.broadcasted_iota(jnp.int32, (LANE, LANE), 0) <= lax.broadcasted_iota(jnp.int32, (LANE, LANE), 1),
                   1.0, 0.0).astype(BF16)

    def lane_prefix(mask):
        mf = jnp.where(mask, 1.0, 0.0).astype(BF16)
        blocks = jnp.concatenate([mf[:, j * LANE:(j + 1) * LANE] for j in range(nb)], axis=0)
        inc = _dot(blocks, ut)
        outs = []
        off = jnp.zeros((ne, 1), F32)
        for j in range(nb):
            if j * LANE == ctx_len:
                off = jnp.zeros((ne, 1), F32)
            blk = inc[j * ne:(j + 1) * ne]
            outs.append(blk + off)
            off = off + blk[:, LANE - 1:LANE]
        return jnp.concatenate(outs, axis=1)

    sel = gt | (eq & (lane_prefix(eq) <= need))
    slot = lane_prefix(sel) - 1.0 + jnp.where(is_ctx, 0.0, float(cap_ctx))
    pos_ref[...] = jnp.where(sel, slot, -1.0)
    gate_ref[...] = jnp.where(sel, probs, 0.0)


def _moe_select(logits_t, ctx_len, cap_ctx, cap_lat):
    b, ne, ta = logits_t.shape
    spec = pl.BlockSpec((None, ne, ta), lambda i: (i, 0, 0))
    return pl.pallas_call(
        functools.partial(_select_kernel, ctx_len=ctx_len, cap_ctx=cap_ctx, cap_lat=cap_lat),
        out_shape=(jax.ShapeDtypeStruct((b, ne, ta), F32), jax.ShapeDtypeStruct((b, ne, ta), F32)),
        grid=(b,),
        in_specs=[spec],
        out_specs=(spec, spec),
        compiler_params=_cparams("parallel"),
        name="moe_select",
    )(logits_t)


def _slot_windows(pos_t, nt, win, nslot):
    b, ne, ta = pos_t.shape
    pt = pos_t.reshape(b, ne, nt, ta // nt)
    hi = jnp.max(pt, axis=-1).astype(jnp.int32) + 1
    lo = jnp.min(jnp.where(pt >= 0, pt, float(nslot)), axis=-1).astype(jnp.int32)
    lo = jnp.where(hi > 0, lo // 16 * 16, 0)
    npass = jnp.maximum(jnp.max((hi - lo + win - 1) // win, axis=1), 1)
    return jnp.transpose(lo, (0, 2, 1)).reshape(-1), npass.reshape(-1)


def _gather_kernel(start_ref, npass_ref, pos_ref, h_ref, o_ref, acc_ref):
    i = pl.program_id(0)
    eg = pl.program_id(1)
    j = pl.program_id(2)
    ge, tk = pos_ref.shape
    ne = pl.num_programs(1) * ge
    nslot = o_ref.shape[1]
    win = GATHER_WIN
    tile = i * pl.num_programs(2) + j

    @pl.when(j == 0)
    def _():
        acc_ref[...] = jnp.zeros_like(acc_ref)

    pos = pos_ref[...].astype(jnp.int32)
    slot = lax.broadcasted_iota(jnp.int32, (win, tk), 0)

    def one_pass(p, carry):
        src, blocks = [], []
        for el in range(ge):
            lo = start_ref[tile * ne + eg * ge + el] + p * win
            src.append(pl.multiple_of(jnp.minimum(lo, nslot - win), 16))
            pe = pos[el:el + 1, :]
            hit = (pe >= lo) & (pe < lo + win) & (slot == pe - src[el])
            blocks.append(jnp.where(hit, 1.0, 0.0).astype(BF16))
        part = _dot(jnp.concatenate(blocks, axis=0), h_ref[...])
        for el in range(ge):
            acc_ref[el, pl.ds(src[el], win), :] += part[el * win:(el + 1) * win]
        return carry

    lax.fori_loop(0, npass_ref[tile], one_pass, 0)

    @pl.when(j == pl.num_programs(2) - 1)
    def _():
        o_ref[...] = acc_ref[...].astype(BF16)


def _moe_gather(pos_t, h, nslot):
    b, ne, ta = pos_t.shape
    d = h.shape[-1]
    ge = 4
    tk = TOK_TILE
    nt = ta // tk
    start, npass = _slot_windows(pos_t, nt, GATHER_WIN, nslot)
    return pl.pallas_call(
        _gather_kernel,
        out_shape=jax.ShapeDtypeStruct((ne, b, nslot, d), BF16),
        grid_spec=pltpu.PrefetchScalarGridSpec(
            num_scalar_prefetch=2,
            grid=(b, ne // ge, nt),
            in_specs=[pl.BlockSpec((None, None, ge, tk), lambda i, g, j, *_: (i, g, 0, j)),
                      pl.BlockSpec((None, tk, d), lambda i, g, j, *_: (i, j, 0))],
            out_specs=pl.BlockSpec((ge, None, nslot, d), lambda i, g, j, *_: (g, i, 0, 0)),
            scratch_shapes=[pltpu.VMEM((ge, nslot, d), F32)]),
        compiler_params=_cparams("arbitrary", "arbitrary", "arbitrary"),
        name="moe_gather",
    )(start, npass, pos_t.reshape(b, ne // ge, ge, ta), h)


def _ffn_kernel(x_ref, wg_ref, wu_ref, wd_ref, o_ref, hid_ref, *, nf):
    s = pl.program_id(2)
    tf = wg_ref.shape[-1]

    @pl.when(s < nf)
    def _():
        x = x_ref[...]
        g = _dot(x, wg_ref[...].astype(BF16))
        u = _dot(x, wu_ref[...].astype(BF16))
        hid_ref[s] = (_silu(g) * u).astype(BF16)

    @pl.when(s >= nf)
    def _():
        acc = _dot(hid_ref[0], wd_ref[0:tf, :].astype(BF16))
        for f in range(1, nf):
            acc = acc + _dot(hid_ref[f], wd_ref[f * tf:(f + 1) * tf, :].astype(BF16))
        o_ref[...] = acc.astype(BF16)


def _moe_ffn(xin, w_gate, w_up, w_down, layer, tf=512):
    ne, ns, m, d = xin.shape
    ff = w_gate.shape[-1]
    nf = ff // tf
    nd = d // tf
    up = lambda e, s, f: (layer, e, 0, jnp.minimum(f, nf - 1))
    return pl.pallas_call(
        functools.partial(_ffn_kernel, nf=nf),
        out_shape=jax.ShapeDtypeStruct((ne, ns, m, d), BF16),
        grid=(ne, ns, nf + nd),
        in_specs=[pl.BlockSpec((None, None, m, d), lambda e, s, f: (e, s, 0, 0)),
                  pl.BlockSpec((None, None, d, tf), up),
                  pl.BlockSpec((None, None, d, tf), up),
                  pl.BlockSpec((None, None, ff, tf), lambda e, s, f: (layer, e, 0, jnp.maximum(f - nf, 0)))],
        out_specs=pl.BlockSpec((None, None, m, tf), lambda e, s, f: (e, s, 0, jnp.maximum(f - nf, 0))),
        scratch_shapes=[pltpu.VMEM((nf, m, tf), BF16)],
        compiler_params=_cparams("parallel", "parallel", "arbitrary"),
        name="moe_ffn",
    )(xin, w_gate, w_up, w_down)


def _combine_kernel(start_ref, npass_ref, pos_ref, gate_ref, y_hbm, x_ref, m_ref, g_ref, *refs, ctx_len, with_next):
    if with_next:
        gn_ref, mn_ref, o_ref, hn_ref, ybuf, acc_ref, sem = refs
    else:
        o_ref, ybuf, acc_ref, sem = refs
    i = pl.program_id(0)
    j = pl.program_id(1)
    tt, ne = pos_ref.shape
    nslot = y_hbm.shape[2]
    win = COMBINE_WIN
    tile = i * pl.num_programs(1) + j
    pos = pos_ref[...].astype(jnp.int32)
    gate = gate_ref[...]
    lane = lax.broadcasted_iota(jnp.int32, (tt, win), 1)

    def window_copy(e, src):
        return pltpu.make_async_copy(y_hbm.at[e, i, pl.ds(src, win), :], ybuf.at[pl.ds(e * win, win), :], sem.at[e])

    def one_pass(p):
        own_lo, src = [], []
        for e in range(ne):
            lo = start_ref[tile * ne + e] + p * win
            own_lo.append(lo)
            src.append(pl.multiple_of(jnp.minimum(lo, nslot - win), 16))
            window_copy(e, src[e]).start()
        blocks = []
        for e in range(ne):
            pe = pos[:, e:e + 1]
            hit = (pe >= own_lo[e]) & (pe < own_lo[e] + win) & (lane == pe - src[e])
            blocks.append(jnp.where(hit, gate[:, e:e + 1], 0.0).astype(BF16))
        w = jnp.concatenate(blocks, axis=1)
        for e in range(ne):
            window_copy(e, src[e]).wait()
        return _dot(w, ybuf[...])

    acc_ref[...] = one_pass(0)

    def extra(p, carry):
        acc_ref[...] += one_pass(p)
        return carry

    lax.fori_loop(1, npass_ref[tile], extra, 0)
    m = m_ref[...]
    is_ctx = (lax.broadcasted_iota(jnp.int32, (tt, 1), 0) + j * tt) < ctx_len
    mod_gate = jnp.where(is_ctx, m[0, 5:6], m[1, 5:6])
    xn = x_ref[...] + mod_gate * (_rms(acc_ref[...]) * g_ref[...])
    o_ref[...] = xn
    if with_next:
        mn = mn_ref[...]
        scale = jnp.where(is_ctx, mn[0, 1:2], mn[1, 1:2])
        shift = jnp.where(is_ctx, mn[0, 0:1], mn[1, 0:1])
        hn_ref[...] = (_rms(xn) * gn_ref[...] * (1.0 + scale) + shift).astype(BF16)


def _moe_combine(pos_t, gate_t, yout, xa, modtab, g_post, ctx_len, nxt=None):
    b, ta, d = xa.shape
    ne, _, nslot, _ = yout.shape
    nt = 16
    tt = ta // nt
    win = COMBINE_WIN
    start, npass = _slot_windows(pos_t, nt, win, nslot)
    tok = lambda w: pl.BlockSpec((None, tt, w), lambda i, j, *_: (i, j, 0))
    mods = pl.BlockSpec((None, 2, 6, d), lambda i, j, *_: (i, 0, 0, 0))
    vec = pl.BlockSpec((1, d), lambda i, j, *_: (0, 0))
    xo = jax.ShapeDtypeStruct((b, ta, d), F32)
    extra_in, extra_args = ([vec, mods], [nxt[0].reshape(1, d), nxt[1]]) if nxt else ([], [])
    return pl.pallas_call(
        functools.partial(_combine_kernel, ctx_len=ctx_len, with_next=bool(nxt)),
        out_shape=(xo, jax.ShapeDtypeStruct((b, ta, d), BF16)) if nxt else xo,
        grid_spec=pltpu.PrefetchScalarGridSpec(
            num_scalar_prefetch=2,
            grid=(b, nt),
            in_specs=[tok(ne), tok(ne),
                      pl.BlockSpec(memory_space=pl.ANY),
                      tok(d), mods, vec] + extra_in,
            out_specs=(tok(d), tok(d)) if nxt else tok(d),
            scratch_shapes=[pltpu.VMEM((ne * win, d), BF16),
                            pltpu.VMEM((tt, d), F32),
                            pltpu.SemaphoreType.DMA((ne,))]),
        compiler_params=_cparams("arbitrary", "arbitrary"),
        name="moe_combine",
    )(start, npass, jnp.swapaxes(pos_t, 1, 2), jnp.swapaxes(gate_t, 1, 2), yout, xa, modtab,
      g_post.reshape(1, d), *extra_args)


def _ec_moe(logits_t, h, xa, modtab, g_post, w_gate, w_up, w_down, layer, ctx_len, nxt=None):
    b, ne, ta = logits_t.shape
    d = h.shape[-1]
    cap_ctx = max(1, EC_CAPACITY * ctx_len // ne)
    cap_lat = max(1, EC_CAPACITY * (ta - ctx_len) // ne)
    nslot = cap_ctx + cap_lat
    pos_t, gate_t = _moe_select(logits_t, ctx_len, cap_ctx, cap_lat)
    xin = _moe_gather(pos_t, h, nslot)
    pair = 2 if b % 2 == 0 else 1
    yout = _moe_ffn(xin.reshape(ne, b // pair, pair * nslot, d), w_gate, w_up, w_down, layer)
    yout = yout.reshape(ne, b, nslot, d)
    return _moe_combine(pos_t, gate_t, yout, xa, modtab, g_post, ctx_len, nxt)


def kernel(x, c, ctx, c_ctx, w_mod, b_mod, g_mix_pre, g_mix_post, g_ffn_pre, g_ffn_post, w_router, w_exp_gate, w_exp_up, w_exp_down, ev_w_in, ev_w_out, hgrn_lb, hgrn_g_norm, na_rpb, od_w_in, od_w_out, q_norm, k_norm, s5_a_re, s5_a_im, s5_log_dt, s5_b_re, s5_b_im, s5_c_re, s5_c_im, s5_d, s5_w_glu, s5_b_glu):
    b, t, d = x.shape
    ctx_len = ctx.shape[1]
    depth = w_mod.shape[0]
    assert depth == 2 and b <= 7
    ta = ctx_len + t
    a_width = d // 2
    s5_width = d // 4
    cq_width = d - s5_width
    ckv_width = cq_width // 3

    cc = jnp.concatenate([c, c_ctx[None], jnp.zeros((7 - b, d), F32)], axis=0)
    mod = _modulation(cc, w_mod, b_mod)
    mod_lat = mod[:, :b].reshape(depth, b, 1, 6, d)
    mod_ctx = jnp.broadcast_to(mod[:, b].reshape(depth, 1, 1, 6, d), (depth, b, 1, 6, d))
    modtab = jnp.concatenate([mod_ctx, mod_lat], axis=2)

    lb_all = jnp.cumsum(jax.nn.softmax(hgrn_lb.astype(F32), axis=0), axis=0)
    hx = _prenorm((ctx, x), g_mix_pre[0], modtab[0])
    p = _matmul(hx.reshape(b * ta, d), ev_w_in[0]).reshape(b, ta, -1)
    o_f, o_b = _hgrn(p, lb_all[0], ctx_len, a_width)
    mix_a = _hgrn_readout(o_f, o_b, p, hgrn_g_norm[0], a_width)
    mix_b = _natten(p, _natten_bias(na_rpb[0]), ctx_len, 5 * a_width, d - a_width)
    xa, h2, logits = _postmix(mix_a, mix_b, ev_w_out[0], (ctx, x), modtab[0], g_mix_post[0], g_ffn_pre[0], w_router[0])
    xa, hx = _ec_moe(jnp.swapaxes(logits, 1, 2), h2, xa, modtab[0], g_ffn_post[0], w_exp_gate, w_exp_up, w_exp_down, 0,
                     ctx_len, nxt=(g_mix_pre[1], modtab[1]))

    p = _matmul(hx.reshape(b * ta, d), od_w_in[0]).reshape(b, ta, -1)
    cosf, sinf = _rope_tables(ctx_len, t)
    mix_a = _gqa_attention(p, cosf, sinf, q_norm[0], k_norm[0], cq_width, ckv_width, ctx_len // TOK_TILE)
    ops = _s5_operators(s5_a_re[0], s5_a_im[0], s5_log_dt[0], s5_b_re[0], s5_b_im[0], s5_c_re[0], s5_c_im[0])
    mix_b = _s5(p, cq_width + 2 * ckv_width, s5_width, ctx_len, ops, s5_d[0], s5_w_glu[0], s5_b_glu[0])
    xa, h2, logits = _postmix(mix_a, mix_b, od_w_out[0], xa, modtab[1], g_mix_post[1], g_ffn_pre[1], w_router[1])
    xa = _ec_moe(jnp.swapaxes(logits, 1, 2), h2, xa, modtab[1], g_ffn_post[1], w_exp_gate, w_exp_up, w_exp_down, 1, ctx_len)
    return xa[:, ctx_len:]
```

```python
import functools
import math

import jax
import jax.numpy as jnp
from jax import lax
from jax.experimental import pallas as pl
from jax.experimental.pallas import tpu as pltpu

F32 = jnp.float32
BF16 = jnp.bfloat16
HIGHEST = lax.Precision.HIGHEST
EPS = 1e-6

LANE = 128
TOK_TILE = 256
VMEM_LIMIT = 52 << 20

GRID_W = 64
NA_ROWS = 8
NA_COLS = 16
NA_GROUP = 4
NA_SPAN = 12
HEAD = 128
A_CHUNK = 64
N_EXPERTS = 16
EC_CAPACITY = 2
S5_GROUP = 16
S5_STATE = 64
S5_CHUNK = 16
COMBINE_WIN = 64
GATHER_WIN = 64
ROPE_THETA = 10000.0
NEG_BIG = -1e30
LOG2E = 1.4426950408889634


def _cparams(*sem):
    return pltpu.CompilerParams(dimension_semantics=sem, vmem_limit_bytes=VMEM_LIMIT)


def _dot(a, b):
    return jnp.dot(a, b, preferred_element_type=F32)


def _dot_nt(a, b):
    return lax.dot_general(a, b, (((1,), (1,)), ((), ())), preferred_element_type=F32)


def _dot_tn(a, b):
    return lax.dot_general(a, b, (((0,), (0,)), ((), ())), preferred_element_type=F32)


def _rms(x):
    return x * lax.rsqrt(jnp.mean(x * x, axis=-1, keepdims=True) + EPS)


def _silu(x):
    return x * jax.nn.sigmoid(x)


def _mod_kernel(c_ref, w_ref, b_ref, o_ref):
    o_ref[...] = jnp.dot(_silu(c_ref[...]), w_ref[...], preferred_element_type=F32, precision=HIGHEST) + b_ref[...]


def _modulation(cc, w_mod, b_mod):
    depth, d, n = w_mod.shape
    tn = 1024
    return pl.pallas_call(
        _mod_kernel,
        out_shape=jax.ShapeDtypeStruct((depth, cc.shape[0], n), F32),
        grid=(depth, n // tn),
        in_specs=[pl.BlockSpec(cc.shape, lambda l, j: (0, 0)),
                  pl.BlockSpec((None, d, tn), lambda l, j: (l, 0, j)),
                  pl.BlockSpec((None, 1, tn), lambda l, j: (l, 0, j))],
        out_specs=pl.BlockSpec((None, cc.shape[0], tn), lambda l, j: (l, 0, j)),
        compiler_params=_cparams("arbitrary", "arbitrary"),
        name="modulation",
    )(cc, w_mod, b_mod.reshape(depth, 1, n))


def _mod_spec(d):
    return pl.BlockSpec((None, None, 6, d), lambda b, j: (b, jnp.minimum(j, 1), 0, 0))


def _stream_operands(xs):
    if not isinstance(xs, tuple):
        b, ta, d = xs.shape
        return (b, ta, d), [pl.BlockSpec((None, TOK_TILE, d), lambda i, j: (i, j, 0))], [xs]
    ctx, x = xs
    b, t, d = x.shape
    assert ctx.shape[1] == TOK_TILE
    return (b, TOK_TILE + t, d), [pl.BlockSpec((None, TOK_TILE, d), lambda i, j: (i, 0, 0)),
                                  pl.BlockSpec((None, TOK_TILE, d), lambda i, j: (i, jnp.maximum(j - 1, 0), 0))], [ctx, x]


def _stream_tile(refs):
    if len(refs) == 1:
        return refs[0][...]
    return jnp.where(pl.program_id(1) == 0, refs[0][...], refs[1][...])


def _prenorm_kernel(*refs, nx):
    g_ref, m_ref, o_ref = refs[nx:]
    m = m_ref[...]
    y = _rms(_stream_tile(refs[:nx])) * g_ref[...]
    o_ref[...] = (y * (1.0 + m[1:2]) + m[0:1]).astype(BF16)


def _prenorm(xs, g, modtab):
    (b, ta, d), x_specs, x_args = _stream_operands(xs)
    return pl.pallas_call(
        functools.partial(_prenorm_kernel, nx=len(x_args)),
        out_shape=jax.ShapeDtypeStruct((b, ta, d), BF16),
        grid=(b, ta // TOK_TILE),
        in_specs=x_specs + [pl.BlockSpec((1, d), lambda i, j: (0, 0)), _mod_spec(d)],
        out_specs=pl.BlockSpec((None, TOK_TILE, d), lambda i, j: (i, j, 0)),
        compiler_params=_cparams("parallel", "parallel"),
        name="prenorm",
    )(*x_args, g.reshape(1, d), modtab)


def _mm_kernel(a_ref, w_ref, o_ref, wb_ref):
    @pl.when(pl.program_id(1) == 0)
    def _():
        wb_ref[...] = w_ref[...].astype(BF16)

    o_ref[...] = _dot(a_ref[...], wb_ref[...]).astype(o_ref.dtype)


def _matmul(a, w, tm=512, tn=1024, out_dtype=F32):
    m, k = a.shape
    n = w.shape[1]
    return pl.pallas_call(
        _mm_kernel,
        out_shape=jax.ShapeDtypeStruct((m, n), out_dtype),
        grid=(n // tn, m // tm),
        in_specs=[pl.BlockSpec((tm, k), lambda j, i: (i, 0)),
                  pl.BlockSpec((k, tn), lambda j, i: (0, j))],
        out_specs=pl.BlockSpec((tm, tn), lambda j, i: (i, j)),
        scratch_shapes=[pltpu.VMEM((k, tn), BF16)],
        compiler_params=_cparams("arbitrary", "arbitrary"),
        name="proj_in",
    )(a, w)


def _hgrn_masks(chunk):
    row = lax.broadcasted_iota(jnp.int32, (chunk, LANE), 0)
    ti = lax.broadcasted_iota(jnp.int32, (chunk, chunk), 0)
    si = lax.broadcasted_iota(jnp.int32, (chunk, chunk), 1)
    levels = [(((row >> lvl) & 1) == 1, (ti >> (lvl + 1)) == (si >> (lvl + 1))) for lvl in range(chunk.bit_length() - 1)]
    return ti == si, levels


def _hgrn_chunk(q, fr, v, lb, st, rev, masks):
    f = lb + (1.0 - lb) * jax.nn.sigmoid(fr)
    kk = 1.0 - f
    lf = jnp.log(f)
    chunk = q.shape[0]
    diag, levels = masks
    att = jnp.where(diag, _dot_nt(q.astype(BF16), kk.astype(BF16)), 0.0)
    p_in = lf
    r_ex = jnp.zeros_like(lf)
    tot = lf
    for lvl, (bit, same) in enumerate(levels):
        step = 1 << lvl
        up = pltpu.roll(tot, step, 0)
        dn = pltpu.roll(tot, chunk - step, 0)
        is_q = jnp.logical_not(bit) if rev else bit
        qf = jnp.where(is_q, jnp.exp(p_in) * q, 0.0).astype(BF16)
        kf = jnp.where(is_q, 0.0, jnp.exp(r_ex) * kk).astype(BF16)
        att = att + jnp.where(same, _dot_nt(qf, kf), 0.0)
        if rev:
            p_in = p_in + jnp.where(bit, 0.0, dn)
            r_ex = r_ex + jnp.where(bit, up, 0.0)
        else:
            p_in = p_in + jnp.where(bit, up, 0.0)
            r_ex = r_ex + jnp.where(bit, 0.0, dn)
        tot = tot + jnp.where(bit, up, dn)
    vb = v.astype(BF16)
    o = _dot_nt((q * jnp.exp(p_in)).astype(BF16), st.astype(BF16)) + _dot(att.astype(BF16), vb)
    kd = (kk * jnp.exp(r_ex)).astype(BF16)
    st_new = st * jnp.exp(tot[0:1]) + _dot_tn(vb, kd)
    return o, st_new


def _hgrn_kernel(qf_ref, ff_ref, vf_ref, qb_ref, fb_ref, vb_ref, lb_ref, of_ref, ob_ref, st_ref, *, chunk, hb):
    @pl.when(pl.program_id(2) == 0)
    def _():
        st_ref[...] = jnp.zeros_like(st_ref)

    masks = _hgrn_masks(chunk)
    for h in range(hb):
        sl = slice(h * HEAD, (h + 1) * HEAD)
        o, st = _hgrn_chunk(qf_ref[:, sl], ff_ref[:, sl], vf_ref[:, sl], lb_ref[0:1, sl], st_ref[0, h], False, masks)
        of_ref[:, sl] = o
        st_ref[0, h] = st
        o, st = _hgrn_chunk(qb_ref[:, sl], fb_ref[:, sl], vb_ref[:, sl], lb_ref[1:2, sl], st_ref[1, h], True, masks)
        ob_ref[:, sl] = o
        st_ref[1, h] = st


def _hgrn(p, lb, ctx_len, width, hb=4):
    b, ta, _ = p.shape
    chunk = A_CHUNK
    n = ta // chunk
    nc = ctx_len // chunk
    cw = HEAD * hb
    nh = width // cw
    bwd = lambda j: jnp.where(j < nc, nc - 1 - j, n - 1 + nc - j)

    def spec(base, rev):
        return pl.BlockSpec((None, chunk, cw), lambda i, h, j: (i, bwd(j) if rev else j, base // cw + h))

    out = jax.ShapeDtypeStruct((b, ta, width), F32)
    return pl.pallas_call(
        functools.partial(_hgrn_kernel, chunk=chunk, hb=hb),
        out_shape=(out, out),
        grid=(b, nh, n),
        in_specs=[spec(0, False), spec(width, False), spec(3 * width, False),
                  spec(0, True), spec(2 * width, True), spec(3 * width, True),
                  pl.BlockSpec((2, cw), lambda i, h, j: (0, h))],
        out_specs=(spec(0, False), spec(0, True)),
        scratch_shapes=[pltpu.VMEM((2, hb, HEAD, HEAD), F32)],
        compiler_params=_cparams("parallel", "parallel", "arbitrary"),
        name="hgrn_scan",
    )(p, p, p, p, p, p, lb)


def _hgrn_readout_kernel(of_ref, ob_ref, g_ref, gn_ref, o_ref):
    for h in range(o_ref.shape[-1] // HEAD):
        sl = slice(h * HEAD, (h + 1) * HEAD)
        o = of_ref[:, sl] + ob_ref[:, sl]
        o_ref[:, sl] = (_rms(o) * gn_ref[...] * _silu(g_ref[:, sl])).astype(BF16)


def _hgrn_readout(o_f, o_b, p, g_norm, width):
    b, ta, _ = o_f.shape
    spec = pl.BlockSpec((None, TOK_TILE, width), lambda i, j: (i, j, 0))
    return pl.pallas_call(
        _hgrn_readout_kernel,
        out_shape=jax.ShapeDtypeStruct((b, ta, width), BF16),
        grid=(b, ta // TOK_TILE),
        in_specs=[spec, spec,
                  pl.BlockSpec((None, TOK_TILE, width), lambda i, j: (i, j, 4)),
                  pl.BlockSpec((1, HEAD), lambda i, j: (0, 0))],
        out_specs=spec,
        compiler_params=_cparams("parallel", "parallel"),
        name="hgrn_readout",
    )(o_f, o_b, p, g_norm.reshape(1, HEAD))


def _natten_bias(rpb):
    nrow, ncol = 2 * NA_ROWS - 1, 2 * NA_COLS - 1
    col = jnp.arange(GRID_W)
    col_start = jnp.clip(col - NA_COLS // 2, 0, GRID_W - NA_COLS)
    cmask = (col[None, :] >= col_start[:, None]) & (col[None, :] < col_start[:, None] + NA_COLS)
    dc = col[None, :] - col[:, None] + NA_COLS - 1
    cm = (cmask[:, :, None] & (dc[:, :, None] == jnp.arange(ncol))).astype(F32)
    cls = jnp.arange(3)[:, None, None]
    i = jnp.arange(NA_GROUP)[None, :, None]
    w = jnp.arange(NA_SPAN)[None, None, :]
    first_w = jnp.where(cls == 0, 0, jnp.where(cls == 1, i, NA_SPAN - NA_ROWS))
    valid = (w >= first_w) & (w < first_w + NA_ROWS)
    span0 = jnp.where(cls == 0, 0, jnp.where(cls == 1, -(NA_ROWS // 2), NA_GROUP - NA_SPAN))
    dr = span0 + w - i + NA_ROWS - 1
    rm = (valid[..., None] & (dr[..., None] == jnp.arange(nrow))).astype(F32)
    t = jnp.einsum('qkb,hab->haqk', cm, rpb.astype(F32), precision=HIGHEST)
    t = jnp.einsum('ciwa,haqk->hciqwk', rm, t, precision=HIGHEST)
    ok = valid[None, :, :, None, :, None] & cmask[None, None, None, :, None, :]
    t = jnp.where(ok, t, NEG_BIG)
    return t.reshape(rpb.shape[0], 3, NA_GROUP * GRID_W, NA_SPAN * GRID_W)


def _natten_kernel(q_ref, k_ref, v_ref, bias_ref, o_ref, *, ctx_len, rows):
    j = pl.program_id(2)
    nst = rows // NA_GROUP
    scale = HEAD ** -0.5
    heads = [slice(h * HEAD, (h + 1) * HEAD) for h in range(q_ref.shape[-1] // HEAD)]

    def ctx_scores(sl):
        qb = q_ref[:, sl].astype(BF16)
        vc = v_ref[0:ctx_len, sl].astype(BF16)
        s_ctx = _dot_nt(qb, k_ref[0:ctx_len, sl].astype(BF16)) * scale
        return qb, vc, s_ctx, jnp.max(s_ctx, axis=1, keepdims=True)

    @pl.when(j < nst)
    def _():
        last = j == nst - 1
        cls = jnp.where(j == 0, 0, jnp.where(last, 2, 1))
        span_row = jnp.where(j == 0, 0, jnp.where(last, rows - NA_SPAN, NA_GROUP * j - NA_ROWS // 2))
        start = pl.multiple_of(ctx_len + span_row * GRID_W, GRID_W)
        for h, sl in enumerate(heads):
            qb, vc, s_ctx, m_ctx = ctx_scores(sl)
            kw = k_ref[pl.ds(start, NA_SPAN * GRID_W), sl].astype(BF16)
            vw = v_ref[pl.ds(start, NA_SPAN * GRID_W), sl].astype(BF16)
            s_win = _dot_nt(qb, kw) * scale + bias_ref[h, cls]
            m = jnp.maximum(m_ctx, jnp.max(s_win, axis=1, keepdims=True))
            pw = jnp.exp(s_win - m)
            pc = jnp.exp(s_ctx - m)
            den = jnp.sum(pw, axis=1, keepdims=True) + jnp.sum(pc, axis=1, keepdims=True)
            o = _dot(pw.astype(BF16), vw) + _dot(pc.astype(BF16), vc)
            o_ref[:, sl] = (o / den).astype(BF16)

    @pl.when(j >= nst)
    def _():
        for sl in heads:
            _, vc, s_ctx, m_ctx = ctx_scores(sl)
            pc = jnp.exp(s_ctx - m_ctx)
            o = _dot(pc.astype(BF16), vc) / jnp.sum(pc, axis=1, keepdims=True)
            o_ref[:, sl] = o.astype(BF16)


def _natten(p, bias, ctx_len, col0, width, hb=2):
    b, ta, _ = p.shape
    cw = HEAD * hb
    nh = width // cw
    rows = (ta - ctx_len) // GRID_W
    tq = NA_GROUP * GRID_W
    nst = rows // NA_GROUP
    ncq = ctx_len // tq
    assert rows % NA_GROUP == 0 and rows >= NA_SPAN and ctx_len % tq == 0

    def qmap(cb):
        return lambda i, h, j: (i, jnp.where(j < nst, ncq + j, j - nst), cb + h)

    return pl.pallas_call(
        functools.partial(_natten_kernel, ctx_len=ctx_len, rows=rows),
        out_shape=jax.ShapeDtypeStruct((b, ta, width), BF16),
        grid=(b, nh, nst + ncq),
        in_specs=[pl.BlockSpec((None, tq, cw), qmap(col0 // cw)),
                  pl.BlockSpec((None, ta, cw), lambda i, h, j: (i, 0, (col0 + width) // cw + h)),
                  pl.BlockSpec((None, ta, cw), lambda i, h, j: (i, 0, (col0 + 2 * width) // cw + h)),
                  pl.BlockSpec((hb, 3, tq, NA_SPAN * GRID_W), lambda i, h, j: (h, 0, 0, 0))],
        out_specs=pl.BlockSpec((None, tq, cw), qmap(0)),
        compiler_params=_cparams("parallel", "parallel", "arbitrary"),
        name="natten",
    )(p, p, p, bias)


def _rope_tables(ctx_len, t):
    pos = jnp.arange(t)
    row = (pos // GRID_W).astype(F32)
    col = (pos % GRID_W).astype(F32)
    half = HEAD // 2
    inv = ROPE_THETA ** (-jnp.arange(0, half, 2, dtype=F32) / half)
    ang = jnp.concatenate([row[:, None] * inv, col[:, None] * inv], axis=-1)
    cos, sin = jnp.cos(ang), jnp.sin(ang)
    cosf = jnp.repeat(cos, 2, axis=-1)
    sinf = jnp.stack([-sin, sin], axis=-1).reshape(t, HEAD)
    cosf = jnp.concatenate([jnp.ones((ctx_len, HEAD), F32), cosf], axis=0)
    sinf = jnp.concatenate([jnp.zeros((ctx_len, HEAD), F32), sinf], axis=0)
    return cosf, sinf


def _rope(x, cosf, sinf):
    even = (lax.broadcasted_iota(jnp.int32, x.shape, 1) & 1) == 0
    partner = jnp.where(even, pltpu.roll(x, LANE - 1, 1), pltpu.roll(x, 1, 1))
    return x * cosf + partner * sinf


def _attn_kernel(q_ref, k_ref, v_ref, cq_ref, sq_ref, ck_ref, sk_ref, qn_ref, kn_ref, o_ref, kb_ref, vb_ref,
                 *, skip_tiles):
    j = pl.program_id(2)

    @pl.when(j == 0)
    def _():
        kb_ref[...] = _rope(_rms(k_ref[...]) * kn_ref[...], ck_ref[...], sk_ref[...]).astype(BF16)
        vb_ref[...] = v_ref[...].astype(BF16)

    @pl.when(j < skip_tiles)
    def _():
        o_ref[...] = jnp.zeros_like(o_ref)

    @pl.when(j >= skip_tiles)
    def _():
        k = kb_ref[...]
        v = vb_ref[...]
        cq = cq_ref[...]
        sq = sq_ref[...]
        for g in range(q_ref.shape[-1] // HEAD):
            sl = slice(g * HEAD, (g + 1) * HEAD)
            q = (_rope(_rms(q_ref[:, sl]) * qn_ref[...], cq, sq) * (HEAD ** -0.5 * LOG2E)).astype(BF16)
            s = _dot_nt(q, k)
            p = jnp.exp2(s - jnp.max(s, axis=1, keepdims=True))
            o = _dot(p.astype(BF16), v) / jnp.sum(p, axis=1, keepdims=True)
            o_ref[:, sl] = o.astype(BF16)


def _gqa_attention(p, cosf, sinf, q_norm, k_norm, qw, kw, skip_tiles):
    b, ta, _ = p.shape
    nkv = kw // HEAD
    tq = TOK_TILE
    gw = qw // nkv
    tab_q = pl.BlockSpec((tq, HEAD), lambda i, h, j: (j, 0))
    tab_k = pl.BlockSpec((ta, HEAD), lambda i, h, j: (0, 0))
    vec = pl.BlockSpec((1, HEAD), lambda i, h, j: (0, 0))
    return pl.pallas_call(
        functools.partial(_attn_kernel, skip_tiles=skip_tiles),
        out_shape=jax.ShapeDtypeStruct((b, ta, qw), BF16),
        grid=(b, nkv, ta // tq),
        in_specs=[pl.BlockSpec((None, tq, gw), lambda i, h, j: (i, j, h)),
                  pl.BlockSpec((None, ta, HEAD), lambda i, h, j: (i, 0, qw // HEAD + h)),
                  pl.BlockSpec((None, ta, HEAD), lambda i, h, j: (i, 0, (qw + kw) // HEAD + h)),
                  tab_q, tab_q, tab_k, tab_k, vec, vec],
        out_specs=pl.BlockSpec((None, tq, gw), lambda i, h, j: (i, j, h)),
        scratch_shapes=[pltpu.VMEM((ta, HEAD), BF16), pltpu.VMEM((ta, HEAD), BF16)],
        compiler_params=_cparams("parallel", "parallel", "arbitrary"),
        name="gqa_attention",
    )(p, p, p, cosf, sinf, cosf, sinf, q_norm.reshape(1, HEAD), k_norm.reshape(1, HEAD))


def _s5_operators(a_re, a_im, log_dt, b_re, b_im, c_re, c_im):
    lc, pp, ns = S5_CHUNK, S5_GROUP, S5_STATE
    a_c = lax.complex(a_re.astype(F32), a_im.astype(F32))
    adt = a_c * jnp.exp(log_dt.astype(F32))[..., None]
    a_bar = jnp.exp(adt)
    b_bar = ((a_bar - 1.0) / a_c)[..., None] * lax.complex(b_re.astype(F32), b_im.astype(F32))
    c_mat = lax.complex(c_re.astype(F32), c_im.astype(F32))
    nd, g = a_re.shape[0], a_re.shape[1]
    pos = jnp.arange(lc)
    cw = lc * pp

    def lag_rows(d, lags):
        apw = jnp.exp(adt[d][:, :, None] * lags[None, None, :])
        w = (apw[:, :, :, None] * jnp.swapaxes(c_mat[d], 1, 2)[:, :, None, :]).reshape(g, ns, cw)
        bt = jnp.swapaxes(b_bar[d], 1, 2)
        return (jnp.einsum('gqn,gnk->gqk', jnp.real(bt), jnp.real(w), precision=HIGHEST)
                - jnp.einsum('gqn,gnk->gqk', jnp.imag(bt), jnp.imag(w), precision=HIGHEST))

    kk = lag_rows(0, pos.astype(F32))
    kkr = lag_rows(1, pos[::-1].astype(F32))
    fwd = jnp.stack([jnp.pad(kk[..., :cw - pp * i], ((0, 0), (0, 0), (pp * i, 0))) for i in range(lc)], axis=1)
    bwd = jnp.stack([jnp.pad(kkr[..., pp * (lc - 1 - i):], ((0, 0), (0, 0), (0, pp * (lc - 1 - i)))) for i in range(lc)],
                    axis=1)
    tmat = jnp.stack([fwd, bwd], axis=1).reshape(g, nd, cw, cw)
    steps_after = jnp.stack([lc - 1 - pos, pos]).astype(F32)
    gm = jnp.exp(adt[:, :, None, :] * steps_after[:, None, :, None])[:, :, :, None, :] * jnp.swapaxes(b_bar, 2, 3)[:, :, None]
    gm = gm.reshape(nd, g, lc * pp, ns)
    steps_upto = jnp.stack([pos + 1, lc - pos]).astype(F32)
    hm = c_mat[:, :, None] * jnp.exp(adt[:, :, None, :] * steps_upto[:, None, :, None])[:, :, :, None, :]
    hm = jnp.transpose(hm, (0, 1, 4, 2, 3)).reshape(nd, g, ns, lc * pp)
    hmat = jnp.concatenate([jnp.real(hm), -jnp.imag(hm)], axis=2)
    al = jnp.exp(adt * float(lc))
    al = al[:, :, None, :]
    gd = lambda x: jnp.swapaxes(x, 0, 1)
    return (tmat.astype(BF16), gd(jnp.real(gm)).astype(BF16), gd(jnp.imag(gm)).astype(BF16),
            gd(hmat).astype(BF16), gd(jnp.real(al)), gd(jnp.imag(al)))


def _s5_kernel(u_ref, t_ref, gr_ref, gi_ref, h_ref, ar_ref, ai_ref, d_ref, y_ref, er_ref, ei_ref, xr_ref, xi_ref,
               *, nchunk, nctx, rpc):
    ns = S5_STATE
    u = u_ref[...]
    for dr in range(2):
        er_ref[dr] = _dot(u, gr_ref[dr])
        ei_ref[dr] = _dot(u, gi_ref[dr])

    coef = [(jnp.broadcast_to(ar_ref[dr], (rpc, ns)), jnp.broadcast_to(ai_ref[dr], (rpc, ns))) for dr in range(2)]

    def body(c, carry):
        out = []
        for dr in range(2):
            xr, xi = carry[2 * dr], carry[2 * dr + 1]
            pc = c if dr == 0 else jnp.where(c < nctx, nctx - 1 - c, nchunk - 1 + nctx - c)
            off = pl.multiple_of(pc * rpc, rpc)
            xr_ref[dr, pl.ds(off, rpc), :] = xr
            xi_ref[dr, pl.ds(off, rpc), :] = xi
            ar, ai = coef[dr]
            out.append(ar * xr - ai * xi + er_ref[dr, pl.ds(off, rpc), :])
            out.append(ar * xi + ai * xr + ei_ref[dr, pl.ds(off, rpc), :])
        return tuple(out)

    zero = jnp.zeros((rpc, ns), F32)
    lax.fori_loop(0, nchunk, body, (zero, zero, zero, zero), unroll=4)
    y = d_ref[...] * u.astype(F32)
    for dr in range(2):
        y = y + _dot(u, t_ref[dr])
        y = y + _dot(xr_ref[dr].astype(BF16), h_ref[dr, 0:ns, :]) + _dot(xi_ref[dr].astype(BF16), h_ref[dr, ns:2 * ns, :])
    y_ref[...] = y.astype(BF16)


def _group_perm(lc, pp, ngl):
    n = lc * ngl * pp
    src = jnp.arange(n)
    dst = (src // pp) % ngl * (lc * pp) + src // (ngl * pp) * pp + src % pp
    return (dst[:, None] == jnp.arange(n)[None, :]).astype(BF16)


def _s5_pack_kernel(u_ref, perm_ref, o_ref, *, lc, width):
    ngl = LANE // S5_GROUP
    cw = o_ref.shape[-1]
    for gb in range(width // LANE):
        lhs = jnp.concatenate([u_ref[:, i * width + gb * LANE:i * width + (gb + 1) * LANE] for i in range(lc)], axis=1)
        res = _dot(lhs, perm_ref[...]).astype(BF16)
        for gl in range(ngl):
            o_ref[gb * ngl + gl] = res[:, gl * cw:(gl + 1) * cw]


def _s5_unpack_glu_kernel(y_ref, perm_ref, w_ref, b_ref, o_ref, *, lc, width):
    ngl = LANE // S5_GROUP
    pieces = []
    for gb in range(width // LANE):
        lhs = jnp.concatenate([y_ref[gb * ngl + gl] for gl in range(ngl)], axis=1)
        pieces.append(_dot(lhs, perm_ref[...]))
    for i in range(lc):
        y = jnp.concatenate([p[:, i * LANE:(i + 1) * LANE] for p in pieces], axis=1)
        y = 0.5 * y * (1.0 + jnp.tanh(math.sqrt(2.0 / math.pi) * (y + 0.044715 * (y * y * y))))
        z = _dot(y.astype(BF16), w_ref[...]) + b_ref[...]
        o_ref[:, i * width:(i + 1) * width] = (y * jax.nn.sigmoid(z)).astype(BF16)


def _s5(p, ucol, width, ctx_len, ops, d_skip, w_glu, b_glu):
    tmat, g_re, g_im, hmat, a_re, a_im = ops
    b, ta, _ = p.shape
    lc, pp, ns = S5_CHUNK, S5_GROUP, S5_STATE
    g = width // pp
    ngl = LANE // pp
    nchunk = ta // lc
    rpc = 8
    cw = lc * pp
    rows = nchunk * rpc
    perm = _group_perm(lc, pp, ngl)
    nperm = perm.shape[0]
    u2 = p[..., ucol:ucol + width].astype(BF16).reshape(b, nchunk, lc * width)
    ug = pl.pallas_call(
        functools.partial(_s5_pack_kernel, lc=lc, width=width),
        out_shape=jax.ShapeDtypeStruct((g, b, nchunk, cw), BF16),
        grid=(b,),
        in_specs=[pl.BlockSpec((None, nchunk, lc * width), lambda i: (i, 0, 0)),
                  pl.BlockSpec((nperm, nperm), lambda i: (0, 0))],
        out_specs=pl.BlockSpec((g, None, nchunk, cw), lambda i: (0, i, 0, 0)),
        compiler_params=_cparams("parallel"),
        name="s5_pack",
    )(u2, perm)
    ug = jnp.pad(jnp.swapaxes(ug, 1, 2), ((0, 0), (0, 0), (0, rpc - b), (0, 0))).reshape(g, rows, cw)
    dvec = jnp.tile(d_skip.astype(F32).reshape(g, 1, pp), (1, lc, 1)).reshape(g, 1, cw)
    op = lambda r, c: pl.BlockSpec((None, 2, r, c), lambda i: (i, 0, 0, 0))
    y = pl.pallas_call(
        functools.partial(_s5_kernel, nchunk=nchunk, nctx=ctx_len // lc, rpc=rpc),
        out_shape=jax.ShapeDtypeStruct((g, rows, cw), BF16),
        grid=(g,),
        in_specs=[pl.BlockSpec((None, rows, cw), lambda i: (i, 0, 0)),
                  op(cw, cw), op(cw, ns), op(cw, ns), op(2 * ns, cw), op(1, ns), op(1, ns),
                  pl.BlockSpec((None, 1, cw), lambda i: (i, 0, 0))],
        out_specs=pl.BlockSpec((None, rows, cw), lambda i: (i, 0, 0)),
        scratch_shapes=[pltpu.VMEM((2, rows, ns), F32) for _ in range(4)],
        compiler_params=_cparams("parallel"),
        name="s5_scan",
    )(ug, tmat, g_re, g_im, hmat, a_re, a_im, dvec)
    y = jnp.swapaxes(y.reshape(g, nchunk, rpc, cw)[:, :, :b], 1, 2)
    out = pl.pallas_call(
        functools.partial(_s5_unpack_glu_kernel, lc=lc, width=width),
        out_shape=jax.ShapeDtypeStruct((b, nchunk, lc * width), BF16),
        grid=(b,),
        in_specs=[pl.BlockSpec((g, None, nchunk, cw), lambda i: (0, i, 0, 0)),
                  pl.BlockSpec((nperm, nperm), lambda i: (0, 0)),
                  pl.BlockSpec((width, width), lambda i: (0, 0)),
                  pl.BlockSpec((1, width), lambda i: (0, 0))],
        out_specs=pl.BlockSpec((None, nchunk, lc * width), lambda i: (i, 0, 0)),
        compiler_params=_cparams("parallel"),
        name="s5_unpack_glu",
    )(y, perm.T, w_glu.astype(BF16), b_glu.reshape(1, width))
    return out.reshape(b, ta, width)


def _postmix_kernel(a_ref, b_ref, w_ref, *refs, nx):
    m_ref, gpost_ref, gpre_ref, wr_ref, xo_ref, h_ref, lg_ref = refs[nx:]
    wa = a_ref.shape[-1]
    y = _dot(a_ref[...], w_ref[0:wa, :]) + _dot(b_ref[...], w_ref[wa:, :])
    m = m_ref[...]
    xn = _stream_tile(refs[:nx]) + m[2:3] * (_rms(y) * gpost_ref[...])
    xo_ref[...] = xn
    h2 = _rms(xn) * gpre_ref[...] * (1.0 + m[4:5]) + m[3:4]
    hi = h2.astype(BF16)
    h_ref[...] = hi
    lo = (h2 - hi.astype(F32)).astype(BF16)
    ne = lg_ref.shape[-1]
    both = _dot(hi, wr_ref[...])
    lg_ref[...] = both[:, 0:ne] + both[:, ne:2 * ne] + _dot(lo, wr_ref[:, 0:ne])


def _postmix(mix_a, mix_b, w_out, xs, modtab, g_post, g_pre, w_router):
    (b, ta, d), x_specs, x_args = _stream_operands(xs)
    wa, wb = mix_a.shape[-1], mix_b.shape[-1]
    ne = w_router.shape[-1]
    tok = lambda w: pl.BlockSpec((None, TOK_TILE, w), lambda i, j: (i, j, 0))
    vec = pl.BlockSpec((1, d), lambda i, j: (0, 0))
    wr_hi = w_router.astype(BF16)
    wr2 = jnp.concatenate([wr_hi, (w_router - wr_hi.astype(F32)).astype(BF16)], axis=1)
    return pl.pallas_call(
        functools.partial(_postmix_kernel, nx=len(x_args)),
        out_shape=(jax.ShapeDtypeStruct((b, ta, d), F32),
                   jax.ShapeDtypeStruct((b, ta, d), BF16),
                   jax.ShapeDtypeStruct((b, ta, ne), F32)),
        grid=(b, ta // TOK_TILE),
        in_specs=[tok(wa), tok(wb),
                  pl.BlockSpec((wa + wb, d), lambda i, j: (0, 0))] + x_specs + [
                  _mod_spec(d), vec, vec,
                  pl.BlockSpec((d, 2 * ne), lambda i, j: (0, 0))],
        out_specs=(tok(d), tok(d), tok(ne)),
        compiler_params=_cparams("parallel", "parallel"),
        name="mix_out",
    )(mix_a, mix_b, w_out.astype(BF16), *x_args, modtab, g_post.reshape(1, d), g_pre.reshape(1, d), wr2)


def _select_kernel(lg_ref, pos_ref, gate_ref, *, ctx_len, cap_ctx, cap_lat):
    lg = lg_ref[...]
    ne, ta = lg.shape
    ex = jnp.exp(lg - jnp.max(lg, axis=0, keepdims=True))
    probs = ex / jnp.sum(ex, axis=0, keepdims=True)
    bits = pltpu.bitcast(probs, jnp.int32)
    is_ctx = lax.broadcasted_iota(jnp.int32, (ne, ta), 1) < ctx_len

    def counts(mask):
        mf = jnp.where(mask, 1.0, 0.0)
        return (jnp.sum(jnp.where(is_ctx, mf, 0.0), axis=1, keepdims=True),
                jnp.sum(jnp.where(is_ctx, 0.0, mf), axis=1, keepdims=True))

    def search(i, carry):
        pc, pt = carry
        bit = jnp.left_shift(jnp.int32(1), 30 - i)
        cc, ct = counts(bits >= jnp.where(is_ctx, pc | bit, pt | bit))
        return jnp.where(cc >= cap_ctx, pc | bit, pc), jnp.where(ct >= cap_lat, pt | bit, pt)

    z = jnp.zeros((ne, 1), jnp.int32)
    pc, pt = lax.fori_loop(0, 31, search, (z, z))
    thr = jnp.where(is_ctx, pc, pt)
    gt = bits > thr
    eq = bits == thr
    gc, gl = counts(gt)
    need = jnp.where(is_ctx, cap_ctx - gc, cap_lat - gl)

    nb = ta // LANE
    ut = jnp.where(lax.broadcasted_iota(jnp.int32, (LANE, LANE), 0) <= lax.broadcasted_iota(jnp.int32, (LANE, LANE), 1),
                   1.0, 0.0).astype(BF16)

    def lane_prefix(mask):
        mf = jnp.where(mask, 1.0, 0.0).astype(BF16)
        blocks = jnp.concatenate([mf[:, j * LANE:(j + 1) * LANE] for j in range(nb)], axis=0)
        inc = _dot(blocks, ut)
        outs = []
        off = jnp.zeros((ne, 1), F32)
        for j in range(nb):
            if j * LANE == ctx_len:
                off = jnp.zeros((ne, 1), F32)
            blk = inc[j * ne:(j + 1) * ne]
            outs.append(blk + off)
            off = off + blk[:, LANE - 1:LANE]
        return jnp.concatenate(outs, axis=1)

    sel = gt | (eq & (lane_prefix(eq) <= need))
    slot = lane_prefix(sel) - 1.0 + jnp.where(is_ctx, 0.0, float(cap_ctx))
    pos_ref[...] = jnp.where(sel, slot, -1.0)
    gate_ref[...] = jnp.where(sel, probs, 0.0)


def _moe_select(logits_t, ctx_len, cap_ctx, cap_lat):
    b, ne, ta = logits_t.shape
    spec = pl.BlockSpec((None, ne, ta), lambda i: (i, 0, 0))
    return pl.pallas_call(
        functools.partial(_select_kernel, ctx_len=ctx_len, cap_ctx=cap_ctx, cap_lat=cap_lat),
        out_shape=(jax.ShapeDtypeStruct((b, ne, ta), F32), jax.ShapeDtypeStruct((b, ne, ta), F32)),
        grid=(b,),
        in_specs=[spec],
        out_specs=(spec, spec),
        compiler_params=_cparams("parallel"),
        name="moe_select",
    )(logits_t)


def _slot_windows(pos_t, nt, win, nslot):
    b, ne, ta = pos_t.shape
    pt = pos_t.reshape(b, ne, nt, ta // nt)
    hi = jnp.max(pt, axis=-1).astype(jnp.int32) + 1
    lo = jnp.min(jnp.where(pt >= 0, pt, float(nslot)), axis=-1).astype(jnp.int32)
    lo = jnp.where(hi > 0, lo // 16 * 16, 0)
    npass = jnp.maximum(jnp.max((hi - lo + win - 1) // win, axis=1), 1)
    return jnp.transpose(lo, (0, 2, 1)).reshape(-1), npass.reshape(-1)


def _gather_kernel(start_ref, npass_ref, pos_ref, h_ref, o_ref, acc_ref):
    i = pl.program_id(0)
    eg = pl.program_id(1)
    j = pl.program_id(2)
    ge, tk = pos_ref.shape
    ne = pl.num_programs(1) * ge
    nslot = o_ref.shape[1]
    win = GATHER_WIN
    tile = i * pl.num_programs(2) + j

    @pl.when(j == 0)
    def _():
        acc_ref[...] = jnp.zeros_like(acc_ref)

    pos = pos_ref[...].astype(jnp.int32)
    slot = lax.broadcasted_iota(jnp.int32, (win, tk), 0)

    def one_pass(p, carry):
        src, blocks = [], []
        for el in range(ge):
            lo = start_ref[tile * ne + eg * ge + el] + p * win
            src.append(pl.multiple_of(jnp.minimum(lo, nslot - win), 16))
            pe = pos[el:el + 1, :]
            hit = (pe >= lo) & (pe < lo + win) & (slot == pe - src[el])
            blocks.append(jnp.where(hit, 1.0, 0.0).astype(BF16))
        part = _dot(jnp.concatenate(blocks, axis=0), h_ref[...])
        for el in range(ge):
            acc_ref[el, pl.ds(src[el], win), :] += part[el * win:(el + 1) * win]
        return carry

    lax.fori_loop(0, npass_ref[tile], one_pass, 0)

    @pl.when(j == pl.num_programs(2) - 1)
    def _():
        o_ref[...] = acc_ref[...].astype(BF16)


def _moe_gather(pos_t, h, nslot):
    b, ne, ta = pos_t.shape
    d = h.shape[-1]
    ge = 4
    tk = TOK_TILE
    nt = ta // tk
    start, npass = _slot_windows(pos_t, nt, GATHER_WIN, nslot)
    return pl.pallas_call(
        _gather_kernel,
        out_shape=jax.ShapeDtypeStruct((ne, b, nslot, d), BF16),
        grid_spec=pltpu.PrefetchScalarGridSpec(
            num_scalar_prefetch=2,
            grid=(b, ne // ge, nt),
            in_specs=[pl.BlockSpec((None, None, ge, tk), lambda i, g, j, *_: (i, g, 0, j)),
                      pl.BlockSpec((None, tk, d), lambda i, g, j, *_: (i, j, 0))],
            out_specs=pl.BlockSpec((ge, None, nslot, d), lambda i, g, j, *_: (g, i, 0, 0)),
            scratch_shapes=[pltpu.VMEM((ge, nslot, d), F32)]),
        compiler_params=_cparams("arbitrary", "arbitrary", "arbitrary"),
        name="moe_gather",
    )(start, npass, pos_t.reshape(b, ne // ge, ge, ta), h)


def _ffn_kernel(x_ref, wg_ref, wu_ref, wd_ref, o_ref, hid_ref, *, nf):
    s = pl.program_id(2)
    tf = wg_ref.shape[-1]

    @pl.when(s < nf)
    def _():
        x = x_ref[...]
        g = _dot(x, wg_ref[...].astype(BF16))
        u = _dot(x, wu_ref[...].astype(BF16))
        hid_ref[s] = (_silu(g) * u).astype(BF16)

    @pl.when(s >= nf)
    def _():
        acc = _dot(hid_ref[0], wd_ref[0:tf, :].astype(BF16))
        for f in range(1, nf):
            acc = acc + _dot(hid_ref[f], wd_ref[f * tf:(f + 1) * tf, :].astype(BF16))
        o_ref[...] = acc.astype(BF16)


def _moe_ffn(xin, w_gate, w_up, w_down, layer, tf=512):
    ne, ns, m, d = xin.shape
    ff = w_gate.shape[-1]
    nf = ff // tf
    nd = d // tf
    up = lambda e, s, f: (layer, e, 0, jnp.minimum(f, nf - 1))
    return pl.pallas_call(
        functools.partial(_ffn_kernel, nf=nf),
        out_shape=jax.ShapeDtypeStruct((ne, ns, m, d), BF16),
        grid=(ne, ns, nf + nd),
        in_specs=[pl.BlockSpec((None, None, m, d), lambda e, s, f: (e, s, 0, 0)),
                  pl.BlockSpec((None, None, d, tf), up),
                  pl.BlockSpec((None, None, d, tf), up),
                  pl.BlockSpec((None, None, ff, tf), lambda e, s, f: (layer, e, 0, jnp.maximum(f - nf, 0)))],
        out_specs=pl.BlockSpec((None, None, m, tf), lambda e, s, f: (e, s, 0, jnp.maximum(f - nf, 0))),
        scratch_shapes=[pltpu.VMEM((nf, m, tf), BF16)],
        compiler_params=_cparams("parallel", "parallel", "arbitrary"),
        name="moe_ffn",
    )(xin, w_gate, w_up, w_down)


def _combine_kernel(start_ref, npass_ref, pos_ref, gate_ref, y_hbm, x_ref, m_ref, g_ref, *refs,
                    ctx_tiles, skip, with_next):
    if with_next:
        gn_ref, mn_ref, o_ref, hn_ref, ybuf, acc_ref, sem = refs
    else:
        o_ref, ybuf, acc_ref, sem = refs
    tt, ne = pos_ref.shape
    nslot = y_hbm.shape[2]
    win = COMBINE_WIN
    nj = pl.num_programs(1)
    step = pl.program_id(0) * nj + pl.program_id(1)
    buf = step % 2
    pos = pos_ref[...].astype(jnp.int32)
    gate = gate_ref[...]
    lane = lax.broadcasted_iota(jnp.int32, (tt, win), 1)

    def windows(s, p):
        smp = s // nj
        tile = smp * (nj + skip) + s % nj + skip
        lo = [start_ref[tile * ne + e] + p * win for e in range(ne)]
        return smp, lo, [pl.multiple_of(jnp.minimum(v, nslot - win), 16) for v in lo]

    def window_copy(smp, e, src, slot):
        return pltpu.make_async_copy(y_hbm.at[e, smp, pl.ds(src, win), :], ybuf.at[slot, pl.ds(e * win, win), :],
                                     sem.at[slot, e])

    def fetch(s, p, slot):
        smp, _, src = windows(s, p)
        for e in range(ne):
            window_copy(smp, e, src[e], slot).start()

    def one_pass(p):
        smp, lo, src = windows(step, p)
        blocks = []
        for e in range(ne):
            pe = pos[:, e:e + 1]
            hit = (pe >= lo[e]) & (pe < lo[e] + win) & (lane == pe - src[e])
            blocks.append(jnp.where(hit, gate[:, e:e + 1], 0.0).astype(BF16))
        w = jnp.concatenate(blocks, axis=1)
        for e in range(ne):
            window_copy(smp, e, src[e], buf).wait()
        return _dot(w, ybuf[buf])

    @pl.when(step == 0)
    def _():
        fetch(step, 0, buf)

    @pl.when(step + 1 < pl.num_programs(0) * nj)
    def _():
        fetch(step + 1, 0, 1 - buf)

    acc_ref[...] = one_pass(0)

    def extra(p, carry):
        fetch(step, p, buf)
        acc_ref[...] += one_pass(p)
        return carry

    lax.fori_loop(1, npass_ref[windows(step, 0)[0] * (nj + skip) + step % nj + skip], extra, 0)
    m = m_ref[...]
    is_ctx = pl.program_id(1) + skip < ctx_tiles
    pick = lambda k: jnp.where(is_ctx, m[0, k:k + 1], m[1, k:k + 1])
    xn = x_ref[...] + pick(5) * (_rms(acc_ref[...]) * g_ref[...])
    o_ref[...] = xn
    if with_next:
        mn = mn_ref[...]
        pick_n = lambda k: jnp.where(is_ctx, mn[0, k:k + 1], mn[1, k:k + 1])
        hn_ref[...] = (_rms(xn) * gn_ref[...] * (1.0 + pick_n(1)) + pick_n(0)).astype(BF16)


def _moe_combine(pos_t, gate_t, yout, xa, modtab, g_post, ctx_len, nxt=None, latent_only=False):
    b, ta, d = xa.shape
    ne, _, nslot, _ = yout.shape
    tt = TOK_TILE
    nt = ta // tt
    skip = ctx_len // tt if latent_only else 0
    win = COMBINE_WIN
    start, npass = _slot_windows(pos_t, nt, win, nslot)
    tok = lambda w: pl.BlockSpec((None, tt, w), lambda i, j, *_: (i, j + skip, 0))
    out_tok = lambda: pl.BlockSpec((None, tt, d), lambda i, j, *_: (i, j, 0))
    mods = pl.BlockSpec((None, 2, 6, d), lambda i, j, *_: (i, 0, 0, 0))
    vec = pl.BlockSpec((1, d), lambda i, j, *_: (0, 0))
    xo = jax.ShapeDtypeStruct((b, ta - skip * tt, d), F32)
    extra_in, extra_args = ([vec, mods], [nxt[0].reshape(1, d), nxt[1]]) if nxt else ([], [])
    return pl.pallas_call(
        functools.partial(_combine_kernel, ctx_tiles=ctx_len // tt, skip=skip, with_next=bool(nxt)),
        out_shape=(xo, jax.ShapeDtypeStruct((b, ta - skip * tt, d), BF16)) if nxt else xo,
        grid_spec=pltpu.PrefetchScalarGridSpec(
            num_scalar_prefetch=2,
            grid=(b, nt - skip),
            in_specs=[tok(ne), tok(ne),
                      pl.BlockSpec(memory_space=pl.ANY),
                      tok(d), mods, vec] + extra_in,
            out_specs=(out_tok(), out_tok()) if nxt else out_tok(),
            scratch_shapes=[pltpu.VMEM((2, ne * win, d), BF16),
                            pltpu.VMEM((tt, d), F32),
                            pltpu.SemaphoreType.DMA((2, ne))]),
        compiler_params=_cparams("arbitrary", "arbitrary"),
        name="moe_combine",
    )(start, npass, jnp.swapaxes(pos_t, 1, 2), jnp.swapaxes(gate_t, 1, 2), yout, xa, modtab,
      g_post.reshape(1, d), *extra_args)


def _ec_moe(logits_t, h, xa, modtab, g_post, w_gate, w_up, w_down, layer, ctx_len, nxt=None, latent_only=False):
    b, ne, ta = logits_t.shape
    d = h.shape[-1]
    cap_ctx = max(1, EC_CAPACITY * ctx_len // ne)
    cap_lat = max(1, EC_CAPACITY * (ta - ctx_len) // ne)
    nslot = cap_ctx + cap_lat
    pos_t, gate_t = _moe_select(logits_t, ctx_len, cap_ctx, cap_lat)
    xin = _moe_gather(pos_t, h, nslot)
    pair = 2 if b % 2 == 0 else 1
    yout = _moe_ffn(xin.reshape(ne, b // pair, pair * nslot, d), w_gate, w_up, w_down, layer)
    yout = yout.reshape(ne, b, nslot, d)
    return _moe_combine(pos_t, gate_t, yout, xa, modtab, g_post, ctx_len, nxt, latent_only)


def kernel(x, c, ctx, c_ctx, w_mod, b_mod, g_mix_pre, g_mix_post, g_ffn_pre, g_ffn_post, w_router, w_exp_gate, w_exp_up, w_exp_down, ev_w_in, ev_w_out, hgrn_lb, hgrn_g_norm, na_rpb, od_w_in, od_w_out, q_norm, k_norm, s5_a_re, s5_a_im, s5_log_dt, s5_b_re, s5_b_im, s5_c_re, s5_c_im, s5_d, s5_w_glu, s5_b_glu):
    b, t, d = x.shape
    ctx_len = ctx.shape[1]
    depth = w_mod.shape[0]
    assert depth == 2 and b <= 7
    ta = ctx_len + t
    a_width = d // 2
    s5_width = d // 4
    cq_width = d - s5_width
    ckv_width = cq_width // 3

    cc = jnp.concatenate([c, c_ctx[None], jnp.zeros((7 - b, d), F32)], axis=0)
    mod = _modulation(cc, w_mod, b_mod)
    mod_lat = mod[:, :b].reshape(depth, b, 1, 6, d)
    mod_ctx = jnp.broadcast_to(mod[:, b].reshape(depth, 1, 1, 6, d), (depth, b, 1, 6, d))
    modtab = jnp.concatenate([mod_ctx, mod_lat], axis=2)

    lb_all = jnp.cumsum(jax.nn.softmax(hgrn_lb.astype(F32), axis=0), axis=0)
    hx = _prenorm((ctx, x), g_mix_pre[0], modtab[0])
    p = _matmul(hx.reshape(b * ta, d), ev_w_in[0]).reshape(b, ta, -1)
    o_f, o_b = _hgrn(p, lb_all[0], ctx_len, a_width)
    mix_a = _hgrn_readout(o_f, o_b, p, hgrn_g_norm[0], a_width)
    mix_b = _natten(p, _natten_bias(na_rpb[0]), ctx_len, 5 * a_width, d - a_width)
    xa, h2, logits = _postmix(mix_a, mix_b, ev_w_out[0], (ctx, x), modtab[0], g_mix_post[0], g_ffn_pre[0], w_router[0])
    xa, hx = _ec_moe(jnp.swapaxes(logits, 1, 2), h2, xa, modtab[0], g_ffn_post[0], w_exp_gate, w_exp_up, w_exp_down, 0,
                     ctx_len, nxt=(g_mix_pre[1], modtab[1]))

    p = _matmul(hx.reshape(b * ta, d), od_w_in[0]).reshape(b, ta, -1)
    cosf, sinf = _rope_tables(ctx_len, t)
    mix_a = _gqa_attention(p, cosf, sinf, q_norm[0], k_norm[0], cq_width, ckv_width, ctx_len // TOK_TILE)
    ops = _s5_operators(s5_a_re[0], s5_a_im[0], s5_log_dt[0], s5_b_re[0], s5_b_im[0], s5_c_re[0], s5_c_im[0])
    mix_b = _s5(p, cq_width + 2 * ckv_width, s5_width, ctx_len, ops, s5_d[0], s5_w_glu[0], s5_b_glu[0])
    xa, h2, logits = _postmix(mix_a, mix_b, od_w_out[0], xa, modtab[1], g_mix_post[1], g_ffn_pre[1], w_router[1])
    return _ec_moe(jnp.swapaxes(logits, 1, 2), h2, xa, modtab[1], g_ffn_post[1], w_exp_gate, w_exp_up, w_exp_down, 1,
                   ctx_len, latent_only=True)
```

```python
import functools
import math

import jax
import jax.numpy as jnp
from jax import lax
from jax.experimental import pallas as pl
from jax.experimental.pallas import tpu as pltpu

F32 = jnp.float32
BF16 = jnp.bfloat16
HIGHEST = lax.Precision.HIGHEST
EPS = 1e-6

LANE = 128
TOK_TILE = 256
VMEM_LIMIT = 52 << 20

GRID_W = 64
NA_ROWS = 8
NA_COLS = 16
NA_GROUP = 4
NA_SPAN = 12
HEAD = 128
A_CHUNK = 64
N_EXPERTS = 16
EC_CAPACITY = 2
S5_GROUP = 16
S5_STATE = 64
S5_CHUNK = 16
COMBINE_WIN = 64
GATHER_WIN = 64
ROPE_THETA = 10000.0
NEG_BIG = -1e30
LOG2E = 1.4426950408889634


def _cparams(*sem):
    return pltpu.CompilerParams(dimension_semantics=sem, vmem_limit_bytes=VMEM_LIMIT)


def _dot(a, b):
    return jnp.dot(a, b, preferred_element_type=F32)


def _dot_nt(a, b):
    return lax.dot_general(a, b, (((1,), (1,)), ((), ())), preferred_element_type=F32)


def _dot_tn(a, b):
    return lax.dot_general(a, b, (((0,), (0,)), ((), ())), preferred_element_type=F32)


def _rms(x):
    return x * lax.rsqrt(jnp.mean(x * x, axis=-1, keepdims=True) + EPS)


def _silu(x):
    return x * jax.nn.sigmoid(x)


def _mod_kernel(c_ref, w_ref, b_ref, o_ref):
    o_ref[...] = jnp.dot(_silu(c_ref[...]), w_ref[...], preferred_element_type=F32, precision=HIGHEST) + b_ref[...]


def _modulation(cc, w_mod, b_mod):
    depth, d, n = w_mod.shape
    tn = 1024
    return pl.pallas_call(
        _mod_kernel,
        out_shape=jax.ShapeDtypeStruct((depth, cc.shape[0], n), F32),
        grid=(depth, n // tn),
        in_specs=[pl.BlockSpec(cc.shape, lambda l, j: (0, 0)),
                  pl.BlockSpec((None, d, tn), lambda l, j: (l, 0, j)),
                  pl.BlockSpec((None, 1, tn), lambda l, j: (l, 0, j))],
        out_specs=pl.BlockSpec((None, cc.shape[0], tn), lambda l, j: (l, 0, j)),
        compiler_params=_cparams("arbitrary", "arbitrary"),
        name="modulation",
    )(cc, w_mod, b_mod.reshape(depth, 1, n))


def _mod_spec(d):
    return pl.BlockSpec((None, None, 6, d), lambda b, j: (b, jnp.minimum(j, 1), 0, 0))


def _stream_operands(xs):
    if not isinstance(xs, tuple):
        b, ta, d = xs.shape
        return (b, ta, d), [pl.BlockSpec((None, TOK_TILE, d), lambda i, j: (i, j, 0))], [xs]
    ctx, x = xs
    b, t, d = x.shape
    assert ctx.shape[1] == TOK_TILE
    return (b, TOK_TILE + t, d), [pl.BlockSpec((None, TOK_TILE, d), lambda i, j: (i, 0, 0)),
                                  pl.BlockSpec((None, TOK_TILE, d), lambda i, j: (i, jnp.maximum(j - 1, 0), 0))], [ctx, x]


def _stream_tile(refs):
    if len(refs) == 1:
        return refs[0][...]
    return jnp.where(pl.program_id(1) == 0, refs[0][...], refs[1][...])


def _prenorm_kernel(*refs, nx):
    g_ref, m_ref, o_ref = refs[nx:]
    m = m_ref[...]
    y = _rms(_stream_tile(refs[:nx])) * g_ref[...]
    o_ref[...] = (y * (1.0 + m[1:2]) + m[0:1]).astype(BF16)


def _prenorm(xs, g, modtab):
    (b, ta, d), x_specs, x_args = _stream_operands(xs)
    return pl.pallas_call(
        functools.partial(_prenorm_kernel, nx=len(x_args)),
        out_shape=jax.ShapeDtypeStruct((b, ta, d), BF16),
        grid=(b, ta // TOK_TILE),
        in_specs=x_specs + [pl.BlockSpec((1, d), lambda i, j: (0, 0)), _mod_spec(d)],
        out_specs=pl.BlockSpec((None, TOK_TILE, d), lambda i, j: (i, j, 0)),
        compiler_params=_cparams("parallel", "parallel"),
        name="prenorm",
    )(*x_args, g.reshape(1, d), modtab)


def _mm_kernel(a_ref, w_ref, o_ref, wb_ref):
    @pl.when(pl.program_id(1) == 0)
    def _():
        wb_ref[...] = w_ref[...].astype(BF16)

    o_ref[...] = _dot(a_ref[...], wb_ref[...]).astype(o_ref.dtype)


def _matmul(a, w, tm=1024, tn=1024, out_dtype=F32):
    m, k = a.shape
    n = w.shape[1]
    return pl.pallas_call(
        _mm_kernel,
        out_shape=jax.ShapeDtypeStruct((m, n), out_dtype),
        grid=(n // tn, m // tm),
        in_specs=[pl.BlockSpec((tm, k), lambda j, i: (i, 0)),
                  pl.BlockSpec((k, tn), lambda j, i: (0, j))],
        out_specs=pl.BlockSpec((tm, tn), lambda j, i: (i, j)),
        scratch_shapes=[pltpu.VMEM((k, tn), BF16)],
        compiler_params=_cparams("arbitrary", "arbitrary"),
        name="proj_in",
    )(a, w)


def _hgrn_masks(chunk):
    row = lax.broadcasted_iota(jnp.int32, (chunk, LANE), 0)
    ti = lax.broadcasted_iota(jnp.int32, (chunk, chunk), 0)
    si = lax.broadcasted_iota(jnp.int32, (chunk, chunk), 1)
    levels = [(((row >> lvl) & 1) == 1, (ti >> (lvl + 1)) == (si >> (lvl + 1))) for lvl in range(chunk.bit_length() - 1)]
    return ti == si, levels


def _hgrn_chunk(q, fr, v, lb, st, rev, masks):
    f = lb + (1.0 - lb) * jax.nn.sigmoid(fr)
    kk = 1.0 - f
    lf = jnp.log(f)
    chunk = q.shape[0]
    diag, levels = masks
    att = jnp.where(diag, _dot_nt(q.astype(BF16), kk.astype(BF16)), 0.0)
    p_in = lf
    r_ex = jnp.zeros_like(lf)
    tot = lf
    for lvl, (bit, same) in enumerate(levels):
        step = 1 << lvl
        up = pltpu.roll(tot, step, 0)
        dn = pltpu.roll(tot, chunk - step, 0)
        is_q = jnp.logical_not(bit) if rev else bit
        qf = jnp.where(is_q, jnp.exp(p_in) * q, 0.0).astype(BF16)
        kf = jnp.where(is_q, 0.0, jnp.exp(r_ex) * kk).astype(BF16)
        att = att + jnp.where(same, _dot_nt(qf, kf), 0.0)
        if rev:
            p_in = p_in + jnp.where(bit, 0.0, dn)
            r_ex = r_ex + jnp.where(bit, up, 0.0)
        else:
            p_in = p_in + jnp.where(bit, up, 0.0)
            r_ex = r_ex + jnp.where(bit, 0.0, dn)
        tot = tot + jnp.where(bit, up, dn)
    vb = v.astype(BF16)
    o = _dot_nt((q * jnp.exp(p_in)).astype(BF16), st.astype(BF16)) + _dot(att.astype(BF16), vb)
    kd = (kk * jnp.exp(r_ex)).astype(BF16)
    st_new = st * jnp.exp(tot[0:1]) + _dot_tn(vb, kd)
    return o, st_new


def _hgrn_kernel(qf_ref, ff_ref, vf_ref, qb_ref, fb_ref, vb_ref, lb_ref, of_ref, ob_ref, st_ref, *, chunk, hb):
    @pl.when(pl.program_id(2) == 0)
    def _():
        st_ref[...] = jnp.zeros_like(st_ref)

    masks = _hgrn_masks(chunk)
    for h in range(hb):
        sl = slice(h * HEAD, (h + 1) * HEAD)
        o, st = _hgrn_chunk(qf_ref[:, sl], ff_ref[:, sl], vf_ref[:, sl], lb_ref[0:1, sl], st_ref[0, h], False, masks)
        of_ref[:, sl] = o
        st_ref[0, h] = st
        o, st = _hgrn_chunk(qb_ref[:, sl], fb_ref[:, sl], vb_ref[:, sl], lb_ref[1:2, sl], st_ref[1, h], True, masks)
        ob_ref[:, sl] = o
        st_ref[1, h] = st


def _hgrn(p, lb, ctx_len, width, hb=4):
    b, ta, _ = p.shape
    chunk = A_CHUNK
    n = ta // chunk
    nc = ctx_len // chunk
    cw = HEAD * hb
    nh = width // cw
    bwd = lambda j: jnp.where(j < nc, nc - 1 - j, n - 1 + nc - j)

    def spec(base, rev):
        return pl.BlockSpec((None, chunk, cw), lambda i, h, j: (i, bwd(j) if rev else j, base // cw + h))

    out = jax.ShapeDtypeStruct((b, ta, width), F32)
    return pl.pallas_call(
        functools.partial(_hgrn_kernel, chunk=chunk, hb=hb),
        out_shape=(out, out),
        grid=(b, nh, n),
        in_specs=[spec(0, False), spec(width, False), spec(3 * width, False),
                  spec(0, True), spec(2 * width, True), spec(3 * width, True),
                  pl.BlockSpec((2, cw), lambda i, h, j: (0, h))],
        out_specs=(spec(0, False), spec(0, True)),
        scratch_shapes=[pltpu.VMEM((2, hb, HEAD, HEAD), F32)],
        compiler_params=_cparams("parallel", "parallel", "arbitrary"),
        name="hgrn_scan",
    )(p, p, p, p, p, p, lb)


def _hgrn_readout_kernel(of_ref, ob_ref, g_ref, gn_ref, o_ref):
    for h in range(o_ref.shape[-1] // HEAD):
        sl = slice(h * HEAD, (h + 1) * HEAD)
        o = of_ref[:, sl] + ob_ref[:, sl]
        o_ref[:, sl] = (_rms(o) * gn_ref[...] * _silu(g_ref[:, sl])).astype(BF16)


def _hgrn_readout(o_f, o_b, p, g_norm, width):
    b, ta, _ = o_f.shape
    spec = pl.BlockSpec((None, TOK_TILE, width), lambda i, j: (i, j, 0))
    return pl.pallas_call(
        _hgrn_readout_kernel,
        out_shape=jax.ShapeDtypeStruct((b, ta, width), BF16),
        grid=(b, ta // TOK_TILE),
        in_specs=[spec, spec,
                  pl.BlockSpec((None, TOK_TILE, width), lambda i, j: (i, j, 4)),
                  pl.BlockSpec((1, HEAD), lambda i, j: (0, 0))],
        out_specs=spec,
        compiler_params=_cparams("parallel", "parallel"),
        name="hgrn_readout",
    )(o_f, o_b, p, g_norm.reshape(1, HEAD))


def _natten_bias(rpb):
    nrow, ncol = 2 * NA_ROWS - 1, 2 * NA_COLS - 1
    col = jnp.arange(GRID_W)
    col_start = jnp.clip(col - NA_COLS // 2, 0, GRID_W - NA_COLS)
    cmask = (col[None, :] >= col_start[:, None]) & (col[None, :] < col_start[:, None] + NA_COLS)
    dc = col[None, :] - col[:, None] + NA_COLS - 1
    cm = (cmask[:, :, None] & (dc[:, :, None] == jnp.arange(ncol))).astype(F32)
    cls = jnp.arange(3)[:, None, None]
    i = jnp.arange(NA_GROUP)[None, :, None]
    w = jnp.arange(NA_SPAN)[None, None, :]
    first_w = jnp.where(cls == 0, 0, jnp.where(cls == 1, i, NA_SPAN - NA_ROWS))
    valid = (w >= first_w) & (w < first_w + NA_ROWS)
    span0 = jnp.where(cls == 0, 0, jnp.where(cls == 1, -(NA_ROWS // 2), NA_GROUP - NA_SPAN))
    dr = span0 + w - i + NA_ROWS - 1
    rm = (valid[..., None] & (dr[..., None] == jnp.arange(nrow))).astype(F32)
    t = jnp.einsum('qkb,hab->haqk', cm, rpb.astype(F32), precision=HIGHEST)
    t = jnp.einsum('ciwa,haqk->hciqwk', rm, t, precision=HIGHEST)
    ok = valid[None, :, :, None, :, None] & cmask[None, None, None, :, None, :]
    t = jnp.where(ok, t, NEG_BIG)
    return t.reshape(rpb.shape[0], 3, NA_GROUP * GRID_W, NA_SPAN * GRID_W)


def _natten_kernel(q_ref, k_ref, v_ref, bias_ref, o_ref, *, ctx_len, rows):
    j = pl.program_id(2)
    nst = rows // NA_GROUP
    scale = HEAD ** -0.5
    heads = [slice(h * HEAD, (h + 1) * HEAD) for h in range(q_ref.shape[-1] // HEAD)]

    def ctx_scores(sl):
        qb = q_ref[:, sl].astype(BF16)
        vc = v_ref[0:ctx_len, sl].astype(BF16)
        s_ctx = _dot_nt(qb, k_ref[0:ctx_len, sl].astype(BF16)) * scale
        return qb, vc, s_ctx, jnp.max(s_ctx, axis=1, keepdims=True)

    @pl.when(j < nst)
    def _():
        last = j == nst - 1
        cls = jnp.where(j == 0, 0, jnp.where(last, 2, 1))
        span_row = jnp.where(j == 0, 0, jnp.where(last, rows - NA_SPAN, NA_GROUP * j - NA_ROWS // 2))
        start = pl.multiple_of(ctx_len + span_row * GRID_W, GRID_W)
        for h, sl in enumerate(heads):
            qb, vc, s_ctx, m_ctx = ctx_scores(sl)
            kw = k_ref[pl.ds(start, NA_SPAN * GRID_W), sl].astype(BF16)
            vw = v_ref[pl.ds(start, NA_SPAN * GRID_W), sl].astype(BF16)
            s_win = _dot_nt(qb, kw) * scale + bias_ref[h, cls]
            m = jnp.maximum(m_ctx, jnp.max(s_win, axis=1, keepdims=True))
            pw = jnp.exp(s_win - m)
            pc = jnp.exp(s_ctx - m)
            den = jnp.sum(pw, axis=1, keepdims=True) + jnp.sum(pc, axis=1, keepdims=True)
            o = _dot(pw.astype(BF16), vw) + _dot(pc.astype(BF16), vc)
            o_ref[:, sl] = (o / den).astype(BF16)

    @pl.when(j >= nst)
    def _():
        for sl in heads:
            _, vc, s_ctx, m_ctx = ctx_scores(sl)
            pc = jnp.exp(s_ctx - m_ctx)
            o = _dot(pc.astype(BF16), vc) / jnp.sum(pc, axis=1, keepdims=True)
            o_ref[:, sl] = o.astype(BF16)


def _natten(p, bias, ctx_len, col0, width, hb=2):
    b, ta, _ = p.shape
    cw = HEAD * hb
    nh = width // cw
    rows = (ta - ctx_len) // GRID_W
    tq = NA_GROUP * GRID_W
    nst = rows // NA_GROUP
    ncq = ctx_len // tq
    assert rows % NA_GROUP == 0 and rows >= NA_SPAN and ctx_len % tq == 0

    def qmap(cb):
        return lambda i, h, j: (i, jnp.where(j < nst, ncq + j, j - nst), cb + h)

    return pl.pallas_call(
        functools.partial(_natten_kernel, ctx_len=ctx_len, rows=rows),
        out_shape=jax.ShapeDtypeStruct((b, ta, width), BF16),
        grid=(b, nh, nst + ncq),
        in_specs=[pl.BlockSpec((None, tq, cw), qmap(col0 // cw)),
                  pl.BlockSpec((None, ta, cw), lambda i, h, j: (i, 0, (col0 + width) // cw + h)),
                  pl.BlockSpec((None, ta, cw), lambda i, h, j: (i, 0, (col0 + 2 * width) // cw + h)),
                  pl.BlockSpec((hb, 3, tq, NA_SPAN * GRID_W), lambda i, h, j: (h, 0, 0, 0))],
        out_specs=pl.BlockSpec((None, tq, cw), qmap(0)),
        compiler_params=_cparams("parallel", "parallel", "arbitrary"),
        name="natten",
    )(p, p, p, bias)


def _rope_tables(ctx_len, t):
    pos = jnp.arange(t)
    row = (pos // GRID_W).astype(F32)
    col = (pos % GRID_W).astype(F32)
    half = HEAD // 2
    inv = ROPE_THETA ** (-jnp.arange(0, half, 2, dtype=F32) / half)
    ang = jnp.concatenate([row[:, None] * inv, col[:, None] * inv], axis=-1)
    cos, sin = jnp.cos(ang), jnp.sin(ang)
    cosf = jnp.repeat(cos, 2, axis=-1)
    sinf = jnp.stack([-sin, sin], axis=-1).reshape(t, HEAD)
    cosf = jnp.concatenate([jnp.ones((ctx_len, HEAD), F32), cosf], axis=0)
    sinf = jnp.concatenate([jnp.zeros((ctx_len, HEAD), F32), sinf], axis=0)
    return cosf, sinf


def _rope(x, cosf, sinf):
    even = (lax.broadcasted_iota(jnp.int32, x.shape, 1) & 1) == 0
    partner = jnp.where(even, pltpu.roll(x, LANE - 1, 1), pltpu.roll(x, 1, 1))
    return x * cosf + partner * sinf


def _attn_kernel(q_ref, k_ref, v_ref, cq_ref, sq_ref, ck_ref, sk_ref, qn_ref, kn_ref, o_ref, kb_ref, vb_ref,
                 *, skip_tiles):
    j = pl.program_id(2)

    @pl.when(j == 0)
    def _():
        kb_ref[...] = _rope(_rms(k_ref[...]) * kn_ref[...], ck_ref[...], sk_ref[...]).astype(BF16)
        vb_ref[...] = v_ref[...].astype(BF16)

    @pl.when(j < skip_tiles)
    def _():
        o_ref[...] = jnp.zeros_like(o_ref)

    @pl.when(j >= skip_tiles)
    def _():
        k = kb_ref[...]
        v = vb_ref[...]
        cq = cq_ref[...]
        sq = sq_ref[...]
        for g in range(q_ref.shape[-1] // HEAD):
            sl = slice(g * HEAD, (g + 1) * HEAD)
            q = (_rope(_rms(q_ref[:, sl]) * qn_ref[...], cq, sq) * (HEAD ** -0.5 * LOG2E)).astype(BF16)
            s = _dot_nt(q, k)
            p = jnp.exp2(s - jnp.max(s, axis=1, keepdims=True))
            o = _dot(p.astype(BF16), v) / jnp.sum(p, axis=1, keepdims=True)
            o_ref[:, sl] = o.astype(BF16)


def _gqa_attention(p, cosf, sinf, q_norm, k_norm, qw, kw, skip_tiles):
    b, ta, _ = p.shape
    nkv = kw // HEAD
    tq = TOK_TILE
    gw = qw // nkv
    tab_q = pl.BlockSpec((tq, HEAD), lambda i, h, j: (j, 0))
    tab_k = pl.BlockSpec((ta, HEAD), lambda i, h, j: (0, 0))
    vec = pl.BlockSpec((1, HEAD), lambda i, h, j: (0, 0))
    return pl.pallas_call(
        functools.partial(_attn_kernel, skip_tiles=skip_tiles),
        out_shape=jax.ShapeDtypeStruct((b, ta, qw), BF16),
        grid=(b, nkv, ta // tq),
        in_specs=[pl.BlockSpec((None, tq, gw), lambda i, h, j: (i, j, h)),
                  pl.BlockSpec((None, ta, HEAD), lambda i, h, j: (i, 0, qw // HEAD + h)),
                  pl.BlockSpec((None, ta, HEAD), lambda i, h, j: (i, 0, (qw + kw) // HEAD + h)),
                  tab_q, tab_q, tab_k, tab_k, vec, vec],
        out_specs=pl.BlockSpec((None, tq, gw), lambda i, h, j: (i, j, h)),
        scratch_shapes=[pltpu.VMEM((ta, HEAD), BF16), pltpu.VMEM((ta, HEAD), BF16)],
        compiler_params=_cparams("parallel", "parallel", "arbitrary"),
        name="gqa_attention",
    )(p, p, p, cosf, sinf, cosf, sinf, q_norm.reshape(1, HEAD), k_norm.reshape(1, HEAD))


def _s5_operators(a_re, a_im, log_dt, b_re, b_im, c_re, c_im):
    lc, pp, ns = S5_CHUNK, S5_GROUP, S5_STATE
    a_c = lax.complex(a_re.astype(F32), a_im.astype(F32))
    adt = a_c * jnp.exp(log_dt.astype(F32))[..., None]
    a_bar = jnp.exp(adt)
    b_bar = ((a_bar - 1.0) / a_c)[..., None] * lax.complex(b_re.astype(F32), b_im.astype(F32))
    c_mat = lax.complex(c_re.astype(F32), c_im.astype(F32))
    nd, g = a_re.shape[0], a_re.shape[1]
    pos = jnp.arange(lc)
    cw = lc * pp

    def lag_rows(d, lags):
        apw = jnp.exp(adt[d][:, :, None] * lags[None, None, :])
        w = (apw[:, :, :, None] * jnp.swapaxes(c_mat[d], 1, 2)[:, :, None, :]).reshape(g, ns, cw)
        bt = jnp.swapaxes(b_bar[d], 1, 2)
        return (jnp.einsum('gqn,gnk->gqk', jnp.real(bt), jnp.real(w), precision=HIGHEST)
                - jnp.einsum('gqn,gnk->gqk', jnp.imag(bt), jnp.imag(w), precision=HIGHEST))

    kk = lag_rows(0, pos.astype(F32))
    kkr = lag_rows(1, pos[::-1].astype(F32))
    fwd = jnp.stack([jnp.pad(kk[..., :cw - pp * i], ((0, 0), (0, 0), (pp * i, 0))) for i in range(lc)], axis=1)
    bwd = jnp.stack([jnp.pad(kkr[..., pp * (lc - 1 - i):], ((0, 0), (0, 0), (0, pp * (lc - 1 - i)))) for i in range(lc)],
                    axis=1)
    tmat = jnp.stack([fwd, bwd], axis=1).reshape(g, nd, cw, cw)
    steps_after = jnp.stack([lc - 1 - pos, pos]).astype(F32)
    gm = jnp.exp(adt[:, :, None, :] * steps_after[:, None, :, None])[:, :, :, None, :] * jnp.swapaxes(b_bar, 2, 3)[:, :, None]
    gm = gm.reshape(nd, g, lc * pp, ns)
    steps_upto = jnp.stack([pos + 1, lc - pos]).astype(F32)
    hm = c_mat[:, :, None] * jnp.exp(adt[:, :, None, :] * steps_upto[:, None, :, None])[:, :, :, None, :]
    hm = jnp.transpose(hm, (0, 1, 4, 2, 3)).reshape(nd, g, ns, lc * pp)
    hmat = jnp.concatenate([jnp.real(hm), -jnp.imag(hm)], axis=2)
    al = jnp.exp(adt * float(lc))
    al = al[:, :, None, :]
    gd = lambda x: jnp.swapaxes(x, 0, 1)
    return (tmat.astype(BF16), gd(jnp.real(gm)).astype(BF16), gd(jnp.imag(gm)).astype(BF16),
            gd(hmat).astype(BF16), gd(jnp.real(al)), gd(jnp.imag(al)))


def _s5_kernel(u_ref, t_ref, gr_ref, gi_ref, h_ref, ar_ref, ai_ref, d_ref, y_ref, er_ref, ei_ref, xr_ref, xi_ref,
               *, nchunk, nctx, rpc):
    ns = S5_STATE
    u = u_ref[...]
    for dr in range(2):
        er_ref[dr] = _dot(u, gr_ref[dr])
        ei_ref[dr] = _dot(u, gi_ref[dr])

    coef = [(jnp.broadcast_to(ar_ref[dr], (rpc, ns)), jnp.broadcast_to(ai_ref[dr], (rpc, ns))) for dr in range(2)]

    def body(c, carry):
        out = []
        for dr in range(2):
            xr, xi = carry[2 * dr], carry[2 * dr + 1]
            pc = c if dr == 0 else jnp.where(c < nctx, nctx - 1 - c, nchunk - 1 + nctx - c)
            off = pl.multiple_of(pc * rpc, rpc)
            xr_ref[dr, pl.ds(off, rpc), :] = xr
            xi_ref[dr, pl.ds(off, rpc), :] = xi
            ar, ai = coef[dr]
            out.append(ar * xr - ai * xi + er_ref[dr, pl.ds(off, rpc), :])
            out.append(ar * xi + ai * xr + ei_ref[dr, pl.ds(off, rpc), :])
        return tuple(out)

    zero = jnp.zeros((rpc, ns), F32)
    lax.fori_loop(0, nchunk, body, (zero, zero, zero, zero), unroll=4)
    y = d_ref[...] * u.astype(F32)
    for dr in range(2):
        y = y + _dot(u, t_ref[dr])
        y = y + _dot(xr_ref[dr].astype(BF16), h_ref[dr, 0:ns, :]) + _dot(xi_ref[dr].astype(BF16), h_ref[dr, ns:2 * ns, :])
    y_ref[...] = y.astype(BF16)


def _group_perm(lc, pp, ngl):
    n = lc * ngl * pp
    src = jnp.arange(n)
    dst = (src // pp) % ngl * (lc * pp) + src // (ngl * pp) * pp + src % pp
    return (dst[:, None] == jnp.arange(n)[None, :]).astype(BF16)


def _s5_pack_kernel(u_ref, perm_ref, o_ref, *, lc, width):
    ngl = LANE // S5_GROUP
    cw = o_ref.shape[-1]
    for gb in range(width // LANE):
        lhs = jnp.concatenate([u_ref[:, i * width + gb * LANE:i * width + (gb + 1) * LANE] for i in range(lc)], axis=1)
        res = _dot(lhs, perm_ref[...]).astype(BF16)
        for gl in range(ngl):
            o_ref[gb * ngl + gl] = res[:, gl * cw:(gl + 1) * cw]


def _s5_unpack_glu_kernel(y_ref, perm_ref, w_ref, b_ref, o_ref, *, lc, width):
    ngl = LANE // S5_GROUP
    pieces = []
    for gb in range(width // LANE):
        lhs = jnp.concatenate([y_ref[gb * ngl + gl] for gl in range(ngl)], axis=1)
        pieces.append(_dot(lhs, perm_ref[...]))
    for i in range(lc):
        y = jnp.concatenate([p[:, i * LANE:(i + 1) * LANE] for p in pieces], axis=1)
        y = 0.5 * y * (1.0 + jnp.tanh(math.sqrt(2.0 / math.pi) * (y + 0.044715 * (y * y * y))))
        z = _dot(y.astype(BF16), w_ref[...]) + b_ref[...]
        o_ref[:, i * width:(i + 1) * width] = (y * jax.nn.sigmoid(z)).astype(BF16)


def _s5(p, ucol, width, ctx_len, ops, d_skip, w_glu, b_glu):
    tmat, g_re, g_im, hmat, a_re, a_im = ops
    b, ta, _ = p.shape
    lc, pp, ns = S5_CHUNK, S5_GROUP, S5_STATE
    g = width // pp
    ngl = LANE // pp
    nchunk = ta // lc
    rpc = 8
    cw = lc * pp
    rows = nchunk * rpc
    perm = _group_perm(lc, pp, ngl)
    nperm = perm.shape[0]
    u2 = p[..., ucol:ucol + width].astype(BF16).reshape(b, nchunk, lc * width)
    ug = pl.pallas_call(
        functools.partial(_s5_pack_kernel, lc=lc, width=width),
        out_shape=jax.ShapeDtypeStruct((g, b, nchunk, cw), BF16),
        grid=(b,),
        in_specs=[pl.BlockSpec((None, nchunk, lc * width), lambda i: (i, 0, 0)),
                  pl.BlockSpec((nperm, nperm), lambda i: (0, 0))],
        out_specs=pl.BlockSpec((g, None, nchunk, cw), lambda i: (0, i, 0, 0)),
        compiler_params=_cparams("parallel"),
        name="s5_pack",
    )(u2, perm)
    ug = jnp.pad(jnp.swapaxes(ug, 1, 2), ((0, 0), (0, 0), (0, rpc - b), (0, 0))).reshape(g, rows, cw)
    dvec = jnp.tile(d_skip.astype(F32).reshape(g, 1, pp), (1, lc, 1)).reshape(g, 1, cw)
    op = lambda r, c: pl.BlockSpec((None, 2, r, c), lambda i: (i, 0, 0, 0))
    y = pl.pallas_call(
        functools.partial(_s5_kernel, nchunk=nchunk, nctx=ctx_len // lc, rpc=rpc),
        out_shape=jax.ShapeDtypeStruct((g, rows, cw), BF16),
        grid=(g,),
        in_specs=[pl.BlockSpec((None, rows, cw), lambda i: (i, 0, 0)),
                  op(cw, cw), op(cw, ns), op(cw, ns), op(2 * ns, cw), op(1, ns), op(1, ns),
                  pl.BlockSpec((None, 1, cw), lambda i: (i, 0, 0))],
        out_specs=pl.BlockSpec((None, rows, cw), lambda i: (i, 0, 0)),
        scratch_shapes=[pltpu.VMEM((2, rows, ns), F32) for _ in range(4)],
        compiler_params=_cparams("parallel"),
        name="s5_scan",
    )(ug, tmat, g_re, g_im, hmat, a_re, a_im, dvec)
    y = jnp.swapaxes(y.reshape(g, nchunk, rpc, cw)[:, :, :b], 1, 2)
    out = pl.pallas_call(
        functools.partial(_s5_unpack_glu_kernel, lc=lc, width=width),
        out_shape=jax.ShapeDtypeStruct((b, nchunk, lc * width), BF16),
        grid=(b,),
        in_specs=[pl.BlockSpec((g, None, nchunk, cw), lambda i: (0, i, 0, 0)),
                  pl.BlockSpec((nperm, nperm), lambda i: (0, 0)),
                  pl.BlockSpec((width, width), lambda i: (0, 0)),
                  pl.BlockSpec((1, width), lambda i: (0, 0))],
        out_specs=pl.BlockSpec((None, nchunk, lc * width), lambda i: (i, 0, 0)),
        compiler_params=_cparams("parallel"),
        name="s5_unpack_glu",
    )(y, perm.T, w_glu.astype(BF16), b_glu.reshape(1, width))
    return out.reshape(b, ta, width)


def _postmix_kernel(a_ref, b_ref, w_ref, *refs, nx):
    m_ref, gpost_ref, gpre_ref, wr_ref, xo_ref, h_ref, lg_ref = refs[nx:]
    wa = a_ref.shape[-1]
    y = _dot(a_ref[...], w_ref[0:wa, :]) + _dot(b_ref[...], w_ref[wa:, :])
    m = m_ref[...]
    xn = _stream_tile(refs[:nx]) + m[2:3] * (_rms(y) * gpost_ref[...])
    xo_ref[...] = xn
    h2 = _rms(xn) * gpre_ref[...] * (1.0 + m[4:5]) + m[3:4]
    hi = h2.astype(BF16)
    h_ref[...] = hi
    lo = (h2 - hi.astype(F32)).astype(BF16)
    ne = lg_ref.shape[-1]
    both = _dot(hi, wr_ref[...])
    lg_ref[...] = both[:, 0:ne] + both[:, ne:2 * ne] + _dot(lo, wr_ref[:, 0:ne])


def _postmix(mix_a, mix_b, w_out, xs, modtab, g_post, g_pre, w_router):
    (b, ta, d), x_specs, x_args = _stream_operands(xs)
    wa, wb = mix_a.shape[-1], mix_b.shape[-1]
    ne = w_router.shape[-1]
    tok = lambda w: pl.BlockSpec((None, TOK_TILE, w), lambda i, j: (i, j, 0))
    vec = pl.BlockSpec((1, d), lambda i, j: (0, 0))
    wr_hi = w_router.astype(BF16)
    wr2 = jnp.concatenate([wr_hi, (w_router - wr_hi.astype(F32)).astype(BF16)], axis=1)
    return pl.pallas_call(
        functools.partial(_postmix_kernel, nx=len(x_args)),
        out_shape=(jax.ShapeDtypeStruct((b, ta, d), F32),
                   jax.ShapeDtypeStruct((b, ta, d), BF16),
                   jax.ShapeDtypeStruct((b, ta, ne), F32)),
        grid=(b, ta // TOK_TILE),
        in_specs=[tok(wa), tok(wb),
                  pl.BlockSpec((wa + wb, d), lambda i, j: (0, 0))] + x_specs + [
                  _mod_spec(d), vec, vec,
                  pl.BlockSpec((d, 2 * ne), lambda i, j: (0, 0))],
        out_specs=(tok(d), tok(d), tok(ne)),
        compiler_params=_cparams("parallel", "parallel"),
        name="mix_out",
    )(mix_a, mix_b, w_out.astype(BF16), *x_args, modtab, g_post.reshape(1, d), g_pre.reshape(1, d), wr2)


def _select_kernel(lg_ref, pos_ref, gate_ref, *, ctx_len, cap_ctx, cap_lat):
    lg = lg_ref[...]
    ne, ta = lg.shape
    ex = jnp.exp(lg - jnp.max(lg, axis=0, keepdims=True))
    probs = ex / jnp.sum(ex, axis=0, keepdims=True)
    bits = pltpu.bitcast(probs, jnp.int32)
    is_ctx = lax.broadcasted_iota(jnp.int32, (ne, ta), 1) < ctx_len

    def counts(mask):
        mf = jnp.where(mask, 1.0, 0.0)
        return (jnp.sum(jnp.where(is_ctx, mf, 0.0), axis=1, keepdims=True),
                jnp.sum(jnp.where(is_ctx, 0.0, mf), axis=1, keepdims=True))

    def search(i, carry):
        pc, pt = carry
        bit = jnp.left_shift(jnp.int32(1), 30 - i)
        cc, ct = counts(bits >= jnp.where(is_ctx, pc | bit, pt | bit))
        return jnp.where(cc >= cap_ctx, pc | bit, pc), jnp.where(ct >= cap_lat, pt | bit, pt)

    z = jnp.zeros((ne, 1), jnp.int32)
    pc, pt = lax.fori_loop(0, 31, search, (z, z))
    thr = jnp.where(is_ctx, pc, pt)
    gt = bits > thr
    eq = bits == thr
    gc, gl = counts(gt)
    need = jnp.where(is_ctx, cap_ctx - gc, cap_lat - gl)

    nb = ta // LANE
    ut = jnp.where(lax.broadcasted_iota(jnp.int32, (LANE, LANE), 0) <= lax.broadcasted_iota(jnp.int32, (LANE, LANE), 1),
                   1.0, 0.0).astype(BF16)

    def lane_prefix(mask):
        mf = jnp.where(mask, 1.0, 0.0).astype(BF16)
        blocks = jnp.concatenate([mf[:, j * LANE:(j + 1) * LANE] for j in range(nb)], axis=0)
        inc = _dot(blocks, ut)
        outs = []
        off = jnp.zeros((ne, 1), F32)
        for j in range(nb):
            if j * LANE == ctx_len:
                off = jnp.zeros((ne, 1), F32)
            blk = inc[j * ne:(j + 1) * ne]
            outs.append(blk + off)
            off = off + blk[:, LANE - 1:LANE]
        return jnp.concatenate(outs, axis=1)

    sel = gt | (eq & (lane_prefix(eq) <= need))
    slot = lane_prefix(sel) - 1.0 + jnp.where(is_ctx, 0.0, float(cap_ctx))
    pos_ref[...] = jnp.where(sel, slot, -1.0)
    gate_ref[...] = jnp.where(sel, probs, 0.0)


def _moe_select(logits_t, ctx_len, cap_ctx, cap_lat):
    b, ne, ta = logits_t.shape
    spec = pl.BlockSpec((None, ne, ta), lambda i: (i, 0, 0))
    return pl.pallas_call(
        functools.partial(_select_kernel, ctx_len=ctx_len, cap_ctx=cap_ctx, cap_lat=cap_lat),
        out_shape=(jax.ShapeDtypeStruct((b, ne, ta), F32), jax.ShapeDtypeStruct((b, ne, ta), F32)),
        grid=(b,),
        in_specs=[spec],
        out_specs=(spec, spec),
        compiler_params=_cparams("parallel"),
        name="moe_select",
    )(logits_t)


def _slot_windows(pos_t, nt, win, nslot):
    b, ne, ta = pos_t.shape
    pt = pos_t.reshape(b, ne, nt, ta // nt)
    hi = jnp.max(pt, axis=-1).astype(jnp.int32) + 1
    lo = jnp.min(jnp.where(pt >= 0, pt, float(nslot)), axis=-1).astype(jnp.int32)
    lo = jnp.where(hi > 0, lo // 16 * 16, 0)
    npass = jnp.maximum(jnp.max((hi - lo + win - 1) // win, axis=1), 1)
    return jnp.transpose(lo, (0, 2, 1)).reshape(-1), npass.reshape(-1)


def _gather_kernel(start_ref, npass_ref, pos_ref, h_ref, o_ref):
    i = pl.program_id(0)
    eg = pl.program_id(1)
    j = pl.program_id(2)
    ge, tk = pos_ref.shape
    ne = pl.num_programs(1) * ge
    nslot = o_ref.shape[1]
    win = GATHER_WIN
    tile = i * pl.num_programs(2) + j

    @pl.when(j == 0)
    def _():
        o_ref[...] = jnp.zeros_like(o_ref)

    pos = pos_ref[...].astype(jnp.int32)
    slot = lax.broadcasted_iota(jnp.int32, (win, tk), 0)

    def one_pass(p, carry):
        src, blocks = [], []
        for el in range(ge):
            lo = start_ref[tile * ne + eg * ge + el] + p * win
            src.append(pl.multiple_of(jnp.minimum(lo, nslot - win), 16))
            pe = pos[el:el + 1, :]
            hit = (pe >= lo) & (pe < lo + win) & (slot == pe - src[el])
            blocks.append(jnp.where(hit, 1.0, 0.0).astype(BF16))
        part = _dot(jnp.concatenate(blocks, axis=0), h_ref[...]).astype(BF16)
        for el in range(ge):
            o_ref[el, pl.ds(src[el], win), :] += part[el * win:(el + 1) * win]
        return carry

    lax.fori_loop(0, npass_ref[tile], one_pass, 0)


def _moe_gather(pos_t, h, nslot):
    b, ne, ta = pos_t.shape
    d = h.shape[-1]
    ge = 8
    tk = TOK_TILE
    nt = ta // tk
    start, npass = _slot_windows(pos_t, nt, GATHER_WIN, nslot)
    return pl.pallas_call(
        _gather_kernel,
        out_shape=jax.ShapeDtypeStruct((ne, b, nslot, d), BF16),
        grid_spec=pltpu.PrefetchScalarGridSpec(
            num_scalar_prefetch=2,
            grid=(b, ne // ge, nt),
            in_specs=[pl.BlockSpec((None, None, ge, tk), lambda i, g, j, *_: (i, g, 0, j)),
                      pl.BlockSpec((None, tk, d), lambda i, g, j, *_: (i, j, 0))],
            out_specs=pl.BlockSpec((ge, None, nslot, d), lambda i, g, j, *_: (g, i, 0, 0))),
        compiler_params=_cparams("arbitrary", "arbitrary", "arbitrary"),
        name="moe_gather",
    )(start, npass, pos_t.reshape(b, ne // ge, ge, ta), h)


def _ffn_kernel(x_ref, wg_ref, wu_ref, wd_ref, o_ref, hid_ref, *, nf):
    s = pl.program_id(2)
    tf = wg_ref.shape[-1]

    @pl.when(s < nf)
    def _():
        x = x_ref[...]
        g = _dot(x, wg_ref[...].astype(BF16))
        u = _dot(x, wu_ref[...].astype(BF16))
        hid_ref[s] = (_silu(g) * u).astype(BF16)

    @pl.when(s >= nf)
    def _():
        acc = _dot(hid_ref[0], wd_ref[0:tf, :].astype(BF16))
        for f in range(1, nf):
            acc = acc + _dot(hid_ref[f], wd_ref[f * tf:(f + 1) * tf, :].astype(BF16))
        o_ref[...] = acc.astype(BF16)


def _moe_ffn(xin, w_gate, w_up, w_down, layer, tf=512):
    ne, ns, m, d = xin.shape
    ff = w_gate.shape[-1]
    nf = ff // tf
    nd = d // tf
    up = lambda e, s, f: (layer, e, 0, jnp.minimum(f, nf - 1))
    return pl.pallas_call(
        functools.partial(_ffn_kernel, nf=nf),
        out_shape=jax.ShapeDtypeStruct((ne, ns, m, d), BF16),
        grid=(ne, ns, nf + nd),
        in_specs=[pl.BlockSpec((None, None, m, d), lambda e, s, f: (e, s, 0, 0)),
                  pl.BlockSpec((None, None, d, tf), up),
                  pl.BlockSpec((None, None, d, tf), up),
                  pl.BlockSpec((None, None, ff, tf), lambda e, s, f: (layer, e, 0, jnp.maximum(f - nf, 0)))],
        out_specs=pl.BlockSpec((None, None, m, tf), lambda e, s, f: (e, s, 0, jnp.maximum(f - nf, 0))),
        scratch_shapes=[pltpu.VMEM((nf, m, tf), BF16)],
        compiler_params=_cparams("parallel", "parallel", "arbitrary"),
        name="moe_ffn",
    )(xin, w_gate, w_up, w_down)


def _combine_kernel(start_ref, npass_ref, pos_ref, gate_ref, y_hbm, x_ref, m_ref, g_ref, *refs,
                    ctx_tiles, skip, with_next):
    if with_next:
        gn_ref, mn_ref, o_ref, hn_ref, ybuf, acc_ref, sem = refs
    else:
        o_ref, ybuf, acc_ref, sem = refs
    tt, ne = pos_ref.shape
    nslot = y_hbm.shape[2]
    win = COMBINE_WIN
    nj = pl.num_programs(1)
    step = pl.program_id(0) * nj + pl.program_id(1)
    buf = step % 2
    pos = pos_ref[...].astype(jnp.int32)
    gate = gate_ref[...]
    lane = lax.broadcasted_iota(jnp.int32, (tt, win), 1)

    def windows(s, p):
        smp = s // nj
        tile = smp * (nj + skip) + s % nj + skip
        lo = [start_ref[tile * ne + e] + p * win for e in range(ne)]
        return smp, lo, [pl.multiple_of(jnp.minimum(v, nslot - win), 16) for v in lo]

    def window_copy(smp, e, src, slot):
        return pltpu.make_async_copy(y_hbm.at[e, smp, pl.ds(src, win), :], ybuf.at[slot, pl.ds(e * win, win), :],
                                     sem.at[slot, e])

    def fetch(s, p, slot):
        smp, _, src = windows(s, p)
        for e in range(ne):
            window_copy(smp, e, src[e], slot).start()

    def one_pass(p):
        smp, lo, src = windows(step, p)
        blocks = []
        for e in range(ne):
            pe = pos[:, e:e + 1]
            hit = (pe >= lo[e]) & (pe < lo[e] + win) & (lane == pe - src[e])
            blocks.append(jnp.where(hit, gate[:, e:e + 1], 0.0).astype(BF16))
        w = jnp.concatenate(blocks, axis=1)
        for e in range(ne):
            window_copy(smp, e, src[e], buf).wait()
        return _dot(w, ybuf[buf])

    @pl.when(step == 0)
    def _():
        fetch(step, 0, buf)

    @pl.when(step + 1 < pl.num_programs(0) * nj)
    def _():
        fetch(step + 1, 0, 1 - buf)

    acc_ref[...] = one_pass(0)

    def extra(p, carry):
        fetch(step, p, buf)
        acc_ref[...] += one_pass(p)
        return carry

    lax.fori_loop(1, npass_ref[windows(step, 0)[0] * (nj + skip) + step % nj + skip], extra, 0)
    m = m_ref[...]
    is_ctx = pl.program_id(1) + skip < ctx_tiles
    pick = lambda k: jnp.where(is_ctx, m[0, k:k + 1], m[1, k:k + 1])
    xn = x_ref[...] + pick(5) * (_rms(acc_ref[...]) * g_ref[...])
    o_ref[...] = xn
    if with_next:
        mn = mn_ref[...]
        pick_n = lambda k: jnp.where(is_ctx, mn[0, k:k + 1], mn[1, k:k + 1])
        hn_ref[...] = (_rms(xn) * gn_ref[...] * (1.0 + pick_n(1)) + pick_n(0)).astype(BF16)


def _moe_combine(pos_t, gate_t, yout, xa, modtab, g_post, ctx_len, nxt=None, latent_only=False):
    b, ta, d = xa.shape
    ne, _, nslot, _ = yout.shape
    tt = TOK_TILE
    nt = ta // tt
    skip = ctx_len // tt if latent_only else 0
    win = COMBINE_WIN
    start, npass = _slot_windows(pos_t, nt, win, nslot)
    tok = lambda w: pl.BlockSpec((None, tt, w), lambda i, j, *_: (i, j + skip, 0))
    out_tok = lambda: pl.BlockSpec((None, tt, d), lambda i, j, *_: (i, j, 0))
    mods = pl.BlockSpec((None, 2, 6, d), lambda i, j, *_: (i, 0, 0, 0))
    vec = pl.BlockSpec((1, d), lambda i, j, *_: (0, 0))
    xo = jax.ShapeDtypeStruct((b, ta - skip * tt, d), F32)
    extra_in, extra_args = ([vec, mods], [nxt[0].reshape(1, d), nxt[1]]) if nxt else ([], [])
    return pl.pallas_call(
        functools.partial(_combine_kernel, ctx_tiles=ctx_len // tt, skip=skip, with_next=bool(nxt)),
        out_shape=(xo, jax.ShapeDtypeStruct((b, ta - skip * tt, d), BF16)) if nxt else xo,
        grid_spec=pltpu.PrefetchScalarGridSpec(
            num_scalar_prefetch=2,
            grid=(b, nt - skip),
            in_specs=[tok(ne), tok(ne),
                      pl.BlockSpec(memory_space=pl.ANY),
                      tok(d), mods, vec] + extra_in,
            out_specs=(out_tok(), out_tok()) if nxt else out_tok(),
            scratch_shapes=[pltpu.VMEM((2, ne * win, d), BF16),
                            pltpu.VMEM((tt, d), F32),
                            pltpu.SemaphoreType.DMA((2, ne))]),
        compiler_params=_cparams("arbitrary", "arbitrary"),
        name="moe_combine",
    )(start, npass, jnp.swapaxes(pos_t, 1, 2), jnp.swapaxes(gate_t, 1, 2), yout, xa, modtab,
      g_post.reshape(1, d), *extra_args)


def _ec_moe(logits_t, h, xa, modtab, g_post, w_gate, w_up, w_down, layer, ctx_len, nxt=None, latent_only=False):
    b, ne, ta = logits_t.shape
    d = h.shape[-1]
    cap_ctx = max(1, EC_CAPACITY * ctx_len // ne)
    cap_lat = max(1, EC_CAPACITY * (ta - ctx_len) // ne)
    nslot = cap_ctx + cap_lat
    pos_t, gate_t = _moe_select(logits_t, ctx_len, cap_ctx, cap_lat)
    xin = _moe_gather(pos_t, h, nslot)
    pair = 2 if b % 2 == 0 else 1
    yout = _moe_ffn(xin.reshape(ne, b // pair, pair * nslot, d), w_gate, w_up, w_down, layer)
    yout = yout.reshape(ne, b, nslot, d)
    return _moe_combine(pos_t, gate_t, yout, xa, modtab, g_post, ctx_len, nxt, latent_only)


def kernel(x, c, ctx, c_ctx, w_mod, b_mod, g_mix_pre, g_mix_post, g_ffn_pre, g_ffn_post, w_router, w_exp_gate, w_exp_up, w_exp_down, ev_w_in, ev_w_out, hgrn_lb, hgrn_g_norm, na_rpb, od_w_in, od_w_out, q_norm, k_norm, s5_a_re, s5_a_im, s5_log_dt, s5_b_re, s5_b_im, s5_c_re, s5_c_im, s5_d, s5_w_glu, s5_b_glu):
    b, t, d = x.shape
    ctx_len = ctx.shape[1]
    depth = w_mod.shape[0]
    assert depth == 2 and b <= 7
    ta = ctx_len + t
    a_width = d // 2
    s5_width = d // 4
    cq_width = d - s5_width
    ckv_width = cq_width // 3

    cc = jnp.concatenate([c, c_ctx[None], jnp.zeros((7 - b, d), F32)], axis=0)
    mod = _modulation(cc, w_mod, b_mod)
    mod_lat = mod[:, :b].reshape(depth, b, 1, 6, d)
    mod_ctx = jnp.broadcast_to(mod[:, b].reshape(depth, 1, 1, 6, d), (depth, b, 1, 6, d))
    modtab = jnp.concatenate([mod_ctx, mod_lat], axis=2)

    lb_all = jnp.cumsum(jax.nn.softmax(hgrn_lb.astype(F32), axis=0), axis=0)
    hx = _prenorm((ctx, x), g_mix_pre[0], modtab[0])
    p = _matmul(hx.reshape(b * ta, d), ev_w_in[0]).reshape(b, ta, -1)
    o_f, o_b = _hgrn(p, lb_all[0], ctx_len, a_width)
    mix_a = _hgrn_readout(o_f, o_b, p, hgrn_g_norm[0], a_width)
    mix_b = _natten(p, _natten_bias(na_rpb[0]), ctx_len, 5 * a_width, d - a_width)
    xa, h2, logits = _postmix(mix_a, mix_b, ev_w_out[0], (ctx, x), modtab[0], g_mix_post[0], g_ffn_pre[0], w_router[0])
    xa, hx = _ec_moe(jnp.swapaxes(logits, 1, 2), h2, xa, modtab[0], g_ffn_post[0], w_exp_gate, w_exp_up, w_exp_down, 0,
                     ctx_len, nxt=(g_mix_pre[1], modtab[1]))

    p = _matmul(hx.reshape(b * ta, d), od_w_in[0]).reshape(b, ta, -1)
    cosf, sinf = _rope_tables(ctx_len, t)
    mix_a = _gqa_attention(p, cosf, sinf, q_norm[0], k_norm[0], cq_width, ckv_width, ctx_len // TOK_TILE)
    ops = _s5_operators(s5_a_re[0], s5_a_im[0], s5_log_dt[0], s5_b_re[0], s5_b_im[0], s5_c_re[0], s5_c_im[0])
    mix_b = _s5(p, cq_width + 2 * ckv_width, s5_width, ctx_len, ops, s5_d[0], s5_w_glu[0], s5_b_glu[0])
    xa, h2, logits = _postmix(mix_a, mix_b, od_w_out[0], xa, modtab[1], g_mix_post[1], g_ffn_pre[1], w_router[1])
    return _ec_moe(jnp.swapaxes(logits, 1, 2), h2, xa, modtab[1], g_ffn_post[1], w_exp_gate, w_exp_up, w_exp_down, 1,
                   ctx_len, latent_only=True)
```

```python
import functools
import math

import jax
import jax.numpy as jnp
from jax import lax
from jax.experimental import pallas as pl
from jax.experimental.pallas import tpu as pltpu

F32 = jnp.float32
BF16 = jnp.bfloat16
HIGHEST = lax.Precision.HIGHEST
EPS = 1e-6

LANE = 128
TOK_TILE = 256
VMEM_LIMIT = 52 << 20

GRID_W = 64
NA_ROWS = 8
NA_COLS = 16
NA_GROUP = 4
NA_SPAN = 12
HEAD = 128
A_CHUNK = 64
N_EXPERTS = 16
EC_CAPACITY = 2
S5_GROUP = 16
S5_STATE = 64
S5_CHUNK = 16
COMBINE_WIN = 64
ATTN_TILE = 256
GATHER_WIN = 64
ROPE_THETA = 10000.0
NEG_BIG = -1e30
LOG2E = 1.4426950408889634


def _cparams(*sem):
    return pltpu.CompilerParams(dimension_semantics=sem, vmem_limit_bytes=VMEM_LIMIT)


def _dot(a, b):
    return jnp.dot(a, b, preferred_element_type=F32)


def _dot_nt(a, b):
    return lax.dot_general(a, b, (((1,), (1,)), ((), ())), preferred_element_type=F32)


def _dot_tn(a, b):
    return lax.dot_general(a, b, (((0,), (0,)), ((), ())), preferred_element_type=F32)


def _rms(x):
    return x * lax.rsqrt(jnp.mean(x * x, axis=-1, keepdims=True) + EPS)


def _silu(x):
    return x * jax.nn.sigmoid(x)


def _mod_kernel(c_ref, w_ref, b_ref, o_ref):
    o_ref[...] = jnp.dot(_silu(c_ref[...]), w_ref[...], preferred_element_type=F32, precision=HIGHEST) + b_ref[...]


def _modulation(cc, w_mod, b_mod):
    depth, d, n = w_mod.shape
    tn = 1024
    return pl.pallas_call(
        _mod_kernel,
        out_shape=jax.ShapeDtypeStruct((depth, cc.shape[0], n), F32),
        grid=(depth, n // tn),
        in_specs=[pl.BlockSpec(cc.shape, lambda l, j: (0, 0)),
                  pl.BlockSpec((None, d, tn), lambda l, j: (l, 0, j)),
                  pl.BlockSpec((None, 1, tn), lambda l, j: (l, 0, j))],
        out_specs=pl.BlockSpec((None, cc.shape[0], tn), lambda l, j: (l, 0, j)),
        compiler_params=_cparams("arbitrary", "arbitrary"),
        name="modulation",
    )(cc, w_mod, b_mod.reshape(depth, 1, n))


def _mod_spec(d):
    return pl.BlockSpec((None, None, 6, d), lambda b, j: (b, jnp.minimum(j, 1), 0, 0))


def _stream_operands(xs):
    if not isinstance(xs, tuple):
        b, ta, d = xs.shape
        return (b, ta, d), [pl.BlockSpec((None, TOK_TILE, d), lambda i, j: (i, j, 0))], [xs]
    ctx, x = xs
    b, t, d = x.shape
    assert ctx.shape[1] == TOK_TILE
    return (b, TOK_TILE + t, d), [pl.BlockSpec((None, TOK_TILE, d), lambda i, j: (i, 0, 0)),
                                  pl.BlockSpec((None, TOK_TILE, d), lambda i, j: (i, jnp.maximum(j - 1, 0), 0))], [ctx, x]


def _stream_tile(refs):
    if len(refs) == 1:
        return refs[0][...]
    return jnp.where(pl.program_id(1) == 0, refs[0][...], refs[1][...])


def _prenorm_kernel(*refs, nx):
    g_ref, m_ref, o_ref = refs[nx:]
    m = m_ref[...]
    y = _rms(_stream_tile(refs[:nx])) * g_ref[...]
    o_ref[...] = (y * (1.0 + m[1:2]) + m[0:1]).astype(BF16)


def _prenorm(xs, g, modtab):
    (b, ta, d), x_specs, x_args = _stream_operands(xs)
    return pl.pallas_call(
        functools.partial(_prenorm_kernel, nx=len(x_args)),
        out_shape=jax.ShapeDtypeStruct((b, ta, d), BF16),
        grid=(b, ta // TOK_TILE),
        in_specs=x_specs + [pl.BlockSpec((1, d), lambda i, j: (0, 0)), _mod_spec(d)],
        out_specs=pl.BlockSpec((None, TOK_TILE, d), lambda i, j: (i, j, 0)),
        compiler_params=_cparams("parallel", "parallel"),
        name="prenorm",
    )(*x_args, g.reshape(1, d), modtab)


def _mm_kernel(a_ref, w_ref, o_ref, wb_ref):
    @pl.when(pl.program_id(1) == 0)
    def _():
        wb_ref[...] = w_ref[...].astype(BF16)

    o_ref[...] = _dot(a_ref[...], wb_ref[...]).astype(o_ref.dtype)


def _matmul(a, w, tm=1024, tn=1024, out_dtype=F32):
    m, k = a.shape
    n = w.shape[1]
    return pl.pallas_call(
        _mm_kernel,
        out_shape=jax.ShapeDtypeStruct((m, n), out_dtype),
        grid=(n // tn, m // tm),
        in_specs=[pl.BlockSpec((tm, k), lambda j, i: (i, 0)),
                  pl.BlockSpec((k, tn), lambda j, i: (0, j))],
        out_specs=pl.BlockSpec((tm, tn), lambda j, i: (i, j)),
        scratch_shapes=[pltpu.VMEM((k, tn), BF16)],
        compiler_params=_cparams("arbitrary", "arbitrary"),
        name="proj_in",
    )(a, w)


def _hgrn_masks(chunk):
    row = lax.broadcasted_iota(jnp.int32, (chunk, LANE), 0)
    ti = lax.broadcasted_iota(jnp.int32, (chunk, chunk), 0)
    si = lax.broadcasted_iota(jnp.int32, (chunk, chunk), 1)
    levels = [(((row >> lvl) & 1) == 1, (ti >> (lvl + 1)) == (si >> (lvl + 1))) for lvl in range(chunk.bit_length() - 1)]
    return ti == si, levels


def _hgrn_chunk(q, fr, v, lb, st, rev, masks):
    f = lb + (1.0 - lb) * jax.nn.sigmoid(fr)
    kk = 1.0 - f
    lf = jnp.log(f)
    chunk = q.shape[0]
    diag, levels = masks
    att = jnp.where(diag, _dot_nt(q.astype(BF16), kk.astype(BF16)), 0.0)
    p_in = lf
    r_ex = jnp.zeros_like(lf)
    tot = lf
    for lvl, (bit, same) in enumerate(levels):
        step = 1 << lvl
        up = pltpu.roll(tot, step, 0)
        dn = pltpu.roll(tot, chunk - step, 0)
        is_q = jnp.logical_not(bit) if rev else bit
        qf = jnp.where(is_q, jnp.exp(p_in) * q, 0.0).astype(BF16)
        kf = jnp.where(is_q, 0.0, jnp.exp(r_ex) * kk).astype(BF16)
        att = att + jnp.where(same, _dot_nt(qf, kf), 0.0)
        if rev:
            p_in = p_in + jnp.where(bit, 0.0, dn)
            r_ex = r_ex + jnp.where(bit, up, 0.0)
        else:
            p_in = p_in + jnp.where(bit, up, 0.0)
            r_ex = r_ex + jnp.where(bit, 0.0, dn)
        tot = tot + jnp.where(bit, up, dn)
    vb = v.astype(BF16)
    o = _dot_nt((q * jnp.exp(p_in)).astype(BF16), st.astype(BF16)) + _dot(att.astype(BF16), vb)
    kd = (kk * jnp.exp(r_ex)).astype(BF16)
    st_new = st * jnp.exp(tot[0:1]) + _dot_tn(vb, kd)
    return o, st_new


def _hgrn_kernel(qf_ref, ff_ref, vf_ref, qb_ref, fb_ref, vb_ref, lb_ref, of_ref, ob_ref, st_ref, *, chunk, hb):
    @pl.when(pl.program_id(2) == 0)
    def _():
        st_ref[...] = jnp.zeros_like(st_ref)

    masks = _hgrn_masks(chunk)
    for h in range(hb):
        sl = slice(h * HEAD, (h + 1) * HEAD)
        o, st = _hgrn_chunk(qf_ref[:, sl], ff_ref[:, sl], vf_ref[:, sl], lb_ref[0:1, sl], st_ref[0, h], False, masks)
        of_ref[:, sl] = o
        st_ref[0, h] = st
        o, st = _hgrn_chunk(qb_ref[:, sl], fb_ref[:, sl], vb_ref[:, sl], lb_ref[1:2, sl], st_ref[1, h], True, masks)
        ob_ref[:, sl] = o
        st_ref[1, h] = st


def _hgrn(p, lb, ctx_len, width, hb=8):
    b, ta, _ = p.shape
    chunk = A_CHUNK
    n = ta // chunk
    nc = ctx_len // chunk
    cw = HEAD * hb
    nh = width // cw
    bwd = lambda j: jnp.where(j < nc, nc - 1 - j, n - 1 + nc - j)

    def spec(base, rev):
        return pl.BlockSpec((None, chunk, cw), lambda i, h, j: (i, bwd(j) if rev else j, base // cw + h))

    out = jax.ShapeDtypeStruct((b, ta, width), F32)
    return pl.pallas_call(
        functools.partial(_hgrn_kernel, chunk=chunk, hb=hb),
        out_shape=(out, out),
        grid=(b, nh, n),
        in_specs=[spec(0, False), spec(width, False), spec(3 * width, False),
                  spec(0, True), spec(2 * width, True), spec(3 * width, True),
                  pl.BlockSpec((2, cw), lambda i, h, j: (0, h))],
        out_specs=(spec(0, False), spec(0, True)),
        scratch_shapes=[pltpu.VMEM((2, hb, HEAD, HEAD), F32)],
        compiler_params=_cparams("parallel", "parallel", "arbitrary"),
        name="hgrn_scan",
    )(p, p, p, p, p, p, lb)


def _hgrn_readout_kernel(of_ref, ob_ref, g_ref, gn_ref, o_ref):
    for h in range(o_ref.shape[-1] // HEAD):
        sl = slice(h * HEAD, (h + 1) * HEAD)
        o = of_ref[:, sl] + ob_ref[:, sl]
        o_ref[:, sl] = (_rms(o) * gn_ref[...] * _silu(g_ref[:, sl])).astype(BF16)


def _hgrn_readout(o_f, o_b, p, g_norm, width):
    b, ta, _ = o_f.shape
    spec = pl.BlockSpec((None, TOK_TILE, width), lambda i, j: (i, j, 0))
    return pl.pallas_call(
        _hgrn_readout_kernel,
        out_shape=jax.ShapeDtypeStruct((b, ta, width), BF16),
        grid=(b, ta // TOK_TILE),
        in_specs=[spec, spec,
                  pl.BlockSpec((None, TOK_TILE, width), lambda i, j: (i, j, 4)),
                  pl.BlockSpec((1, HEAD), lambda i, j: (0, 0))],
        out_specs=spec,
        compiler_params=_cparams("parallel", "parallel"),
        name="hgrn_readout",
    )(o_f, o_b, p, g_norm.reshape(1, HEAD))


def _natten_bias(rpb):
    nrow, ncol = 2 * NA_ROWS - 1, 2 * NA_COLS - 1
    col = jnp.arange(GRID_W)
    col_start = jnp.clip(col - NA_COLS // 2, 0, GRID_W - NA_COLS)
    cmask = (col[None, :] >= col_start[:, None]) & (col[None, :] < col_start[:, None] + NA_COLS)
    dc = col[None, :] - col[:, None] + NA_COLS - 1
    cm = (cmask[:, :, None] & (dc[:, :, None] == jnp.arange(ncol))).astype(F32)
    cls = jnp.arange(3)[:, None, None]
    i = jnp.arange(NA_GROUP)[None, :, None]
    w = jnp.arange(NA_SPAN)[None, None, :]
    first_w = jnp.where(cls == 0, 0, jnp.where(cls == 1, i, NA_SPAN - NA_ROWS))
    valid = (w >= first_w) & (w < first_w + NA_ROWS)
    span0 = jnp.where(cls == 0, 0, jnp.where(cls == 1, -(NA_ROWS // 2), NA_GROUP - NA_SPAN))
    dr = span0 + w - i + NA_ROWS - 1
    rm = (valid[..., None] & (dr[..., None] == jnp.arange(nrow))).astype(F32)
    t = jnp.einsum('qkb,hab->haqk', cm, rpb.astype(F32), precision=HIGHEST)
    t = jnp.einsum('ciwa,haqk->hciqwk', rm, t, precision=HIGHEST)
    ok = valid[None, :, :, None, :, None] & cmask[None, None, None, :, None, :]
    t = jnp.where(ok, t, NEG_BIG)
    return t.reshape(rpb.shape[0], 3, NA_GROUP * GRID_W, NA_SPAN * GRID_W)


def _natten_kernel(q_ref, k_ref, v_ref, bias_ref, o_ref, *, ctx_len, rows):
    j = pl.program_id(2)
    nst = rows // NA_GROUP
    scale = HEAD ** -0.5
    heads = [slice(h * HEAD, (h + 1) * HEAD) for h in range(q_ref.shape[-1] // HEAD)]

    def ctx_scores(sl):
        qb = q_ref[:, sl].astype(BF16)
        vc = v_ref[0:ctx_len, sl].astype(BF16)
        s_ctx = _dot_nt(qb, k_ref[0:ctx_len, sl].astype(BF16)) * scale
        return qb, vc, s_ctx, jnp.max(s_ctx, axis=1, keepdims=True)

    @pl.when(j < nst)
    def _():
        last = j == nst - 1
        cls = jnp.where(j == 0, 0, jnp.where(last, 2, 1))
        span_row = jnp.where(j == 0, 0, jnp.where(last, rows - NA_SPAN, NA_GROUP * j - NA_ROWS // 2))
        start = pl.multiple_of(ctx_len + span_row * GRID_W, GRID_W)
        for h, sl in enumerate(heads):
            qb, vc, s_ctx, m_ctx = ctx_scores(sl)
            kw = k_ref[pl.ds(start, NA_SPAN * GRID_W), sl].astype(BF16)
            vw = v_ref[pl.ds(start, NA_SPAN * GRID_W), sl].astype(BF16)
            s_win = _dot_nt(qb, kw) * scale + bias_ref[h, cls]
            m = jnp.maximum(m_ctx, jnp.max(s_win, axis=1, keepdims=True))
            pw = jnp.exp(s_win - m)
            pc = jnp.exp(s_ctx - m)
            den = jnp.sum(pw, axis=1, keepdims=True) + jnp.sum(pc, axis=1, keepdims=True)
            o = _dot(pw.astype(BF16), vw) + _dot(pc.astype(BF16), vc)
            o_ref[:, sl] = (o / den).astype(BF16)

    @pl.when(j >= nst)
    def _():
        for sl in heads:
            _, vc, s_ctx, m_ctx = ctx_scores(sl)
            pc = jnp.exp(s_ctx - m_ctx)
            o = _dot(pc.astype(BF16), vc) / jnp.sum(pc, axis=1, keepdims=True)
            o_ref[:, sl] = o.astype(BF16)


def _natten(p, bias, ctx_len, col0, width, hb=2):
    b, ta, _ = p.shape
    cw = HEAD * hb
    nh = width // cw
    rows = (ta - ctx_len) // GRID_W
    tq = NA_GROUP * GRID_W
    nst = rows // NA_GROUP
    ncq = ctx_len // tq
    assert rows % NA_GROUP == 0 and rows >= NA_SPAN and ctx_len % tq == 0

    def qmap(cb):
        return lambda i, h, j: (i, jnp.where(j < nst, ncq + j, j - nst), cb + h)

    return pl.pallas_call(
        functools.partial(_natten_kernel, ctx_len=ctx_len, rows=rows),
        out_shape=jax.ShapeDtypeStruct((b, ta, width), BF16),
        grid=(b, nh, nst + ncq),
        in_specs=[pl.BlockSpec((None, tq, cw), qmap(col0 // cw)),
                  pl.BlockSpec((None, ta, cw), lambda i, h, j: (i, 0, (col0 + width) // cw + h)),
                  pl.BlockSpec((None, ta, cw), lambda i, h, j: (i, 0, (col0 + 2 * width) // cw + h)),
                  pl.BlockSpec((hb, 3, tq, NA_SPAN * GRID_W), lambda i, h, j: (h, 0, 0, 0))],
        out_specs=pl.BlockSpec((None, tq, cw), qmap(0)),
        compiler_params=_cparams("parallel", "parallel", "arbitrary"),
        name="natten",
    )(p, p, p, bias)


def _rope_tables(ctx_len, t):
    pos = jnp.arange(t)
    row = (pos // GRID_W).astype(F32)
    col = (pos % GRID_W).astype(F32)
    half = HEAD // 2
    inv = ROPE_THETA ** (-jnp.arange(0, half, 2, dtype=F32) / half)
    ang = jnp.concatenate([row[:, None] * inv, col[:, None] * inv], axis=-1)
    cos, sin = jnp.cos(ang), jnp.sin(ang)
    cosf = jnp.repeat(cos, 2, axis=-1)
    sinf = jnp.stack([-sin, sin], axis=-1).reshape(t, HEAD)
    cosf = jnp.concatenate([jnp.ones((ctx_len, HEAD), F32), cosf], axis=0)
    sinf = jnp.concatenate([jnp.zeros((ctx_len, HEAD), F32), sinf], axis=0)
    return cosf, sinf


def _rope(x, cosf, sinf):
    even = (lax.broadcasted_iota(jnp.int32, x.shape, 1) & 1) == 0
    partner = jnp.where(even, pltpu.roll(x, LANE - 1, 1), pltpu.roll(x, 1, 1))
    return x * cosf + partner * sinf


def _attn_kernel(q_ref, k_ref, v_ref, cq_ref, sq_ref, ck_ref, sk_ref, qn_ref, kn_ref, o_ref, kb_ref, vb_ref,
                 *, skip_tiles):
    j = pl.program_id(2)

    @pl.when(j == 0)
    def _():
        kb_ref[...] = _rope(_rms(k_ref[...]) * kn_ref[...], ck_ref[...], sk_ref[...]).astype(BF16)
        vb_ref[...] = v_ref[...].astype(BF16)

    @pl.when(j < skip_tiles)
    def _():
        o_ref[...] = jnp.zeros_like(o_ref)

    @pl.when(j >= skip_tiles)
    def _():
        k = kb_ref[...]
        v = vb_ref[...]
        cq = cq_ref[...]
        sq = sq_ref[...]
        for g in range(q_ref.shape[-1] // HEAD):
            sl = slice(g * HEAD, (g + 1) * HEAD)
            q = (_rope(_rms(q_ref[:, sl]) * qn_ref[...], cq, sq) * (HEAD ** -0.5 * LOG2E)).astype(BF16)
            s = _dot_nt(q, k)
            p = jnp.exp2(s - jnp.max(s, axis=1, keepdims=True))
            o = _dot(p.astype(BF16), v) / jnp.sum(p, axis=1, keepdims=True)
            o_ref[:, sl] = o.astype(BF16)


def _gqa_attention(p, cosf, sinf, q_norm, k_norm, qw, kw, skip_rows):
    b, ta, _ = p.shape
    nkv = kw // HEAD
    tq = ATTN_TILE
    skip_tiles = skip_rows // tq
    gw = qw // nkv
    tab_q = pl.BlockSpec((tq, HEAD), lambda i, h, j: (j, 0))
    tab_k = pl.BlockSpec((ta, HEAD), lambda i, h, j: (0, 0))
    vec = pl.BlockSpec((1, HEAD), lambda i, h, j: (0, 0))
    return pl.pallas_call(
        functools.partial(_attn_kernel, skip_tiles=skip_tiles),
        out_shape=jax.ShapeDtypeStruct((b, ta, qw), BF16),
        grid=(b, nkv, ta // tq),
        in_specs=[pl.BlockSpec((None, tq, gw), lambda i, h, j: (i, j, h)),
                  pl.BlockSpec((None, ta, HEAD), lambda i, h, j: (i, 0, qw // HEAD + h)),
                  pl.BlockSpec((None, ta, HEAD), lambda i, h, j: (i, 0, (qw + kw) // HEAD + h)),
                  tab_q, tab_q, tab_k, tab_k, vec, vec],
        out_specs=pl.BlockSpec((None, tq, gw), lambda i, h, j: (i, j, h)),
        scratch_shapes=[pltpu.VMEM((ta, HEAD), BF16), pltpu.VMEM((ta, HEAD), BF16)],
        compiler_params=_cparams("parallel", "parallel", "arbitrary"),
        name="gqa_attention",
    )(p, p, p, cosf, sinf, cosf, sinf, q_norm.reshape(1, HEAD), k_norm.reshape(1, HEAD))


def _s5_operators(a_re, a_im, log_dt, b_re, b_im, c_re, c_im):
    lc, pp, ns = S5_CHUNK, S5_GROUP, S5_STATE
    a_c = lax.complex(a_re.astype(F32), a_im.astype(F32))
    adt = a_c * jnp.exp(log_dt.astype(F32))[..., None]
    a_bar = jnp.exp(adt)
    b_bar = ((a_bar - 1.0) / a_c)[..., None] * lax.complex(b_re.astype(F32), b_im.astype(F32))
    c_mat = lax.complex(c_re.astype(F32), c_im.astype(F32))
    nd, g = a_re.shape[0], a_re.shape[1]
    pos = jnp.arange(lc)
    cw = lc * pp

    def lag_rows(d, lags):
        apw = jnp.exp(adt[d][:, :, None] * lags[None, None, :])
        w = (apw[:, :, :, None] * jnp.swapaxes(c_mat[d], 1, 2)[:, :, None, :]).reshape(g, ns, cw)
        bt = jnp.swapaxes(b_bar[d], 1, 2)
        return (jnp.einsum('gqn,gnk->gqk', jnp.real(bt), jnp.real(w), precision=HIGHEST)
                - jnp.einsum('gqn,gnk->gqk', jnp.imag(bt), jnp.imag(w), precision=HIGHEST))

    kk = lag_rows(0, pos.astype(F32))
    kkr = lag_rows(1, pos[::-1].astype(F32))
    fwd = jnp.stack([jnp.pad(kk[..., :cw - pp * i], ((0, 0), (0, 0), (pp * i, 0))) for i in range(lc)], axis=1)
    bwd = jnp.stack([jnp.pad(kkr[..., pp * (lc - 1 - i):], ((0, 0), (0, 0), (0, pp * (lc - 1 - i)))) for i in range(lc)],
                    axis=1)
    tmat = jnp.stack([fwd, bwd], axis=1).reshape(g, nd, cw, cw)
    steps_after = jnp.stack([lc - 1 - pos, pos]).astype(F32)
    gm = jnp.exp(adt[:, :, None, :] * steps_after[:, None, :, None])[:, :, :, None, :] * jnp.swapaxes(b_bar, 2, 3)[:, :, None]
    gm = gm.reshape(nd, g, lc * pp, ns)
    steps_upto = jnp.stack([pos + 1, lc - pos]).astype(F32)
    hm = c_mat[:, :, None] * jnp.exp(adt[:, :, None, :] * steps_upto[:, None, :, None])[:, :, :, None, :]
    hm = jnp.transpose(hm, (0, 1, 4, 2, 3)).reshape(nd, g, ns, lc * pp)
    hmat = jnp.concatenate([jnp.real(hm), -jnp.imag(hm)], axis=2)
    al = jnp.exp(adt * float(lc))
    al = al[:, :, None, :]
    gd = lambda x: jnp.swapaxes(x, 0, 1)
    return (tmat.astype(BF16), gd(jnp.real(gm)).astype(BF16), gd(jnp.imag(gm)).astype(BF16),
            gd(hmat).astype(BF16), gd(jnp.real(al)), gd(jnp.imag(al)))


def _s5_kernel(u_ref, t_ref, gr_ref, gi_ref, h_ref, ar_ref, ai_ref, d_ref, y_ref, er_ref, ei_ref, xr_ref, xi_ref,
               *, nchunk, nctx, rpc):
    ns = S5_STATE
    u = u_ref[...]
    for dr in range(2):
        er_ref[dr] = _dot(u, gr_ref[dr])
        ei_ref[dr] = _dot(u, gi_ref[dr])

    coef = [(jnp.broadcast_to(ar_ref[dr], (rpc, ns)), jnp.broadcast_to(ai_ref[dr], (rpc, ns))) for dr in range(2)]

    def body(c, carry):
        out = []
        for dr in range(2):
            xr, xi = carry[2 * dr], carry[2 * dr + 1]
            pc = c if dr == 0 else jnp.where(c < nctx, nctx - 1 - c, nchunk - 1 + nctx - c)
            off = pl.multiple_of(pc * rpc, rpc)
            xr_ref[dr, pl.ds(off, rpc), :] = xr
            xi_ref[dr, pl.ds(off, rpc), :] = xi
            ar, ai = coef[dr]
            out.append(ar * xr - ai * xi + er_ref[dr, pl.ds(off, rpc), :])
            out.append(ar * xi + ai * xr + ei_ref[dr, pl.ds(off, rpc), :])
        return tuple(out)

    zero = jnp.zeros((rpc, ns), F32)
    lax.fori_loop(0, nchunk, body, (zero, zero, zero, zero), unroll=4)
    y = d_ref[...] * u.astype(F32)
    for dr in range(2):
        y = y + _dot(u, t_ref[dr])
        y = y + _dot(xr_ref[dr].astype(BF16), h_ref[dr, 0:ns, :]) + _dot(xi_ref[dr].astype(BF16), h_ref[dr, ns:2 * ns, :])
    y_ref[...] = y.astype(BF16)


def _group_perm(lc, pp, ngl):
    n = lc * ngl * pp
    src = jnp.arange(n)
    dst = (src // pp) % ngl * (lc * pp) + src // (ngl * pp) * pp + src % pp
    return (dst[:, None] == jnp.arange(n)[None, :]).astype(BF16)


def _s5_pack_kernel(u_ref, perm_ref, o_ref, *, lc, width):
    ngl = LANE // S5_GROUP
    cw = o_ref.shape[-1]
    for gb in range(width // LANE):
        lhs = jnp.concatenate([u_ref[:, i * width + gb * LANE:i * width + (gb + 1) * LANE] for i in range(lc)], axis=1)
        res = _dot(lhs, perm_ref[...]).astype(BF16)
        for gl in range(ngl):
            o_ref[gb * ngl + gl] = res[:, gl * cw:(gl + 1) * cw]


def _s5_unpack_glu_kernel(y_ref, perm_ref, w_ref, b_ref, o_ref, *, lc, width):
    ngl = LANE // S5_GROUP
    pieces = []
    for gb in range(width // LANE):
        lhs = jnp.concatenate([y_ref[gb * ngl + gl] for gl in range(ngl)], axis=1)
        pieces.append(_dot(lhs, perm_ref[...]))
    for i in range(lc):
        y = jnp.concatenate([p[:, i * LANE:(i + 1) * LANE] for p in pieces], axis=1)
        y = 0.5 * y * (1.0 + jnp.tanh(math.sqrt(2.0 / math.pi) * (y + 0.044715 * (y * y * y))))
        z = _dot(y.astype(BF16), w_ref[...]) + b_ref[...]
        o_ref[:, i * width:(i + 1) * width] = (y * jax.nn.sigmoid(z)).astype(BF16)


def _s5(p, ucol, width, ctx_len, ops, d_skip, w_glu, b_glu):
    tmat, g_re, g_im, hmat, a_re, a_im = ops
    b, ta, _ = p.shape
    lc, pp, ns = S5_CHUNK, S5_GROUP, S5_STATE
    g = width // pp
    ngl = LANE // pp
    nchunk = ta // lc
    rpc = 8
    cw = lc * pp
    rows = nchunk * rpc
    perm = _group_perm(lc, pp, ngl)
    nperm = perm.shape[0]
    u2 = p[..., ucol:ucol + width].astype(BF16).reshape(b, nchunk, lc * width)
    ug = pl.pallas_call(
        functools.partial(_s5_pack_kernel, lc=lc, width=width),
        out_shape=jax.ShapeDtypeStruct((g, b, nchunk, cw), BF16),
        grid=(b,),
        in_specs=[pl.BlockSpec((None, nchunk, lc * width), lambda i: (i, 0, 0)),
                  pl.BlockSpec((nperm, nperm), lambda i: (0, 0))],
        out_specs=pl.BlockSpec((g, None, nchunk, cw), lambda i: (0, i, 0, 0)),
        compiler_params=_cparams("parallel"),
        name="s5_pack",
    )(u2, perm)
    ug = jnp.pad(jnp.swapaxes(ug, 1, 2), ((0, 0), (0, 0), (0, rpc - b), (0, 0))).reshape(g, rows, cw)
    dvec = jnp.tile(d_skip.astype(F32).reshape(g, 1, pp), (1, lc, 1)).reshape(g, 1, cw)
    op = lambda r, c: pl.BlockSpec((None, 2, r, c), lambda i: (i, 0, 0, 0))
    y = pl.pallas_call(
        functools.partial(_s5_kernel, nchunk=nchunk, nctx=ctx_len // lc, rpc=rpc),
        out_shape=jax.ShapeDtypeStruct((g, rows, cw), BF16),
        grid=(g,),
        in_specs=[pl.BlockSpec((None, rows, cw), lambda i: (i, 0, 0)),
                  op(cw, cw), op(cw, ns), op(cw, ns), op(2 * ns, cw), op(1, ns), op(1, ns),
                  pl.BlockSpec((None, 1, cw), lambda i: (i, 0, 0))],
        out_specs=pl.BlockSpec((None, rows, cw), lambda i: (i, 0, 0)),
        scratch_shapes=[pltpu.VMEM((2, rows, ns), F32) for _ in range(4)],
        compiler_params=_cparams("parallel"),
        name="s5_scan",
    )(ug, tmat, g_re, g_im, hmat, a_re, a_im, dvec)
    y = jnp.swapaxes(y.reshape(g, nchunk, rpc, cw)[:, :, :b], 1, 2)
    out = pl.pallas_call(
        functools.partial(_s5_unpack_glu_kernel, lc=lc, width=width),
        out_shape=jax.ShapeDtypeStruct((b, nchunk, lc * width), BF16),
        grid=(b,),
        in_specs=[pl.BlockSpec((g, None, nchunk, cw), lambda i: (0, i, 0, 0)),
                  pl.BlockSpec((nperm, nperm), lambda i: (0, 0)),
                  pl.BlockSpec((width, width), lambda i: (0, 0)),
                  pl.BlockSpec((1, width), lambda i: (0, 0))],
        out_specs=pl.BlockSpec((None, nchunk, lc * width), lambda i: (i, 0, 0)),
        compiler_params=_cparams("parallel"),
        name="s5_unpack_glu",
    )(y, perm.T, w_glu.astype(BF16), b_glu.reshape(1, width))
    return out.reshape(b, ta, width)


def _postmix_kernel(a_ref, b_ref, w_ref, *refs, nx):
    m_ref, gpost_ref, gpre_ref, wr_ref, xo_ref, h_ref, lg_ref = refs[nx:]
    wa = a_ref.shape[-1]
    y = _dot(a_ref[...], w_ref[0:wa, :]) + _dot(b_ref[...], w_ref[wa:, :])
    m = m_ref[...]
    xn = _stream_tile(refs[:nx]) + m[2:3] * (_rms(y) * gpost_ref[...])
    xo_ref[...] = xn
    h2 = _rms(xn) * gpre_ref[...] * (1.0 + m[4:5]) + m[3:4]
    hi = h2.astype(BF16)
    h_ref[...] = hi
    lo = (h2 - hi.astype(F32)).astype(BF16)
    ne = lg_ref.shape[-1]
    both = _dot(hi, wr_ref[...])
    lg_ref[...] = both[:, 0:ne] + both[:, ne:2 * ne] + _dot(lo, wr_ref[:, 0:ne])


def _postmix(mix_a, mix_b, w_out, xs, modtab, g_post, g_pre, w_router):
    (b, ta, d), x_specs, x_args = _stream_operands(xs)
    wa, wb = mix_a.shape[-1], mix_b.shape[-1]
    ne = w_router.shape[-1]
    tok = lambda w: pl.BlockSpec((None, TOK_TILE, w), lambda i, j: (i, j, 0))
    vec = pl.BlockSpec((1, d), lambda i, j: (0, 0))
    wr_hi = w_router.astype(BF16)
    wr2 = jnp.concatenate([wr_hi, (w_router - wr_hi.astype(F32)).astype(BF16)], axis=1)
    return pl.pallas_call(
        functools.partial(_postmix_kernel, nx=len(x_args)),
        out_shape=(jax.ShapeDtypeStruct((b, ta, d), F32),
                   jax.ShapeDtypeStruct((b, ta, d), BF16),
                   jax.ShapeDtypeStruct((b, ta, ne), F32)),
        grid=(b, ta // TOK_TILE),
        in_specs=[tok(wa), tok(wb),
                  pl.BlockSpec((wa + wb, d), lambda i, j: (0, 0))] + x_specs + [
                  _mod_spec(d), vec, vec,
                  pl.BlockSpec((d, 2 * ne), lambda i, j: (0, 0))],
        out_specs=(tok(d), tok(d), tok(ne)),
        compiler_params=_cparams("parallel", "parallel"),
        name="mix_out",
    )(mix_a, mix_b, w_out.astype(BF16), *x_args, modtab, g_post.reshape(1, d), g_pre.reshape(1, d), wr2)


def _select_kernel(lg_ref, pos_ref, gate_ref, *, ctx_len, cap_ctx, cap_lat):
    lg = lg_ref[...]
    ne, ta = lg.shape
    ex = jnp.exp(lg - jnp.max(lg, axis=0, keepdims=True))
    probs = ex / jnp.sum(ex, axis=0, keepdims=True)
    bits = pltpu.bitcast(probs, jnp.int32)
    is_ctx = lax.broadcasted_iota(jnp.int32, (ne, ta), 1) < ctx_len

    def counts(mask):
        mf = jnp.where(mask, 1.0, 0.0)
        return (jnp.sum(jnp.where(is_ctx, mf, 0.0), axis=1, keepdims=True),
                jnp.sum(jnp.where(is_ctx, 0.0, mf), axis=1, keepdims=True))

    def search(i, carry):
        pc, pt = carry
        bit = jnp.left_shift(jnp.int32(1), 30 - i)
        cc, ct = counts(bits >= jnp.where(is_ctx, pc | bit, pt | bit))
        return jnp.where(cc >= cap_ctx, pc | bit, pc), jnp.where(ct >= cap_lat, pt | bit, pt)

    z = jnp.zeros((ne, 1), jnp.int32)
    pc, pt = lax.fori_loop(0, 31, search, (z, z))
    thr = jnp.where(is_ctx, pc, pt)
    gt = bits > thr
    eq = bits == thr
    gc, gl = counts(gt)
    need = jnp.where(is_ctx, cap_ctx - gc, cap_lat - gl)

    nb = ta // LANE
    ut = jnp.where(lax.broadcasted_iota(jnp.int32, (LANE, LANE), 0) <= lax.broadcasted_iota(jnp.int32, (LANE, LANE), 1),
                   1.0, 0.0).astype(BF16)

    def lane_prefix(mask):
        mf = jnp.where(mask, 1.0, 0.0).astype(BF16)
        blocks = jnp.concatenate([mf[:, j * LANE:(j + 1) * LANE] for j in range(nb)], axis=0)
        inc = _dot(blocks, ut)
        outs = []
        off = jnp.zeros((ne, 1), F32)
        for j in range(nb):
            if j * LANE == ctx_len:
                off = jnp.zeros((ne, 1), F32)
            blk = inc[j * ne:(j + 1) * ne]
            outs.append(blk + off)
            off = off + blk[:, LANE - 1:LANE]
        return jnp.concatenate(outs, axis=1)

    sel = gt | (eq & (lane_prefix(eq) <= need))
    slot = lane_prefix(sel) - 1.0 + jnp.where(is_ctx, 0.0, float(cap_ctx))
    pos_ref[...] = jnp.where(sel, slot, -1.0)
    gate_ref[...] = jnp.where(sel, probs, 0.0)


def _moe_select(logits_t, ctx_len, cap_ctx, cap_lat):
    b, ne, ta = logits_t.shape
    spec = pl.BlockSpec((None, ne, ta), lambda i: (i, 0, 0))
    return pl.pallas_call(
        functools.partial(_select_kernel, ctx_len=ctx_len, cap_ctx=cap_ctx, cap_lat=cap_lat),
        out_shape=(jax.ShapeDtypeStruct((b, ne, ta), F32), jax.ShapeDtypeStruct((b, ne, ta), F32)),
        grid=(b,),
        in_specs=[spec],
        out_specs=(spec, spec),
        compiler_params=_cparams("parallel"),
        name="moe_select",
    )(logits_t)


def _slot_windows(pos_t, nt, win, nslot):
    b, ne, ta = pos_t.shape
    pt = pos_t.reshape(b, ne, nt, ta // nt)
    hi = jnp.max(pt, axis=-1).astype(jnp.int32) + 1
    lo = jnp.min(jnp.where(pt >= 0, pt, float(nslot)), axis=-1).astype(jnp.int32)
    lo = jnp.where(hi > 0, lo // 16 * 16, 0)
    npass = jnp.maximum(jnp.max((hi - lo + win - 1) // win, axis=1), 1)
    return jnp.transpose(lo, (0, 2, 1)).reshape(-1), npass.reshape(-1)


def _gather_kernel(start_ref, npass_ref, pos_ref, h_ref, o_ref):
    i = pl.program_id(0)
    eg = pl.program_id(1)
    j = pl.program_id(2)
    ge, tk = pos_ref.shape
    ne = pl.num_programs(1) * ge
    nslot = o_ref.shape[1]
    win = GATHER_WIN
    tile = i * pl.num_programs(2) + j

    @pl.when(j == 0)
    def _():
        o_ref[...] = jnp.zeros_like(o_ref)

    pos = pos_ref[...].astype(jnp.int32)
    slot = lax.broadcasted_iota(jnp.int32, (win, tk), 0)

    def one_pass(p, carry):
        src, blocks = [], []
        for el in range(ge):
            lo = start_ref[tile * ne + eg * ge + el] + p * win
            src.append(pl.multiple_of(jnp.minimum(lo, nslot - win), 16))
            pe = pos[el:el + 1, :]
            hit = (pe >= lo) & (pe < lo + win) & (slot == pe - src[el])
            blocks.append(jnp.where(hit, 1.0, 0.0).astype(BF16))
        part = _dot(jnp.concatenate(blocks, axis=0), h_ref[...]).astype(BF16)
        for el in range(ge):
            o_ref[el, pl.ds(src[el], win), :] += part[el * win:(el + 1) * win]
        return carry

    lax.fori_loop(0, npass_ref[tile], one_pass, 0)


def _moe_gather(pos_t, h, nslot):
    b, ne, ta = pos_t.shape
    d = h.shape[-1]
    ge = 8
    tk = TOK_TILE
    nt = ta // tk
    start, npass = _slot_windows(pos_t, nt, GATHER_WIN, nslot)
    return pl.pallas_call(
        _gather_kernel,
        out_shape=jax.ShapeDtypeStruct((ne, b, nslot, d), BF16),
        grid_spec=pltpu.PrefetchScalarGridSpec(
            num_scalar_prefetch=2,
            grid=(b, ne // ge, nt),
            in_specs=[pl.BlockSpec((None, None, ge, tk), lambda i, g, j, *_: (i, g, 0, j)),
                      pl.BlockSpec((None, tk, d), lambda i, g, j, *_: (i, j, 0))],
            out_specs=pl.BlockSpec((ge, None, nslot, d), lambda i, g, j, *_: (g, i, 0, 0))),
        compiler_params=_cparams("arbitrary", "arbitrary", "arbitrary"),
        name="moe_gather",
    )(start, npass, pos_t.reshape(b, ne // ge, ge, ta), h)


def _ffn_kernel(x_ref, wg_ref, wu_ref, wd_ref, o_ref, hid_ref, *, nf):
    s = pl.program_id(2)
    tf = wg_ref.shape[-1]

    @pl.when(s < nf)
    def _():
        x = x_ref[...]
        g = _dot(x, wg_ref[...].astype(BF16))
        u = _dot(x, wu_ref[...].astype(BF16))
        hid_ref[s] = (_silu(g) * u).astype(BF16)

    @pl.when(s >= nf)
    def _():
        acc = _dot(hid_ref[0], wd_ref[0:tf, :].astype(BF16))
        for f in range(1, nf):
            acc = acc + _dot(hid_ref[f], wd_ref[f * tf:(f + 1) * tf, :].astype(BF16))
        o_ref[...] = acc.astype(BF16)


def _moe_ffn(xin, w_gate, w_up, w_down, layer, tf=512):
    ne, ns, m, d = xin.shape
    ff = w_gate.shape[-1]
    nf = ff // tf
    nd = d // tf
    up = lambda e, s, f: (layer, e, 0, jnp.minimum(f, nf - 1))
    return pl.pallas_call(
        functools.partial(_ffn_kernel, nf=nf),
        out_shape=jax.ShapeDtypeStruct((ne, ns, m, d), BF16),
        grid=(ne, ns, nf + nd),
        in_specs=[pl.BlockSpec((None, None, m, d), lambda e, s, f: (e, s, 0, 0)),
                  pl.BlockSpec((None, None, d, tf), up),
                  pl.BlockSpec((None, None, d, tf), up),
                  pl.BlockSpec((None, None, ff, tf), lambda e, s, f: (layer, e, 0, jnp.maximum(f - nf, 0)))],
        out_specs=pl.BlockSpec((None, None, m, tf), lambda e, s, f: (e, s, 0, jnp.maximum(f - nf, 0))),
        scratch_shapes=[pltpu.VMEM((nf, m, tf), BF16)],
        compiler_params=_cparams("parallel", "parallel", "arbitrary"),
        name="moe_ffn",
    )(xin, w_gate, w_up, w_down)


def _combine_kernel(start_ref, npass_ref, pos_ref, gate_ref, y_hbm, x_ref, m_ref, g_ref, *refs,
                    ctx_tiles, skip, with_next):
    if with_next:
        gn_ref, mn_ref, o_ref, hn_ref, ybuf, acc_ref, sem = refs
    else:
        o_ref, ybuf, acc_ref, sem = refs
    tt, ne = pos_ref.shape
    nslot = y_hbm.shape[2]
    win = COMBINE_WIN
    nj = pl.num_programs(1)
    step = pl.program_id(0) * nj + pl.program_id(1)
    buf = step % 2
    pos = pos_ref[...].astype(jnp.int32)
    gate = gate_ref[...]
    lane = lax.broadcasted_iota(jnp.int32, (tt, win), 1)

    def windows(s, p):
        smp = s // nj
        tile = smp * (nj + skip) + s % nj + skip
        lo = [start_ref[tile * ne + e] + p * win for e in range(ne)]
        return smp, lo, [pl.multiple_of(jnp.minimum(v, nslot - win), 16) for v in lo]

    def window_copy(smp, e, src, slot):
        return pltpu.make_async_copy(y_hbm.at[e, smp, pl.ds(src, win), :], ybuf.at[slot, pl.ds(e * win, win), :],
                                     sem.at[slot, e])

    def fetch(s, p, slot):
        smp, _, src = windows(s, p)
        for e in range(ne):
            window_copy(smp, e, src[e], slot).start()

    def one_pass(p):
        smp, lo, src = windows(step, p)
        blocks = []
        for e in range(ne):
            pe = pos[:, e:e + 1]
            hit = (pe >= lo[e]) & (pe < lo[e] + win) & (lane == pe - src[e])
            blocks.append(jnp.where(hit, gate[:, e:e + 1], 0.0).astype(BF16))
        w = jnp.concatenate(blocks, axis=1)
        for e in range(ne):
            window_copy(smp, e, src[e], buf).wait()
        return _dot(w, ybuf[buf])

    @pl.when(step == 0)
    def _():
        fetch(step, 0, buf)

    @pl.when(step + 1 < pl.num_programs(0) * nj)
    def _():
        fetch(step + 1, 0, 1 - buf)

    acc_ref[...] = one_pass(0)

    def extra(p, carry):
        fetch(step, p, buf)
        acc_ref[...] += one_pass(p)
        return carry

    lax.fori_loop(1, npass_ref[windows(step, 0)[0] * (nj + skip) + step % nj + skip], extra, 0)
    m = m_ref[...]
    is_ctx = pl.program_id(1) + skip < ctx_tiles
    pick = lambda k: jnp.where(is_ctx, m[0, k:k + 1], m[1, k:k + 1])
    xn = x_ref[...] + pick(5) * (_rms(acc_ref[...]) * g_ref[...])
    o_ref[...] = xn
    if with_next:
        mn = mn_ref[...]
        pick_n = lambda k: jnp.where(is_ctx, mn[0, k:k + 1], mn[1, k:k + 1])
        hn_ref[...] = (_rms(xn) * gn_ref[...] * (1.0 + pick_n(1)) + pick_n(0)).astype(BF16)


def _moe_combine(pos_t, gate_t, yout, xa, modtab, g_post, ctx_len, nxt=None, latent_only=False):
    b, ta, d = xa.shape
    ne, _, nslot, _ = yout.shape
    tt = TOK_TILE
    nt = ta // tt
    skip = ctx_len // tt if latent_only else 0
    win = COMBINE_WIN
    start, npass = _slot_windows(pos_t, nt, win, nslot)
    tok = lambda w: pl.BlockSpec((None, tt, w), lambda i, j, *_: (i, j + skip, 0))
    out_tok = lambda: pl.BlockSpec((None, tt, d), lambda i, j, *_: (i, j, 0))
    mods = pl.BlockSpec((None, 2, 6, d), lambda i, j, *_: (i, 0, 0, 0))
    vec = pl.BlockSpec((1, d), lambda i, j, *_: (0, 0))
    xo = jax.ShapeDtypeStruct((b, ta - skip * tt, d), F32)
    extra_in, extra_args = ([vec, mods], [nxt[0].reshape(1, d), nxt[1]]) if nxt else ([], [])
    return pl.pallas_call(
        functools.partial(_combine_kernel, ctx_tiles=ctx_len // tt, skip=skip, with_next=bool(nxt)),
        out_shape=(xo, jax.ShapeDtypeStruct((b, ta - skip * tt, d), BF16)) if nxt else xo,
        grid_spec=pltpu.PrefetchScalarGridSpec(
            num_scalar_prefetch=2,
            grid=(b, nt - skip),
            in_specs=[tok(ne), tok(ne),
                      pl.BlockSpec(memory_space=pl.ANY),
                      tok(d), mods, vec] + extra_in,
            out_specs=(out_tok(), out_tok()) if nxt else out_tok(),
            scratch_shapes=[pltpu.VMEM((2, ne * win, d), BF16),
                            pltpu.VMEM((tt, d), F32),
                            pltpu.SemaphoreType.DMA((2, ne))]),
        compiler_params=_cparams("arbitrary", "arbitrary"),
        name="moe_combine",
    )(start, npass, jnp.swapaxes(pos_t, 1, 2), jnp.swapaxes(gate_t, 1, 2), yout, xa, modtab,
      g_post.reshape(1, d), *extra_args)


def _ec_moe(logits_t, h, xa, modtab, g_post, w_gate, w_up, w_down, layer, ctx_len, nxt=None, latent_only=False):
    b, ne, ta = logits_t.shape
    d = h.shape[-1]
    cap_ctx = max(1, EC_CAPACITY * ctx_len // ne)
    cap_lat = max(1, EC_CAPACITY * (ta - ctx_len) // ne)
    nslot = cap_ctx + cap_lat
    pos_t, gate_t = _moe_select(logits_t, ctx_len, cap_ctx, cap_lat)
    xin = _moe_gather(pos_t, h, nslot)
    pair = 2 if b % 2 == 0 else 1
    yout = _moe_ffn(xin.reshape(ne, b // pair, pair * nslot, d), w_gate, w_up, w_down, layer)
    yout = yout.reshape(ne, b, nslot, d)
    return _moe_combine(pos_t, gate_t, yout, xa, modtab, g_post, ctx_len, nxt, latent_only)


def kernel(x, c, ctx, c_ctx, w_mod, b_mod, g_mix_pre, g_mix_post, g_ffn_pre, g_ffn_post, w_router, w_exp_gate, w_exp_up, w_exp_down, ev_w_in, ev_w_out, hgrn_lb, hgrn_g_norm, na_rpb, od_w_in, od_w_out, q_norm, k_norm, s5_a_re, s5_a_im, s5_log_dt, s5_b_re, s5_b_im, s5_c_re, s5_c_im, s5_d, s5_w_glu, s5_b_glu):
    b, t, d = x.shape
    ctx_len = ctx.shape[1]
    depth = w_mod.shape[0]
    assert depth == 2 and b <= 7
    ta = ctx_len + t
    a_width = d // 2
    s5_width = d // 4
    cq_width = d - s5_width
    ckv_width = cq_width // 3

    cc = jnp.concatenate([c, c_ctx[None], jnp.zeros((7 - b, d), F32)], axis=0)
    mod = _modulation(cc, w_mod, b_mod)
    mod_lat = mod[:, :b].reshape(depth, b, 1, 6, d)
    mod_ctx = jnp.broadcast_to(mod[:, b].reshape(depth, 1, 1, 6, d), (depth, b, 1, 6, d))
    modtab = jnp.concatenate([mod_ctx, mod_lat], axis=2)

    lb_all = jnp.cumsum(jax.nn.softmax(hgrn_lb.astype(F32), axis=0), axis=0)
    hx = _prenorm((ctx, x), g_mix_pre[0], modtab[0])
    p = _matmul(hx.reshape(b * ta, d), ev_w_in[0]).reshape(b, ta, -1)
    o_f, o_b = _hgrn(p, lb_all[0], ctx_len, a_width)
    mix_a = _hgrn_readout(o_f, o_b, p, hgrn_g_norm[0], a_width)
    mix_b = _natten(p, _natten_bias(na_rpb[0]), ctx_len, 5 * a_width, d - a_width)
    xa, h2, logits = _postmix(mix_a, mix_b, ev_w_out[0], (ctx, x), modtab[0], g_mix_post[0], g_ffn_pre[0], w_router[0])
    xa, hx = _ec_moe(jnp.swapaxes(logits, 1, 2), h2, xa, modtab[0], g_ffn_post[0], w_exp_gate, w_exp_up, w_exp_down, 0,
                     ctx_len, nxt=(g_mix_pre[1], modtab[1]))

    p = _matmul(hx.reshape(b * ta, d), od_w_in[0]).reshape(b, ta, -1)
    cosf, sinf = _rope_tables(ctx_len, t)
    mix_a = _gqa_attention(p, cosf, sinf, q_norm[0], k_norm[0], cq_width, ckv_width, ctx_len)
    ops = _s5_operators(s5_a_re[0], s5_a_im[0], s5_log_dt[0], s5_b_re[0], s5_b_im[0], s5_c_re[0], s5_c_im[0])
    mix_b = _s5(p, cq_width + 2 * ckv_width, s5_width, ctx_len, ops, s5_d[0], s5_w_glu[0], s5_b_glu[0])
    xa, h2, logits = _postmix(mix_a, mix_b, od_w_out[0], xa, modtab[1], g_mix_post[1], g_ffn_pre[1], w_router[1])
    return _ec_moe(jnp.swapaxes(logits, 1, 2), h2, xa, modtab[1], g_ffn_post[1], w_exp_gate, w_exp_up, w_exp_down, 1,
                   ctx_len, latent_only=True)
```

```python
import functools
import math

import jax
import jax.numpy as jnp
from jax import lax
from jax.experimental import pallas as pl
from jax.experimental.pallas import tpu as pltpu

F32 = jnp.float32
BF16 = jnp.bfloat16
HIGHEST = lax.Precision.HIGHEST
EPS = 1e-6

LANE = 128
TOK_TILE = 256
VMEM_LIMIT = 52 << 20

GRID_W = 64
NA_ROWS = 8
NA_COLS = 16
NA_GROUP = 4
NA_SPAN = 12
HEAD = 128
A_CHUNK = 64
N_EXPERTS = 16
EC_CAPACITY = 2
S5_GROUP = 16
S5_STATE = 64
S5_CHUNK = 16
COMBINE_WIN = 64
ATTN_TILE = 256
GATHER_WIN = 64
ROPE_THETA = 10000.0
NEG_BIG = -1e30
LOG2E = 1.4426950408889634


def _cparams(*sem):
    return pltpu.CompilerParams(dimension_semantics=sem, vmem_limit_bytes=VMEM_LIMIT)


def _dot(a, b):
    return jnp.dot(a, b, preferred_element_type=F32)


def _dot_nt(a, b):
    return lax.dot_general(a, b, (((1,), (1,)), ((), ())), preferred_element_type=F32)


def _dot_tn(a, b):
    return lax.dot_general(a, b, (((0,), (0,)), ((), ())), preferred_element_type=F32)


def _rms(x):
    return x * lax.rsqrt(jnp.mean(x * x, axis=-1, keepdims=True) + EPS)


def _silu(x):
    return x * jax.nn.sigmoid(x)


def _mod_kernel(c_ref, w_ref, b_ref, o_ref):
    o_ref[...] = jnp.dot(_silu(c_ref[...]), w_ref[...], preferred_element_type=F32, precision=HIGHEST) + b_ref[...]


def _modulation(cc, w_mod, b_mod):
    depth, d, n = w_mod.shape
    tn = 1024
    return pl.pallas_call(
        _mod_kernel,
        out_shape=jax.ShapeDtypeStruct((depth, cc.shape[0], n), F32),
        grid=(depth, n // tn),
        in_specs=[pl.BlockSpec(cc.shape, lambda l, j: (0, 0)),
                  pl.BlockSpec((None, d, tn), lambda l, j: (l, 0, j)),
                  pl.BlockSpec((None, 1, tn), lambda l, j: (l, 0, j))],
        out_specs=pl.BlockSpec((None, cc.shape[0], tn), lambda l, j: (l, 0, j)),
        compiler_params=_cparams("arbitrary", "arbitrary"),
        name="modulation",
    )(cc, w_mod, b_mod.reshape(depth, 1, n))


def _mod_spec(d):
    return pl.BlockSpec((None, None, 6, d), lambda b, j: (b, jnp.minimum(j, 1), 0, 0))


def _stream_operands(xs):
    if not isinstance(xs, tuple):
        b, ta, d = xs.shape
        return (b, ta, d), [pl.BlockSpec((None, TOK_TILE, d), lambda i, j: (i, j, 0))], [xs]
    ctx, x = xs
    b, t, d = x.shape
    assert ctx.shape[1] == TOK_TILE
    return (b, TOK_TILE + t, d), [pl.BlockSpec((None, TOK_TILE, d), lambda i, j: (i, 0, 0)),
                                  pl.BlockSpec((None, TOK_TILE, d), lambda i, j: (i, jnp.maximum(j - 1, 0), 0))], [ctx, x]


def _stream_tile(refs):
    if len(refs) == 1:
        return refs[0][...]
    return jnp.where(pl.program_id(1) == 0, refs[0][...], refs[1][...])


def _prenorm_kernel(*refs, nx):
    g_ref, m_ref, o_ref = refs[nx:]
    m = m_ref[...]
    y = _rms(_stream_tile(refs[:nx])) * g_ref[...]
    o_ref[...] = (y * (1.0 + m[1:2]) + m[0:1]).astype(BF16)


def _prenorm(xs, g, modtab):
    (b, ta, d), x_specs, x_args = _stream_operands(xs)
    return pl.pallas_call(
        functools.partial(_prenorm_kernel, nx=len(x_args)),
        out_shape=jax.ShapeDtypeStruct((b, ta, d), BF16),
        grid=(b, ta // TOK_TILE),
        in_specs=x_specs + [pl.BlockSpec((1, d), lambda i, j: (0, 0)), _mod_spec(d)],
        out_specs=pl.BlockSpec((None, TOK_TILE, d), lambda i, j: (i, j, 0)),
        compiler_params=_cparams("parallel", "parallel"),
        name="prenorm",
    )(*x_args, g.reshape(1, d), modtab)


def _mm_kernel(a_ref, w_ref, o_ref, wb_ref):
    @pl.when(pl.program_id(1) == 0)
    def _():
        wb_ref[...] = w_ref[...].astype(BF16)

    o_ref[...] = _dot(a_ref[...], wb_ref[...]).astype(o_ref.dtype)


def _matmul(a, w, tm=1024, tn=1024, out_dtype=F32):
    m, k = a.shape
    n = w.shape[1]
    return pl.pallas_call(
        _mm_kernel,
        out_shape=jax.ShapeDtypeStruct((m, n), out_dtype),
        grid=(n // tn, m // tm),
        in_specs=[pl.BlockSpec((tm, k), lambda j, i: (i, 0)),
                  pl.BlockSpec((k, tn), lambda j, i: (0, j))],
        out_specs=pl.BlockSpec((tm, tn), lambda j, i: (i, j)),
        scratch_shapes=[pltpu.VMEM((k, tn), BF16)],
        compiler_params=_cparams("arbitrary", "arbitrary"),
        name="proj_in",
    )(a, w)


def _hgrn_masks(chunk):
    row = lax.broadcasted_iota(jnp.int32, (chunk, LANE), 0)
    ti = lax.broadcasted_iota(jnp.int32, (chunk, chunk), 0)
    si = lax.broadcasted_iota(jnp.int32, (chunk, chunk), 1)
    levels = [(((row >> lvl) & 1) == 1, (ti >> (lvl + 1)) == (si >> (lvl + 1))) for lvl in range(chunk.bit_length() - 1)]
    return ti == si, levels


def _hgrn_chunk(q, fr, v, lb, st, rev, masks):
    f = lb + (1.0 - lb) * jax.nn.sigmoid(fr)
    kk = 1.0 - f
    lf = jnp.log(f)
    chunk = q.shape[0]
    diag, levels = masks
    att = jnp.where(diag, _dot_nt(q.astype(BF16), kk.astype(BF16)), 0.0)
    p_in = lf
    r_ex = jnp.zeros_like(lf)
    tot = lf
    for lvl, (bit, same) in enumerate(levels):
        step = 1 << lvl
        up = pltpu.roll(tot, step, 0)
        dn = pltpu.roll(tot, chunk - step, 0)
        is_q = jnp.logical_not(bit) if rev else bit
        qf = jnp.where(is_q, jnp.exp(p_in) * q, 0.0).astype(BF16)
        kf = jnp.where(is_q, 0.0, jnp.exp(r_ex) * kk).astype(BF16)
        att = att + jnp.where(same, _dot_nt(qf, kf), 0.0)
        if rev:
            p_in = p_in + jnp.where(bit, 0.0, dn)
            r_ex = r_ex + jnp.where(bit, up, 0.0)
        else:
            p_in = p_in + jnp.where(bit, up, 0.0)
            r_ex = r_ex + jnp.where(bit, 0.0, dn)
        tot = tot + jnp.where(bit, up, dn)
    vb = v.astype(BF16)
    o = _dot_nt((q * jnp.exp(p_in)).astype(BF16), st.astype(BF16)) + _dot(att.astype(BF16), vb)
    kd = (kk * jnp.exp(r_ex)).astype(BF16)
    st_new = st * jnp.exp(tot[0:1]) + _dot_tn(vb, kd)
    return o, st_new


def _hgrn_kernel(qf_ref, ff_ref, vf_ref, qb_ref, fb_ref, vb_ref, lb_ref, of_ref, ob_ref, st_ref, *, chunk, hb):
    @pl.when(pl.program_id(2) == 0)
    def _():
        st_ref[...] = jnp.zeros_like(st_ref)

    masks = _hgrn_masks(chunk)
    for h in range(hb):
        sl = slice(h * HEAD, (h + 1) * HEAD)
        o, st = _hgrn_chunk(qf_ref[:, sl], ff_ref[:, sl], vf_ref[:, sl], lb_ref[0:1, sl], st_ref[0, h], False, masks)
        of_ref[:, sl] = o
        st_ref[0, h] = st
        o, st = _hgrn_chunk(qb_ref[:, sl], fb_ref[:, sl], vb_ref[:, sl], lb_ref[1:2, sl], st_ref[1, h], True, masks)
        ob_ref[:, sl] = o
        st_ref[1, h] = st


def _hgrn(p, lb, ctx_len, width, hb=8):
    b, ta, _ = p.shape
    chunk = A_CHUNK
    n = ta // chunk
    nc = ctx_len // chunk
    cw = HEAD * hb
    nh = width // cw
    bwd = lambda j: jnp.where(j < nc, nc - 1 - j, n - 1 + nc - j)

    def spec(base, rev):
        return pl.BlockSpec((None, chunk, cw), lambda i, h, j: (i, bwd(j) if rev else j, base // cw + h))

    out = jax.ShapeDtypeStruct((b, ta, width), F32)
    return pl.pallas_call(
        functools.partial(_hgrn_kernel, chunk=chunk, hb=hb),
        out_shape=(out, out),
        grid=(b, nh, n),
        in_specs=[spec(0, False), spec(width, False), spec(3 * width, False),
                  spec(0, True), spec(2 * width, True), spec(3 * width, True),
                  pl.BlockSpec((2, cw), lambda i, h, j: (0, h))],
        out_specs=(spec(0, False), spec(0, True)),
        scratch_shapes=[pltpu.VMEM((2, hb, HEAD, HEAD), F32)],
        compiler_params=_cparams("parallel", "parallel", "arbitrary"),
        name="hgrn_scan",
    )(p, p, p, p, p, p, lb)


def _hgrn_readout_kernel(of_ref, ob_ref, g_ref, gn_ref, o_ref):
    for h in range(o_ref.shape[-1] // HEAD):
        sl = slice(h * HEAD, (h + 1) * HEAD)
        o = of_ref[:, sl] + ob_ref[:, sl]
        o_ref[:, sl] = (_rms(o) * gn_ref[...] * _silu(g_ref[:, sl])).astype(BF16)


def _hgrn_readout(o_f, o_b, p, g_norm, width):
    b, ta, _ = o_f.shape
    spec = pl.BlockSpec((None, TOK_TILE, width), lambda i, j: (i, j, 0))
    return pl.pallas_call(
        _hgrn_readout_kernel,
        out_shape=jax.ShapeDtypeStruct((b, ta, width), BF16),
        grid=(b, ta // TOK_TILE),
        in_specs=[spec, spec,
                  pl.BlockSpec((None, TOK_TILE, width), lambda i, j: (i, j, 4)),
                  pl.BlockSpec((1, HEAD), lambda i, j: (0, 0))],
        out_specs=spec,
        compiler_params=_cparams("parallel", "parallel"),
        name="hgrn_readout",
    )(o_f, o_b, p, g_norm.reshape(1, HEAD))


def _natten_bias(rpb):
    nrow, ncol = 2 * NA_ROWS - 1, 2 * NA_COLS - 1
    col = jnp.arange(GRID_W)
    col_start = jnp.clip(col - NA_COLS // 2, 0, GRID_W - NA_COLS)
    cmask = (col[None, :] >= col_start[:, None]) & (col[None, :] < col_start[:, None] + NA_COLS)
    dc = col[None, :] - col[:, None] + NA_COLS - 1
    cm = (cmask[:, :, None] & (dc[:, :, None] == jnp.arange(ncol))).astype(F32)
    cls = jnp.arange(3)[:, None, None]
    i = jnp.arange(NA_GROUP)[None, :, None]
    w = jnp.arange(NA_SPAN)[None, None, :]
    first_w = jnp.where(cls == 0, 0, jnp.where(cls == 1, i, NA_SPAN - NA_ROWS))
    valid = (w >= first_w) & (w < first_w + NA_ROWS)
    span0 = jnp.where(cls == 0, 0, jnp.where(cls == 1, -(NA_ROWS // 2), NA_GROUP - NA_SPAN))
    dr = span0 + w - i + NA_ROWS - 1
    rm = (valid[..., None] & (dr[..., None] == jnp.arange(nrow))).astype(F32)
    t = jnp.einsum('qkb,hab->haqk', cm, rpb.astype(F32), precision=HIGHEST)
    t = jnp.einsum('ciwa,haqk->hciqwk', rm, t, precision=HIGHEST)
    ok = valid[None, :, :, None, :, None] & cmask[None, None, None, :, None, :]
    t = jnp.where(ok, t, NEG_BIG)
    return t.reshape(rpb.shape[0], 3, NA_GROUP * GRID_W, NA_SPAN * GRID_W)


def _natten_kernel(q_ref, k_ref, v_ref, bias_ref, o_ref, *, ctx_len, rows):
    j = pl.program_id(2)
    nst = rows // NA_GROUP
    scale = HEAD ** -0.5
    heads = [slice(h * HEAD, (h + 1) * HEAD) for h in range(q_ref.shape[-1] // HEAD)]

    def ctx_scores(sl):
        qb = q_ref[:, sl].astype(BF16)
        vc = v_ref[0:ctx_len, sl].astype(BF16)
        s_ctx = _dot_nt(qb, k_ref[0:ctx_len, sl].astype(BF16)) * scale
        return qb, vc, s_ctx, jnp.max(s_ctx, axis=1, keepdims=True)

    @pl.when(j < nst)
    def _():
        last = j == nst - 1
        cls = jnp.where(j == 0, 0, jnp.where(last, 2, 1))
        span_row = jnp.where(j == 0, 0, jnp.where(last, rows - NA_SPAN, NA_GROUP * j - NA_ROWS // 2))
        start = pl.multiple_of(ctx_len + span_row * GRID_W, GRID_W)
        for h, sl in enumerate(heads):
            qb, vc, s_ctx, m_ctx = ctx_scores(sl)
            kw = k_ref[pl.ds(start, NA_SPAN * GRID_W), sl].astype(BF16)
            vw = v_ref[pl.ds(start, NA_SPAN * GRID_W), sl].astype(BF16)
            s_win = _dot_nt(qb, kw) * scale + bias_ref[h, cls]
            m = jnp.maximum(m_ctx, jnp.max(s_win, axis=1, keepdims=True))
            pw = jnp.exp(s_win - m)
            pc = jnp.exp(s_ctx - m)
            den = jnp.sum(pw, axis=1, keepdims=True) + jnp.sum(pc, axis=1, keepdims=True)
            o = _dot(pw.astype(BF16), vw) + _dot(pc.astype(BF16), vc)
            o_ref[:, sl] = (o / den).astype(BF16)

    @pl.when(j >= nst)
    def _():
        for sl in heads:
            _, vc, s_ctx, m_ctx = ctx_scores(sl)
            pc = jnp.exp(s_ctx - m_ctx)
            o = _dot(pc.astype(BF16), vc) / jnp.sum(pc, axis=1, keepdims=True)
            o_ref[:, sl] = o.astype(BF16)


def _natten(p, bias, ctx_len, col0, width, hb=2):
    b, ta, _ = p.shape
    cw = HEAD * hb
    nh = width // cw
    rows = (ta - ctx_len) // GRID_W
    tq = NA_GROUP * GRID_W
    nst = rows // NA_GROUP
    ncq = ctx_len // tq
    assert rows % NA_GROUP == 0 and rows >= NA_SPAN and ctx_len % tq == 0

    def qmap(cb):
        return lambda i, h, j: (i, jnp.where(j < nst, ncq + j, j - nst), cb + h)

    return pl.pallas_call(
        functools.partial(_natten_kernel, ctx_len=ctx_len, rows=rows),
        out_shape=jax.ShapeDtypeStruct((b, ta, width), BF16),
        grid=(b, nh, nst + ncq),
        in_specs=[pl.BlockSpec((None, tq, cw), qmap(col0 // cw)),
                  pl.BlockSpec((None, ta, cw), lambda i, h, j: (i, 0, (col0 + width) // cw + h)),
                  pl.BlockSpec((None, ta, cw), lambda i, h, j: (i, 0, (col0 + 2 * width) // cw + h)),
                  pl.BlockSpec((hb, 3, tq, NA_SPAN * GRID_W), lambda i, h, j: (h, 0, 0, 0))],
        out_specs=pl.BlockSpec((None, tq, cw), qmap(0)),
        compiler_params=_cparams("parallel", "parallel", "arbitrary"),
        name="natten",
    )(p, p, p, bias)


def _rope_tables(ctx_len, t):
    pos = jnp.arange(t)
    row = (pos // GRID_W).astype(F32)
    col = (pos % GRID_W).astype(F32)
    half = HEAD // 2
    inv = ROPE_THETA ** (-jnp.arange(0, half, 2, dtype=F32) / half)
    ang = jnp.concatenate([row[:, None] * inv, col[:, None] * inv], axis=-1)
    cos, sin = jnp.cos(ang), jnp.sin(ang)
    cosf = jnp.repeat(cos, 2, axis=-1)
    sinf = jnp.stack([-sin, sin], axis=-1).reshape(t, HEAD)
    cosf = jnp.concatenate([jnp.ones((ctx_len, HEAD), F32), cosf], axis=0)
    sinf = jnp.concatenate([jnp.zeros((ctx_len, HEAD), F32), sinf], axis=0)
    return cosf, sinf


def _rope(x, cosf, sinf):
    even = (lax.broadcasted_iota(jnp.int32, x.shape, 1) & 1) == 0
    partner = jnp.where(even, pltpu.roll(x, LANE - 1, 1), pltpu.roll(x, 1, 1))
    return x * cosf + partner * sinf


def _attn_kernel(q_ref, k_ref, v_ref, cq_ref, sq_ref, ck_ref, sk_ref, qn_ref, kn_ref, o_ref, kb_ref, vb_ref,
                 *, skip_tiles):
    j = pl.program_id(2)

    @pl.when(j == 0)
    def _():
        kb_ref[...] = _rope(_rms(k_ref[...]) * kn_ref[...], ck_ref[...], sk_ref[...]).astype(BF16)
        vb_ref[...] = v_ref[...].astype(BF16)

    @pl.when(j < skip_tiles)
    def _():
        o_ref[...] = jnp.zeros_like(o_ref)

    @pl.when(j >= skip_tiles)
    def _():
        k = kb_ref[...]
        v = vb_ref[...]
        cq = cq_ref[...]
        sq = sq_ref[...]
        for g in range(q_ref.shape[-1] // HEAD):
            sl = slice(g * HEAD, (g + 1) * HEAD)
            q = (_rope(_rms(q_ref[:, sl]) * qn_ref[...], cq, sq) * (HEAD ** -0.5 * LOG2E)).astype(BF16)
            s = _dot_nt(q, k)
            p = jnp.exp2(s - jnp.max(s, axis=1, keepdims=True))
            o = _dot(p.astype(BF16), v) / jnp.sum(p, axis=1, keepdims=True)
            o_ref[:, sl] = o.astype(BF16)


def _gqa_attention(p, cosf, sinf, q_norm, k_norm, qw, kw, skip_rows):
    b, ta, _ = p.shape
    nkv = kw // HEAD
    tq = ATTN_TILE
    skip_tiles = skip_rows // tq
    gw = qw // nkv
    tab_q = pl.BlockSpec((tq, HEAD), lambda i, h, j: (j, 0))
    tab_k = pl.BlockSpec((ta, HEAD), lambda i, h, j: (0, 0))
    vec = pl.BlockSpec((1, HEAD), lambda i, h, j: (0, 0))
    return pl.pallas_call(
        functools.partial(_attn_kernel, skip_tiles=skip_tiles),
        out_shape=jax.ShapeDtypeStruct((b, ta, qw), BF16),
        grid=(b, nkv, ta // tq),
        in_specs=[pl.BlockSpec((None, tq, gw), lambda i, h, j: (i, j, h)),
                  pl.BlockSpec((None, ta, HEAD), lambda i, h, j: (i, 0, qw // HEAD + h)),
                  pl.BlockSpec((None, ta, HEAD), lambda i, h, j: (i, 0, (qw + kw) // HEAD + h)),
                  tab_q, tab_q, tab_k, tab_k, vec, vec],
        out_specs=pl.BlockSpec((None, tq, gw), lambda i, h, j: (i, j, h)),
        scratch_shapes=[pltpu.VMEM((ta, HEAD), BF16), pltpu.VMEM((ta, HEAD), BF16)],
        compiler_params=_cparams("parallel", "parallel", "arbitrary"),
        name="gqa_attention",
    )(p, p, p, cosf, sinf, cosf, sinf, q_norm.reshape(1, HEAD), k_norm.reshape(1, HEAD))


def _s5_operators(a_re, a_im, log_dt, b_re, b_im, c_re, c_im):
    lc, pp, ns = S5_CHUNK, S5_GROUP, S5_STATE
    a_c = lax.complex(a_re.astype(F32), a_im.astype(F32))
    adt = a_c * jnp.exp(log_dt.astype(F32))[..., None]
    a_bar = jnp.exp(adt)
    b_bar = ((a_bar - 1.0) / a_c)[..., None] * lax.complex(b_re.astype(F32), b_im.astype(F32))
    c_mat = lax.complex(c_re.astype(F32), c_im.astype(F32))
    nd, g = a_re.shape[0], a_re.shape[1]
    pos = jnp.arange(lc)
    cw = lc * pp

    def lag_rows(d, lags):
        apw = jnp.exp(adt[d][:, :, None] * lags[None, None, :])
        w = (apw[:, :, :, None] * jnp.swapaxes(c_mat[d], 1, 2)[:, :, None, :]).reshape(g, ns, cw)
        bt = jnp.swapaxes(b_bar[d], 1, 2)
        return (jnp.einsum('gqn,gnk->gqk', jnp.real(bt), jnp.real(w), precision=HIGHEST)
                - jnp.einsum('gqn,gnk->gqk', jnp.imag(bt), jnp.imag(w), precision=HIGHEST))

    kk = lag_rows(0, pos.astype(F32))
    kkr = lag_rows(1, pos[::-1].astype(F32))
    fwd = jnp.stack([jnp.pad(kk[..., :cw - pp * i], ((0, 0), (0, 0), (pp * i, 0))) for i in range(lc)], axis=1)
    bwd = jnp.stack([jnp.pad(kkr[..., pp * (lc - 1 - i):], ((0, 0), (0, 0), (0, pp * (lc - 1 - i)))) for i in range(lc)],
                    axis=1)
    tmat = jnp.stack([fwd, bwd], axis=1).reshape(g, nd, cw, cw)
    steps_after = jnp.stack([lc - 1 - pos, pos]).astype(F32)
    gm = jnp.exp(adt[:, :, None, :] * steps_after[:, None, :, None])[:, :, :, None, :] * jnp.swapaxes(b_bar, 2, 3)[:, :, None]
    gm = gm.reshape(nd, g, lc * pp, ns)
    steps_upto = jnp.stack([pos + 1, lc - pos]).astype(F32)
    hm = c_mat[:, :, None] * jnp.exp(adt[:, :, None, :] * steps_upto[:, None, :, None])[:, :, :, None, :]
    hm = jnp.transpose(hm, (0, 1, 4, 2, 3)).reshape(nd, g, ns, lc * pp)
    hmat = jnp.concatenate([jnp.real(hm), -jnp.imag(hm)], axis=2)
    al = jnp.exp(adt * float(lc))
    al = al[:, :, None, :]
    gd = lambda x: jnp.swapaxes(x, 0, 1)
    return (tmat.astype(BF16), gd(jnp.real(gm)).astype(BF16), gd(jnp.imag(gm)).astype(BF16),
            gd(hmat).astype(BF16), gd(jnp.real(al)), gd(jnp.imag(al)))


def _s5_kernel(u_ref, t_ref, gr_ref, gi_ref, h_ref, ar_ref, ai_ref, d_ref, y_ref, er_ref, ei_ref, xr_ref, xi_ref,
               *, nchunk, nctx, rpc):
    ns = S5_STATE
    u = u_ref[...]
    for dr in range(2):
        er_ref[dr] = _dot(u, gr_ref[dr])
        ei_ref[dr] = _dot(u, gi_ref[dr])

    coef = [(jnp.broadcast_to(ar_ref[dr], (rpc, ns)), jnp.broadcast_to(ai_ref[dr], (rpc, ns))) for dr in range(2)]

    def body(c, carry):
        out = []
        for dr in range(2):
            xr, xi = carry[2 * dr], carry[2 * dr + 1]
            pc = c if dr == 0 else jnp.where(c < nctx, nctx - 1 - c, nchunk - 1 + nctx - c)
            off = pl.multiple_of(pc * rpc, rpc)
            xr_ref[dr, pl.ds(off, rpc), :] = xr
            xi_ref[dr, pl.ds(off, rpc), :] = xi
            ar, ai = coef[dr]
            out.append(ar * xr - ai * xi + er_ref[dr, pl.ds(off, rpc), :])
            out.append(ar * xi + ai * xr + ei_ref[dr, pl.ds(off, rpc), :])
        return tuple(out)

    zero = jnp.zeros((rpc, ns), F32)
    lax.fori_loop(0, nchunk, body, (zero, zero, zero, zero), unroll=4)
    y = d_ref[...] * u.astype(F32)
    for dr in range(2):
        y = y + _dot(u, t_ref[dr])
        y = y + _dot(xr_ref[dr].astype(BF16), h_ref[dr, 0:ns, :]) + _dot(xi_ref[dr].astype(BF16), h_ref[dr, ns:2 * ns, :])
    y_ref[...] = y.astype(BF16)


def _group_perm(lc, pp, ngl):
    n = lc * ngl * pp
    src = jnp.arange(n)
    dst = (src // pp) % ngl * (lc * pp) + src // (ngl * pp) * pp + src % pp
    return (dst[:, None] == jnp.arange(n)[None, :]).astype(BF16)


def _s5_pack_kernel(u_ref, perm_ref, o_ref, *, lc, width):
    ngl = LANE // S5_GROUP
    cw = o_ref.shape[-1]
    for gb in range(width // LANE):
        lhs = jnp.concatenate([u_ref[:, i * width + gb * LANE:i * width + (gb + 1) * LANE] for i in range(lc)], axis=1)
        res = _dot(lhs, perm_ref[...]).astype(BF16)
        for gl in range(ngl):
            o_ref[gb * ngl + gl] = res[:, gl * cw:(gl + 1) * cw]


def _s5_unpack_glu_kernel(y_ref, perm_ref, w_ref, b_ref, o_ref, *, lc, width):
    ngl = LANE // S5_GROUP
    pieces = []
    for gb in range(width // LANE):
        lhs = jnp.concatenate([y_ref[gb * ngl + gl] for gl in range(ngl)], axis=1)
        pieces.append(_dot(lhs, perm_ref[...]))
    for i in range(lc):
        y = jnp.concatenate([p[:, i * LANE:(i + 1) * LANE] for p in pieces], axis=1)
        y = 0.5 * y * (1.0 + jnp.tanh(math.sqrt(2.0 / math.pi) * (y + 0.044715 * (y * y * y))))
        z = _dot(y.astype(BF16), w_ref[...]) + b_ref[...]
        o_ref[:, i * width:(i + 1) * width] = (y * jax.nn.sigmoid(z)).astype(BF16)


def _s5(p, ucol, width, ctx_len, ops, d_skip, w_glu, b_glu):
    tmat, g_re, g_im, hmat, a_re, a_im = ops
    b, ta, _ = p.shape
    lc, pp, ns = S5_CHUNK, S5_GROUP, S5_STATE
    g = width // pp
    ngl = LANE // pp
    nchunk = ta // lc
    rpc = 8
    cw = lc * pp
    rows = nchunk * rpc
    perm = _group_perm(lc, pp, ngl)
    nperm = perm.shape[0]
    u2 = p[..., ucol:ucol + width].astype(BF16).reshape(b, nchunk, lc * width)
    ug = pl.pallas_call(
        functools.partial(_s5_pack_kernel, lc=lc, width=width),
        out_shape=jax.ShapeDtypeStruct((g, b, nchunk, cw), BF16),
        grid=(b,),
        in_specs=[pl.BlockSpec((None, nchunk, lc * width), lambda i: (i, 0, 0)),
                  pl.BlockSpec((nperm, nperm), lambda i: (0, 0))],
        out_specs=pl.BlockSpec((g, None, nchunk, cw), lambda i: (0, i, 0, 0)),
        compiler_params=_cparams("parallel"),
        name="s5_pack",
    )(u2, perm)
    ug = jnp.pad(jnp.swapaxes(ug, 1, 2), ((0, 0), (0, 0), (0, rpc - b), (0, 0))).reshape(g, rows, cw)
    dvec = jnp.tile(d_skip.astype(F32).reshape(g, 1, pp), (1, lc, 1)).reshape(g, 1, cw)
    op = lambda r, c: pl.BlockSpec((None, 2, r, c), lambda i: (i, 0, 0, 0))
    y = pl.pallas_call(
        functools.partial(_s5_kernel, nchunk=nchunk, nctx=ctx_len // lc, rpc=rpc),
        out_shape=jax.ShapeDtypeStruct((g, rows, cw), BF16),
        grid=(g,),
        in_specs=[pl.BlockSpec((None, rows, cw), lambda i: (i, 0, 0)),
                  op(cw, cw), op(cw, ns), op(cw, ns), op(2 * ns, cw), op(1, ns), op(1, ns),
                  pl.BlockSpec((None, 1, cw), lambda i: (i, 0, 0))],
        out_specs=pl.BlockSpec((None, rows, cw), lambda i: (i, 0, 0)),
        scratch_shapes=[pltpu.VMEM((2, rows, ns), F32) for _ in range(4)],
        compiler_params=_cparams("parallel"),
        name="s5_scan",
    )(ug, tmat, g_re, g_im, hmat, a_re, a_im, dvec)
    y = jnp.swapaxes(y.reshape(g, nchunk, rpc, cw)[:, :, :b], 1, 2)
    out = pl.pallas_call(
        functools.partial(_s5_unpack_glu_kernel, lc=lc, width=width),
        out_shape=jax.ShapeDtypeStruct((b, nchunk, lc * width), BF16),
        grid=(b,),
        in_specs=[pl.BlockSpec((g, None, nchunk, cw), lambda i: (0, i, 0, 0)),
                  pl.BlockSpec((nperm, nperm), lambda i: (0, 0)),
                  pl.BlockSpec((width, width), lambda i: (0, 0)),
                  pl.BlockSpec((1, width), lambda i: (0, 0))],
        out_specs=pl.BlockSpec((None, nchunk, lc * width), lambda i: (i, 0, 0)),
        compiler_params=_cparams("parallel"),
        name="s5_unpack_glu",
    )(y, perm.T, w_glu.astype(BF16), b_glu.reshape(1, width))
    return out.reshape(b, ta, width)


def _postmix_kernel(a_ref, b_ref, w_ref, *refs, nx):
    m_ref, gpost_ref, gpre_ref, wr_ref, xo_ref, h_ref, lg_ref = refs[nx:]
    wa = a_ref.shape[-1]
    y = _dot(a_ref[...], w_ref[0:wa, :]) + _dot(b_ref[...], w_ref[wa:, :])
    m = m_ref[...]
    xn = _stream_tile(refs[:nx]) + m[2:3] * (_rms(y) * gpost_ref[...])
    xo_ref[...] = xn
    h2 = _rms(xn) * gpre_ref[...] * (1.0 + m[4:5]) + m[3:4]
    hi = h2.astype(BF16)
    h_ref[...] = hi
    lo = (h2 - hi.astype(F32)).astype(BF16)
    ne = lg_ref.shape[-1]
    both = _dot(hi, wr_ref[...])
    lg_ref[...] = both[:, 0:ne] + both[:, ne:2 * ne] + _dot(lo, wr_ref[:, 0:ne])


def _postmix(mix_a, mix_b, w_out, xs, modtab, g_post, g_pre, w_router):
    (b, ta, d), x_specs, x_args = _stream_operands(xs)
    wa, wb = mix_a.shape[-1], mix_b.shape[-1]
    ne = w_router.shape[-1]
    tok = lambda w: pl.BlockSpec((None, TOK_TILE, w), lambda i, j: (i, j, 0))
    vec = pl.BlockSpec((1, d), lambda i, j: (0, 0))
    wr_hi = w_router.astype(BF16)
    wr2 = jnp.concatenate([wr_hi, (w_router - wr_hi.astype(F32)).astype(BF16)], axis=1)
    return pl.pallas_call(
        functools.partial(_postmix_kernel, nx=len(x_args)),
        out_shape=(jax.ShapeDtypeStruct((b, ta, d), F32),
                   jax.ShapeDtypeStruct((b, ta, d), BF16),
                   jax.ShapeDtypeStruct((b, ta, ne), F32)),
        grid=(b, ta // TOK_TILE),
        in_specs=[tok(wa), tok(wb),
                  pl.BlockSpec((wa + wb, d), lambda i, j: (0, 0))] + x_specs + [
                  _mod_spec(d), vec, vec,
                  pl.BlockSpec((d, 2 * ne), lambda i, j: (0, 0))],
        out_specs=(tok(d), tok(d), tok(ne)),
        compiler_params=_cparams("parallel", "parallel"),
        name="mix_out",
    )(mix_a, mix_b, w_out.astype(BF16), *x_args, modtab, g_post.reshape(1, d), g_pre.reshape(1, d), wr2)


def _select_kernel(lg_ref, pos_ref, gate_ref, *, ctx_len, cap_ctx, cap_lat):
    lg = lg_ref[...]
    ne, ta = lg.shape
    ex = jnp.exp(lg - jnp.max(lg, axis=0, keepdims=True))
    probs = ex / jnp.sum(ex, axis=0, keepdims=True)
    bits = pltpu.bitcast(probs, jnp.int32)
    is_ctx = lax.broadcasted_iota(jnp.int32, (ne, ta), 1) < ctx_len

    def counts(mask):
        mf = jnp.where(mask, 1.0, 0.0)
        return (jnp.sum(jnp.where(is_ctx, mf, 0.0), axis=1, keepdims=True),
                jnp.sum(jnp.where(is_ctx, 0.0, mf), axis=1, keepdims=True))

    def search(i, carry):
        pc, pt = carry
        bit = jnp.left_shift(jnp.int32(1), 30 - i)
        cc, ct = counts(bits >= jnp.where(is_ctx, pc | bit, pt | bit))
        return jnp.where(cc >= cap_ctx, pc | bit, pc), jnp.where(ct >= cap_lat, pt | bit, pt)

    z = jnp.zeros((ne, 1), jnp.int32)
    pc, pt = lax.fori_loop(0, 31, search, (z, z))
    thr = jnp.where(is_ctx, pc, pt)
    gt = bits > thr
    eq = bits == thr
    gc, gl = counts(gt)
    need = jnp.where(is_ctx, cap_ctx - gc, cap_lat - gl)

    nb = ta // LANE
    ut = jnp.where(lax.broadcasted_iota(jnp.int32, (LANE, LANE), 0) <= lax.broadcasted_iota(jnp.int32, (LANE, LANE), 1),
                   1.0, 0.0).astype(BF16)

    def lane_prefix(mask):
        mf = jnp.where(mask, 1.0, 0.0).astype(BF16)
        blocks = jnp.concatenate([mf[:, j * LANE:(j + 1) * LANE] for j in range(nb)], axis=0)
        inc = _dot(blocks, ut)
        outs = []
        off = jnp.zeros((ne, 1), F32)
        for j in range(nb):
            if j * LANE == ctx_len:
                off = jnp.zeros((ne, 1), F32)
            blk = inc[j * ne:(j + 1) * ne]
            outs.append(blk + off)
            off = off + blk[:, LANE - 1:LANE]
        return jnp.concatenate(outs, axis=1)

    sel = gt | (eq & (lane_prefix(eq) <= need))
    slot = lane_prefix(sel) - 1.0 + jnp.where(is_ctx, 0.0, float(cap_ctx))
    pos_ref[...] = jnp.where(sel, slot, -1.0)
    gate_ref[...] = jnp.where(sel, probs, 0.0)


def _moe_select(logits_t, ctx_len, cap_ctx, cap_lat):
    b, ne, ta = logits_t.shape
    spec = pl.BlockSpec((None, ne, ta), lambda i: (i, 0, 0))
    return pl.pallas_call(
        functools.partial(_select_kernel, ctx_len=ctx_len, cap_ctx=cap_ctx, cap_lat=cap_lat),
        out_shape=(jax.ShapeDtypeStruct((b, ne, ta), F32), jax.ShapeDtypeStruct((b, ne, ta), F32)),
        grid=(b,),
        in_specs=[spec],
        out_specs=(spec, spec),
        compiler_params=_cparams("parallel"),
        name="moe_select",
    )(logits_t)


def _slot_windows(pos_t, nt, win, nslot):
    b, ne, ta = pos_t.shape
    pt = pos_t.reshape(b, ne, nt, ta // nt)
    hi = jnp.max(pt, axis=-1).astype(jnp.int32) + 1
    lo = jnp.min(jnp.where(pt >= 0, pt, float(nslot)), axis=-1).astype(jnp.int32)
    lo = jnp.where(hi > 0, lo // 16 * 16, 0)
    npass = jnp.maximum(jnp.max((hi - lo + win - 1) // win, axis=1), 1)
    return jnp.transpose(lo, (0, 2, 1)).reshape(-1), npass.reshape(-1)


def _gather_kernel(start_ref, npass_ref, pos_ref, h_ref, o_ref):
    i = pl.program_id(0)
    eg = pl.program_id(1)
    j = pl.program_id(2)
    ge, tk = pos_ref.shape
    ne = pl.num_programs(1) * ge
    nslot = o_ref.shape[1]
    win = GATHER_WIN
    tile = i * pl.num_programs(2) + j

    @pl.when(j == 0)
    def _():
        o_ref[...] = jnp.zeros_like(o_ref)

    pos = pos_ref[...].astype(jnp.int32)
    slot = lax.broadcasted_iota(jnp.int32, (win, tk), 0)

    def one_pass(p, carry):
        src, blocks = [], []
        for el in range(ge):
            lo = start_ref[tile * ne + eg * ge + el] + p * win
            src.append(pl.multiple_of(jnp.minimum(lo, nslot - win), 16))
            pe = pos[el:el + 1, :]
            hit = (pe >= lo) & (pe < lo + win) & (slot == pe - src[el])
            blocks.append(jnp.where(hit, 1.0, 0.0).astype(BF16))
        part = _dot(jnp.concatenate(blocks, axis=0), h_ref[...]).astype(BF16)
        for el in range(ge):
            o_ref[el, pl.ds(src[el], win), :] += part[el * win:(el + 1) * win]
        return carry

    lax.fori_loop(0, npass_ref[tile], one_pass, 0)


def _moe_gather(pos_t, h, nslot):
    b, ne, ta = pos_t.shape
    d = h.shape[-1]
    ge = 8
    tk = TOK_TILE
    nt = ta // tk
    start, npass = _slot_windows(pos_t, nt, GATHER_WIN, nslot)
    return pl.pallas_call(
        _gather_kernel,
        out_shape=jax.ShapeDtypeStruct((ne, b, nslot, d), BF16),
        grid_spec=pltpu.PrefetchScalarGridSpec(
            num_scalar_prefetch=2,
            grid=(b, ne // ge, nt),
            in_specs=[pl.BlockSpec((None, None, ge, tk), lambda i, g, j, *_: (i, g, 0, j)),
                      pl.BlockSpec((None, tk, d), lambda i, g, j, *_: (i, j, 0))],
            out_specs=pl.BlockSpec((ge, None, nslot, d), lambda i, g, j, *_: (g, i, 0, 0))),
        compiler_params=_cparams("arbitrary", "arbitrary", "arbitrary"),
        name="moe_gather",
    )(start, npass, pos_t.reshape(b, ne // ge, ge, ta), h)


def _ffn_kernel(x_ref, wg_ref, wu_ref, wd_ref, o_ref, hid_ref, *, nf):
    s = pl.program_id(2)
    tf = wg_ref.shape[-1]

    @pl.when(s < nf)
    def _():
        x = x_ref[...]
        g = _dot(x, wg_ref[...].astype(BF16))
        u = _dot(x, wu_ref[...].astype(BF16))
        hid_ref[s] = (_silu(g) * u).astype(BF16)

    @pl.when(s >= nf)
    def _():
        acc = _dot(hid_ref[0], wd_ref[0:tf, :].astype(BF16))
        for f in range(1, nf):
            acc = acc + _dot(hid_ref[f], wd_ref[f * tf:(f + 1) * tf, :].astype(BF16))
        o_ref[...] = acc.astype(BF16)


def _moe_ffn(xin, w_gate, w_up, w_down, layer, tf=256):
    ne, ns, m, d = xin.shape
    ff = w_gate.shape[-1]
    nf = ff // tf
    nd = d // tf
    up = lambda e, s, f: (layer, e, 0, jnp.minimum(f, nf - 1))
    return pl.pallas_call(
        functools.partial(_ffn_kernel, nf=nf),
        out_shape=jax.ShapeDtypeStruct((ne, ns, m, d), BF16),
        grid=(ne, ns, nf + nd),
        in_specs=[pl.BlockSpec((None, None, m, d), lambda e, s, f: (e, s, 0, 0)),
                  pl.BlockSpec((None, None, d, tf), up),
                  pl.BlockSpec((None, None, d, tf), up),
                  pl.BlockSpec((None, None, ff, tf), lambda e, s, f: (layer, e, 0, jnp.maximum(f - nf, 0)))],
        out_specs=pl.BlockSpec((None, None, m, tf), lambda e, s, f: (e, s, 0, jnp.maximum(f - nf, 0))),
        scratch_shapes=[pltpu.VMEM((nf, m, tf), BF16)],
        compiler_params=_cparams("parallel", "parallel", "arbitrary"),
        name="moe_ffn",
    )(xin, w_gate, w_up, w_down)


def _combine_kernel(start_ref, npass_ref, pos_ref, gate_ref, y_hbm, x_ref, m_ref, g_ref, *refs,
                    ctx_tiles, skip, with_next):
    if with_next:
        gn_ref, mn_ref, o_ref, hn_ref, ybuf, acc_ref, sem = refs
    else:
        o_ref, ybuf, acc_ref, sem = refs
    tt, ne = pos_ref.shape
    nslot = y_hbm.shape[2]
    win = COMBINE_WIN
    nj = pl.num_programs(1)
    step = pl.program_id(0) * nj + pl.program_id(1)
    buf = step % 2
    pos = pos_ref[...].astype(jnp.int32)
    gate = gate_ref[...]
    lane = lax.broadcasted_iota(jnp.int32, (tt, win), 1)

    def windows(s, p):
        smp = s // nj
        tile = smp * (nj + skip) + s % nj + skip
        lo = [start_ref[tile * ne + e] + p * win for e in range(ne)]
        return smp, lo, [pl.multiple_of(jnp.minimum(v, nslot - win), 16) for v in lo]

    def window_copy(smp, e, src, slot):
        return pltpu.make_async_copy(y_hbm.at[e, smp, pl.ds(src, win), :], ybuf.at[slot, pl.ds(e * win, win), :],
                                     sem.at[slot, e])

    def fetch(s, p, slot):
        smp, _, src = windows(s, p)
        for e in range(ne):
            window_copy(smp, e, src[e], slot).start()

    def one_pass(p):
        smp, lo, src = windows(step, p)
        blocks = []
        for e in range(ne):
            pe = pos[:, e:e + 1]
            hit = (pe >= lo[e]) & (pe < lo[e] + win) & (lane == pe - src[e])
            blocks.append(jnp.where(hit, gate[:, e:e + 1], 0.0).astype(BF16))
        w = jnp.concatenate(blocks, axis=1)
        for e in range(ne):
            window_copy(smp, e, src[e], buf).wait()
        return _dot(w, ybuf[buf])

    @pl.when(step == 0)
    def _():
        fetch(step, 0, buf)

    @pl.when(step + 1 < pl.num_programs(0) * nj)
    def _():
        fetch(step + 1, 0, 1 - buf)

    acc_ref[...] = one_pass(0)

    def extra(p, carry):
        fetch(step, p, buf)
        acc_ref[...] += one_pass(p)
        return carry

    lax.fori_loop(1, npass_ref[windows(step, 0)[0] * (nj + skip) + step % nj + skip], extra, 0)
    m = m_ref[...]
    is_ctx = pl.program_id(1) + skip < ctx_tiles
    pick = lambda k: jnp.where(is_ctx, m[0, k:k + 1], m[1, k:k + 1])
    xn = x_ref[...] + pick(5) * (_rms(acc_ref[...]) * g_ref[...])
    o_ref[...] = xn
    if with_next:
        mn = mn_ref[...]
        pick_n = lambda k: jnp.where(is_ctx, mn[0, k:k + 1], mn[1, k:k + 1])
        hn_ref[...] = (_rms(xn) * gn_ref[...] * (1.0 + pick_n(1)) + pick_n(0)).astype(BF16)


def _moe_combine(pos_t, gate_t, yout, xa, modtab, g_post, ctx_len, nxt=None, latent_only=False):
    b, ta, d = xa.shape
    ne, _, nslot, _ = yout.shape
    tt = TOK_TILE
    nt = ta // tt
    skip = ctx_len // tt if latent_only else 0
    win = COMBINE_WIN
    start, npass = _slot_windows(pos_t, nt, win, nslot)
    tok = lambda w: pl.BlockSpec((None, tt, w), lambda i, j, *_: (i, j + skip, 0))
    out_tok = lambda: pl.BlockSpec((None, tt, d), lambda i, j, *_: (i, j, 0))
    mods = pl.BlockSpec((None, 2, 6, d), lambda i, j, *_: (i, 0, 0, 0))
    vec = pl.BlockSpec((1, d), lambda i, j, *_: (0, 0))
    xo = jax.ShapeDtypeStruct((b, ta - skip * tt, d), F32)
    extra_in, extra_args = ([vec, mods], [nxt[0].reshape(1, d), nxt[1]]) if nxt else ([], [])
    return pl.pallas_call(
        functools.partial(_combine_kernel, ctx_tiles=ctx_len // tt, skip=skip, with_next=bool(nxt)),
        out_shape=(xo, jax.ShapeDtypeStruct((b, ta - skip * tt, d), BF16)) if nxt else xo,
        grid_spec=pltpu.PrefetchScalarGridSpec(
            num_scalar_prefetch=2,
            grid=(b, nt - skip),
            in_specs=[tok(ne), tok(ne),
                      pl.BlockSpec(memory_space=pl.ANY),
                      tok(d), mods, vec] + extra_in,
            out_specs=(out_tok(), out_tok()) if nxt else out_tok(),
            scratch_shapes=[pltpu.VMEM((2, ne * win, d), BF16),
                            pltpu.VMEM((tt, d), F32),
                            pltpu.SemaphoreType.DMA((2, ne))]),
        compiler_params=_cparams("arbitrary", "arbitrary"),
        name="moe_combine",
    )(start, npass, jnp.swapaxes(pos_t, 1, 2), jnp.swapaxes(gate_t, 1, 2), yout, xa, modtab,
      g_post.reshape(1, d), *extra_args)


def _ec_moe(logits_t, h, xa, modtab, g_post, w_gate, w_up, w_down, layer, ctx_len, nxt=None, latent_only=False):
    b, ne, ta = logits_t.shape
    d = h.shape[-1]
    cap_ctx = max(1, EC_CAPACITY * ctx_len // ne)
    cap_lat = max(1, EC_CAPACITY * (ta - ctx_len) // ne)
    nslot = cap_ctx + cap_lat
    pos_t, gate_t = _moe_select(logits_t, ctx_len, cap_ctx, cap_lat)
    xin = _moe_gather(pos_t, h, nslot)
    pair = b
    yout = _moe_ffn(xin.reshape(ne, b // pair, pair * nslot, d), w_gate, w_up, w_down, layer)
    yout = yout.reshape(ne, b, nslot, d)
    return _moe_combine(pos_t, gate_t, yout, xa, modtab, g_post, ctx_len, nxt, latent_only)


def kernel(x, c, ctx, c_ctx, w_mod, b_mod, g_mix_pre, g_mix_post, g_ffn_pre, g_ffn_post, w_router, w_exp_gate, w_exp_up, w_exp_down, ev_w_in, ev_w_out, hgrn_lb, hgrn_g_norm, na_rpb, od_w_in, od_w_out, q_norm, k_norm, s5_a_re, s5_a_im, s5_log_dt, s5_b_re, s5_b_im, s5_c_re, s5_c_im, s5_d, s5_w_glu, s5_b_glu):
    b, t, d = x.shape
    ctx_len = ctx.shape[1]
    depth = w_mod.shape[0]
    assert depth == 2 and b <= 7
    ta = ctx_len + t
    a_width = d // 2
    s5_width = d // 4
    cq_width = d - s5_width
    ckv_width = cq_width // 3

    cc = jnp.concatenate([c, c_ctx[None], jnp.zeros((7 - b, d), F32)], axis=0)
    mod = _modulation(cc, w_mod, b_mod)
    mod_lat = mod[:, :b].reshape(depth, b, 1, 6, d)
    mod_ctx = jnp.broadcast_to(mod[:, b].reshape(depth, 1, 1, 6, d), (depth, b, 1, 6, d))
    modtab = jnp.concatenate([mod_ctx, mod_lat], axis=2)

    lb_all = jnp.cumsum(jax.nn.softmax(hgrn_lb.astype(F32), axis=0), axis=0)
    hx = _prenorm((ctx, x), g_mix_pre[0], modtab[0])
    p = _matmul(hx.reshape(b * ta, d), ev_w_in[0]).reshape(b, ta, -1)
    o_f, o_b = _hgrn(p, lb_all[0], ctx_len, a_width)
    mix_a = _hgrn_readout(o_f, o_b, p, hgrn_g_norm[0], a_width)
    mix_b = _natten(p, _natten_bias(na_rpb[0]), ctx_len, 5 * a_width, d - a_width)
    xa, h2, logits = _postmix(mix_a, mix_b, ev_w_out[0], (ctx, x), modtab[0], g_mix_post[0], g_ffn_pre[0], w_router[0])
    xa, hx = _ec_moe(jnp.swapaxes(logits, 1, 2), h2, xa, modtab[0], g_ffn_post[0], w_exp_gate, w_exp_up, w_exp_down, 0,
                     ctx_len, nxt=(g_mix_pre[1], modtab[1]))

    p = _matmul(hx.reshape(b * ta, d), od_w_in[0]).reshape(b, ta, -1)
    cosf, sinf = _rope_tables(ctx_len, t)
    mix_a = _gqa_attention(p, cosf, sinf, q_norm[0], k_norm[0], cq_width, ckv_width, ctx_len)
    ops = _s5_operators(s5_a_re[0], s5_a_im[0], s5_log_dt[0], s5_b_re[0], s5_b_im[0], s5_c_re[0], s5_c_im[0])
    mix_b = _s5(p, cq_width + 2 * ckv_width, s5_width, ctx_len, ops, s5_d[0], s5_w_glu[0], s5_b_glu[0])
    xa, h2, logits = _postmix(mix_a, mix_b, od_w_out[0], xa, modtab[1], g_mix_post[1], g_ffn_pre[1], w_router[1])
    return _ec_moe(jnp.swapaxes(logits, 1, 2), h2, xa, modtab[1], g_ffn_post[1], w_exp_gate, w_exp_up, w_exp_down, 1,
                   ctx_len, latent_only=True)
```
